```python
import math
import jax, jax.numpy as jnp
from jax import lax
import numpy as np

D_MODEL = 2048
BATCH = 2
SEQ = 4096
DEPTH = 1

D_MIX = D_MODEL
CONF_WIDTH = D_MIX // 2
HYENA_WIDTH = D_MIX - CONF_WIDTH
CONF_KERNEL = 31
SHORT_KERNEL = 3
HYENA_ORDER = 2
HYENA_EMB = 33
HYENA_FILTER_ORDER = 64
HYENA_INNER_MLPS = 2
HYENA_FAST_DECAY = 0.3
HYENA_SLOW_DECAY = 1.5
HYENA_TARGET = 1e-2
N_MEM = 256
XATTN_HEADS = 4
XATTN_HEAD_DIM = D_MODEL // XATTN_HEADS
D_FF = 5632
FFN_KERNEL = 3
EPS = 1e-6
IN_COLS = 2 * CONF_WIDTH + (HYENA_ORDER + 1) * HYENA_WIDTH

kernel_name = "hybrid_conformer_hyena_encoder_block"

F32 = jnp.float32


def rms_norm(x, g):
    xf = x.astype(F32)
    y = xf * lax.rsqrt(jnp.mean(xf * xf, axis=-1, keepdims=True) + EPS)
    return (y * g.astype(F32)).astype(x.dtype)


def layer_norm(x, g, b):
    xf = x.astype(F32)
    xc = xf - jnp.mean(xf, axis=-1, keepdims=True)
    y = xc * lax.rsqrt(jnp.mean(xc * xc, axis=-1, keepdims=True) + EPS)
    return (y * g.astype(F32) + b.astype(F32)).astype(x.dtype)


def depthwise_conv_centred(u, w, b):
    pad = w.shape[0] // 2
    y = lax.conv_general_dilated(
        u, w[:, None, :].astype(u.dtype), window_strides=(1,), padding=[(pad, pad)],
        dimension_numbers=("NWC", "WIO", "NWC"), feature_group_count=u.shape[-1])
    return y + b.astype(u.dtype)


def conformer_conv_group(p, dw_w, dw_b, ln_g, ln_b):
    a, gate = jnp.split(p, 2, axis=-1)
    u = a * jax.nn.sigmoid(gate)
    u = depthwise_conv_centred(u, dw_w, dw_b)
    u = layer_norm(u, ln_g, ln_b)
    return jax.nn.silu(u)


def hyena_position_features(length):
    t = jnp.linspace(0.0, 1.0, length, dtype=F32)[:, None]
    w = 2.0 * math.pi * jnp.arange(length, dtype=F32)[:, None] / length
    bands = (HYENA_EMB - 1) // 2
    f = jnp.linspace(1e-4, bands - 1, bands, dtype=F32)[None, :]
    z = jnp.concatenate([t, jnp.cos(f * w), -jnp.sin(f * w)], axis=-1)
    return t, z


def hyena_filters(length, w1, b1, w_inner, b_inner, w_filt, sin_freq):
    t, z = hyena_position_features(length)
    freq = sin_freq.astype(F32)
    h = jnp.sin(freq * (z @ w1.astype(F32) + b1.astype(F32)))
    for i in range(HYENA_INNER_MLPS):
        h = jnp.sin(freq * (h @ w_inner[i].astype(F32) + b_inner[i].astype(F32)))
    h = (h @ w_filt.astype(F32)).reshape(length, HYENA_ORDER, 2, HYENA_WIDTH)
    max_decay = math.log(HYENA_TARGET) / HYENA_FAST_DECAY
    min_decay = math.log(HYENA_TARGET) / HYENA_SLOW_DECAY
    deltas = jnp.abs(jnp.linspace(min_decay, max_decay, HYENA_WIDTH, dtype=F32))
    h = h * jnp.exp(-t[:, :, None, None] * deltas)
    k_fwd = h[:, :, 0]
    k_bwd = h[:, :, 1]
    k2 = jnp.concatenate(
        [k_fwd, jnp.zeros((1, HYENA_ORDER, HYENA_WIDTH), F32), k_bwd[: length - 1][::-1]], axis=0)
    return k2 * lax.rsqrt(jnp.sum(k2 * k2, axis=0, keepdims=True) + EPS)


def fft_conv_two_sided(u, k2, skip):
    length = u.shape[1]
    uf = jnp.fft.rfft(u.astype(F32), n=2 * length, axis=1)
    kf = jnp.fft.rfft(k2, n=2 * length, axis=0)
    y = jnp.fft.irfft(uf * kf[None], n=2 * length, axis=1)[:, :length]
    return (y + u.astype(F32) * skip.astype(F32)).astype(u.dtype)


def hyena_group(p, short_w, short_b, k2, skip):
    uc = depthwise_conv_centred(p, short_w, short_b)
    chunks = jnp.split(uc, HYENA_ORDER + 1, axis=-1)
    z = chunks[0]
    for o in range(HYENA_ORDER):
        z = chunks[o + 1] * fft_conv_two_sided(z, k2[:, o], skip[o])
    return z


def cross_attention(h, m, w_q, w_k, w_v, w_o):
    b, s, _ = h.shape
    n_mem = m.shape[1]
    q = (h @ w_q).reshape(b, s, XATTN_HEADS, XATTN_HEAD_DIM)
    k = (m @ w_k).reshape(b, n_mem, XATTN_HEADS, XATTN_HEAD_DIM)
    v = (m @ w_v).reshape(b, n_mem, XATTN_HEADS, XATTN_HEAD_DIM)
    scores = jnp.einsum("bshd,bmhd->bhsm", q, k).astype(F32) * (XATTN_HEAD_DIM ** -0.5)
    probs = jax.nn.softmax(scores, axis=-1).astype(v.dtype)
    o = jnp.einsum("bhsm,bmhd->bshd", probs, v).reshape(b, s, D_MODEL)
    return o @ w_o


def conv_glu_ffn(h, w_up, dw_w, dw_b, w_down):
    gate, val = jnp.split(h @ w_up, 2, axis=-1)
    gate = depthwise_conv_centred(gate, dw_w, dw_b)
    return (jax.nn.silu(gate) * val) @ w_down


def setup_inputs(seed: int = 0) -> dict:
    key = jax.random.key(seed)
    ks = iter(jax.random.split(key, 40))

    def nrm(shape, scale):
        return jax.random.normal(next(ks), shape, F32) * scale

    def gain(shape):
        return 1.0 + nrm(shape, 0.02)

    L = DEPTH
    return {
        "x": nrm((BATCH, SEQ, D_MODEL), 1.0),
        "mem": nrm((BATCH, N_MEM, D_MODEL), 1.0),
        "g_mix": gain((L, D_MODEL)),
        "w_in": nrm((L, D_MODEL, IN_COLS), D_MODEL ** -0.5),
        "conf_dw_w": nrm((L, CONF_KERNEL, CONF_WIDTH), CONF_KERNEL ** -0.5),
        "conf_dw_b": nrm((L, CONF_WIDTH), 0.02),
        "conf_ln_g": gain((L, CONF_WIDTH)),
        "conf_ln_b": nrm((L, CONF_WIDTH), 0.02),
        "hyena_short_w": nrm((L, SHORT_KERNEL, (HYENA_ORDER + 1) * HYENA_WIDTH), SHORT_KERNEL ** -0.5),
        "hyena_short_b": nrm((L, (HYENA_ORDER + 1) * HYENA_WIDTH), 0.02),
        "hyena_w1": nrm((L, HYENA_EMB, HYENA_FILTER_ORDER), HYENA_EMB ** -0.5),
        "hyena_b1": nrm((L, HYENA_FILTER_ORDER), 0.02),
        "hyena_w_inner": nrm((L, HYENA_INNER_MLPS, HYENA_FILTER_ORDER, HYENA_FILTER_ORDER), HYENA_FILTER_ORDER ** -0.5),
        "hyena_b_inner": nrm((L, HYENA_INNER_MLPS, HYENA_FILTER_ORDER), 0.02),
        "hyena_w_filt": nrm((L, HYENA_FILTER_ORDER, HYENA_ORDER * 2 * HYENA_WIDTH), HYENA_FILTER_ORDER ** -0.5),
        "hyena_sin_freq": gain((L, HYENA_FILTER_ORDER)),
        "hyena_skip": nrm((L, HYENA_ORDER, HYENA_WIDTH), 0.1),
        "w_out": nrm((L, D_MIX, D_MODEL), D_MIX ** -0.5),
        "g_xattn": gain((L, D_MODEL)),
        "g_mem": gain((L, D_MODEL)),
        "w_q": nrm((L, D_MODEL, D_MODEL), D_MODEL ** -0.5),
        "w_k": nrm((L, D_MODEL, D_MODEL), D_MODEL ** -0.5),
        "w_v": nrm((L, D_MODEL, D_MODEL), D_MODEL ** -0.5),
        "w_o": nrm((L, D_MODEL, D_MODEL), D_MODEL ** -0.5),
        "g_ffn": gain((L, D_MODEL)),
        "w_ffn_up": nrm((L, D_MODEL, 2 * D_FF), D_MODEL ** -0.5),
        "ffn_dw_w": nrm((L, FFN_KERNEL, D_FF), FFN_KERNEL ** -0.5),
        "ffn_dw_b": nrm((L, D_FF), 0.02),
        "w_ffn_down": nrm((L, D_FF, D_MODEL), D_FF ** -0.5),
        "g_final": gain((D_MODEL,)),
    }


def reference(x, mem, g_mix, w_in, conf_dw_w, conf_dw_b, conf_ln_g, conf_ln_b,
              hyena_short_w, hyena_short_b, hyena_w1, hyena_b1, hyena_w_inner, hyena_b_inner,
              hyena_w_filt, hyena_sin_freq, hyena_skip, w_out, g_xattn, g_mem,
              w_q, w_k, w_v, w_o, g_ffn, w_ffn_up, ffn_dw_w, ffn_dw_b, w_ffn_down, g_final):
    length = x.shape[1]
    for l in range(DEPTH):
        p = rms_norm(x, g_mix[l]) @ w_in[l]
        p_conf = p[..., : 2 * CONF_WIDTH]
        p_hyena = p[..., 2 * CONF_WIDTH:]
        y_conf = conformer_conv_group(p_conf, conf_dw_w[l], conf_dw_b[l], conf_ln_g[l], conf_ln_b[l])
        k2 = hyena_filters(length, hyena_w1[l], hyena_b1[l], hyena_w_inner[l], hyena_b_inner[l],
                           hyena_w_filt[l], hyena_sin_freq[l])
        y_hyena = hyena_group(p_hyena, hyena_short_w[l], hyena_short_b[l], k2, hyena_skip[l])
        x = x + jnp.concatenate([y_conf, y_hyena], axis=-1) @ w_out[l]
        x = x + cross_attention(rms_norm(x, g_xattn[l]), rms_norm(mem, g_mem[l]),
                                w_q[l], w_k[l], w_v[l], w_o[l])
        x = x + conv_glu_ffn(rms_norm(x, g_ffn[l]), w_ffn_up[l], ffn_dw_w[l], ffn_dw_b[l], w_ffn_down[l])
    return rms_norm(x, g_final)
```

```python
import functools
import math

import numpy as np
import jax
import jax.numpy as jnp
from jax import lax
from jax.experimental import pallas as pl
from jax.experimental.pallas import tpu as pltpu

F32 = jnp.float32
BF16 = jnp.bfloat16

D_MODEL = 2048
BATCH = 2
SEQ = 4096
CONF_WIDTH = 1024
HYENA_WIDTH = 1024
CONF_KERNEL = 31
SHORT_KERNEL = 3
HYENA_ORDER = 2
HYENA_EMB = 33
HYENA_FILTER_ORDER = 64
HYENA_INNER_MLPS = 2
HYENA_FAST_DECAY = 0.3
HYENA_SLOW_DECAY = 1.5
HYENA_TARGET = 1e-2
N_MEM = 256
XATTN_HEADS = 4
XATTN_HEAD_DIM = D_MODEL // XATTN_HEADS
D_FF = 5632
EPS = 1e-6
IN_COLS = 2 * CONF_WIDTH + (HYENA_ORDER + 1) * HYENA_WIDTH

LANES = 128
VMEM_LIMIT = 56 * 1024 * 1024

N_FFT = 2 * SEQ
FFT_N1 = 128
FFT_N2 = 64
FFT_NZ1 = SEQ // FFT_N2
FFT_PITCH = 2 * FFT_N1 + 8


def _params(*sem):
    return pltpu.CompilerParams(dimension_semantics=sem, vmem_limit_bytes=VMEM_LIMIT)


def _const_spec(shape):
    zeros = (0,) * len(shape)
    return pl.BlockSpec(shape, lambda *_: zeros, pipeline_mode=pl.Buffered(1))


@functools.lru_cache(maxsize=None)
def _dft_constants():
    k1 = np.arange(FFT_N1)
    f1 = np.exp(-2j * np.pi * np.outer(k1, k1) / FFT_N1)
    fh = f1[:, :FFT_NZ1]
    fa = np.block([[fh.real, -fh.imag], [fh.imag, fh.real]])
    perm = np.concatenate([np.arange(FFT_NZ1), FFT_N1 - 1 - np.arange(FFT_N1 - FFT_NZ1)])
    fp = f1[:, perm]
    fak = np.concatenate([fp.real, fp.imag], axis=0)
    n2 = np.arange(FFT_N2)
    kk = k1[:, None, None] + FFT_N1 * n2[None, :, None]
    ang = (n2[None, None, :] * kk) % N_FFT
    tc = np.exp(-2j * np.pi * ang / N_FFT)
    t = np.concatenate([np.concatenate([tc.real, -tc.imag], 2),
                        np.concatenate([tc.imag, tc.real], 2)], 1)
    ti = np.transpose(t, (0, 2, 1))
    g = np.conj(f1[:, :FFT_NZ1]).T / N_FFT
    fd = np.block([[g.real, -g.imag], [g.imag, g.real]])
    return tuple(jnp.asarray(a, dtype=BF16) for a in (fa, fak, t, ti, fd))


def _norm_matmul_kernel(x_ref, g_ref, w_ref, o_ref, h_ref):
    @pl.when(pl.program_id(1) == 0)
    def _():
        x = x_ref[...]
        ms = jnp.mean(x * x, axis=-1, keepdims=True)
        h_ref[...] = (x * lax.rsqrt(ms + EPS) * g_ref[...]).astype(BF16)

    o_ref[...] = jnp.dot(h_ref[...], w_ref[...], preferred_element_type=F32).astype(o_ref.dtype)


def _norm_matmul(x, g, w, tm, tn, out_dtype):
    m, d = x.shape
    n = w.shape[1]
    return pl.pallas_call(
        _norm_matmul_kernel,
        grid=(m // tm, n // tn),
        in_specs=[pl.BlockSpec((tm, d), lambda i, j: (i, 0)),
                  pl.BlockSpec((1, d), lambda i, j: (0, 0)),
                  pl.BlockSpec((d, tn), lambda i, j: (0, j))],
        out_specs=pl.BlockSpec((tm, tn), lambda i, j: (i, j)),
        out_shape=jax.ShapeDtypeStruct((m, n), out_dtype),
        scratch_shapes=[pltpu.VMEM((tm, d), BF16)],
        compiler_params=_params("parallel", "arbitrary"),
        name="norm_matmul",
    )(x, g.reshape(1, d), w)


CONF_T = 512
CONF_HALO = 16
CONF_RC = 64


def _conformer_kernel(a_ref, g_ref, ap_ref, gp_ref, an_ref, gn_ref, w_ref, b_ref, lg_ref, lb_ref,
                      o_ref, u_ref, y_ref):
    i = pl.program_id(1)
    nt = pl.num_programs(1)
    t = CONF_T
    h = CONF_HALO

    def glu(a, g):
        return a * jax.nn.sigmoid(g)

    prev_ok = (i > 0).astype(F32)
    next_ok = (i < nt - 1).astype(F32)
    nc = CONF_WIDTH // LANES
    for c in range(nc):
        cs = slice(c * LANES, (c + 1) * LANES)
        u_ref[c, 0:h, :] = glu(ap_ref[0, :, cs], gp_ref[0, :, cs]) * prev_ok
        u_ref[c, h:h + t, :] = glu(a_ref[0, :, cs], g_ref[0, :, cs])
        u_ref[c, h + t:h + t + h, :] = glu(an_ref[0, :, cs], gn_ref[0, :, cs]) * next_ok

    pad = CONF_KERNEL // 2

    def conv_chunk(c, carry):
        w = w_ref[c]
        bias = b_ref[c]
        for r in range(t // CONF_RC):
            r0 = r * CONF_RC
            acc = jnp.zeros((CONF_RC, LANES), F32) + bias
            for j in range(CONF_KERNEL):
                acc = acc + w[j:j + 1, :] * u_ref[c, r0 + h - pad + j:r0 + h - pad + j + CONF_RC, :]
            y_ref[c, r0:r0 + CONF_RC, :] = acc
        return carry

    lax.fori_loop(0, nc, conv_chunk, 0)

    def ln_chunk(r, carry):
        r0 = pl.multiple_of(r * CONF_RC, CONF_RC)
        ys = [y_ref[c, pl.ds(r0, CONF_RC), :] for c in range(nc)]
        tot = ys[0]
        for y in ys[1:]:
            tot = tot + y
        mu = jnp.sum(tot, axis=-1, keepdims=True) * (1.0 / CONF_WIDTH)
        yc = [y - mu for y in ys]
        sq = yc[0] * yc[0]
        for y in yc[1:]:
            sq = sq + y * y
        inv = lax.rsqrt(jnp.sum(sq, axis=-1, keepdims=True) * (1.0 / CONF_WIDTH) + EPS)
        for c in range(nc):
            cs = slice(c * LANES, (c + 1) * LANES)
            yn = yc[c] * inv * lg_ref[:, cs] + lb_ref[:, cs]
            o_ref[0, pl.ds(r0, CONF_RC), cs] = (yn * jax.nn.sigmoid(yn)).astype(o_ref.dtype)
        return carry

    lax.fori_loop(0, t // CONF_RC, ln_chunk, 0)


def _conformer(p, dw_w, dw_b, ln_g, ln_b):
    b, l, _ = p.shape
    t, h, c = CONF_T, CONF_HALO, CONF_WIDTH
    nt = l // t
    hb = t // h
    last_hb = l // h - 1
    cur = lambda col: pl.BlockSpec((1, t, c), lambda bi, i: (bi, i, col))
    prev = lambda col: pl.BlockSpec((1, h, c), lambda bi, i: (bi, jnp.maximum(i * hb - 1, 0), col))
    nxt = lambda col: pl.BlockSpec((1, h, c), lambda bi, i: (bi, jnp.minimum((i + 1) * hb, last_hb), col))
    vec = pl.BlockSpec((1, c), lambda bi, i: (0, 0))
    nc = c // LANES
    chunked = lambda rows: pl.BlockSpec((nc, rows, LANES), lambda bi, i: (0, 0, 0))
    w3 = dw_w.reshape(CONF_KERNEL, nc, LANES).transpose(1, 0, 2)
    return pl.pallas_call(
        _conformer_kernel,
        grid=(b, nt),
        in_specs=[cur(0), cur(1), prev(0), prev(1), nxt(0), nxt(1),
                  chunked(CONF_KERNEL), chunked(1), vec, vec],
        out_specs=pl.BlockSpec((1, t, c), lambda bi, i: (bi, i, 0)),
        out_shape=jax.ShapeDtypeStruct((b, l, c), BF16),
        scratch_shapes=[pltpu.VMEM((nc, t + 2 * h, LANES), F32), pltpu.VMEM((nc, t, LANES), F32)],
        compiler_params=_params("parallel", "parallel"),
        name="conformer",
    )(p, p, p, p, p, p, w3, dw_b.reshape(nc, 1, LANES), ln_g.reshape(1, c), ln_b.reshape(1, c))


SHORT_T = 512
SHORT_HALO = 8
SHORT_CW = 1024


def _short_conv_kernel(c_ref, p_ref, n_ref, w_ref, b_ref, o_ref, e_ref):
    i = pl.program_id(1)
    nt = pl.num_programs(1)
    t, h = SHORT_T, SHORT_HALO
    e_ref[0:h, :] = p_ref[0] * (i > 0).astype(F32)
    e_ref[h:h + t, :] = c_ref[0]
    e_ref[h + t:h + t + h, :] = n_ref[0] * (i < nt - 1).astype(F32)
    rc = 64
    for r0 in range(0, t, rc):
        for c0 in range(0, SHORT_CW, 4 * LANES):
            cs = slice(c0, c0 + 4 * LANES)
            acc = (w_ref[0:1, cs] * e_ref[r0 + h - 1:r0 + h - 1 + rc, cs]
                   + w_ref[1:2, cs] * e_ref[r0 + h:r0 + h + rc, cs]
                   + w_ref[2:3, cs] * e_ref[r0 + h + 1:r0 + h + 1 + rc, cs]
                   + b_ref[:, cs])
            o_ref[0, r0:r0 + rc, cs] = acc


def _short_conv(p, w, bias):
    b, l, _ = p.shape
    t, h, cw = SHORT_T, SHORT_HALO, SHORT_CW
    ncol = (HYENA_ORDER + 1) * HYENA_WIDTH // cw
    col0 = 2 * CONF_WIDTH // cw
    hb = t // h
    last_hb = l // h - 1
    return pl.pallas_call(
        _short_conv_kernel,
        grid=(b, l // t, ncol),
        in_specs=[pl.BlockSpec((1, t, cw), lambda bi, i, c: (bi, i, col0 + c)),
                  pl.BlockSpec((1, h, cw), lambda bi, i, c: (bi, jnp.maximum(i * hb - 1, 0), col0 + c)),
                  pl.BlockSpec((1, h, cw), lambda bi, i, c: (bi, jnp.minimum((i + 1) * hb, last_hb), col0 + c)),
                  pl.BlockSpec((SHORT_KERNEL, cw), lambda bi, i, c: (0, c)),
                  pl.BlockSpec((1, cw), lambda bi, i, c: (0, c))],
        out_specs=pl.BlockSpec((1, t, cw), lambda bi, i, c: (bi, i, c)),
        out_shape=jax.ShapeDtypeStruct((b, l, ncol * cw), F32),
        scratch_shapes=[pltpu.VMEM((t + 2 * h, cw), F32)],
        compiler_params=_params("parallel", "parallel", "parallel"),
        name="short_conv",
    )(p, p, p, w, bias.reshape(1, -1))


def _filter_mlp_kernel(z_ref, w1_ref, b1_ref, wi_ref, bi_ref, fr_ref, o_ref):
    hi = lax.Precision.HIGHEST
    fr = fr_ref[...]
    h = jnp.sin(fr * (jnp.dot(z_ref[...], w1_ref[...], precision=hi, preferred_element_type=F32) + b1_ref[...]))
    for i in range(HYENA_INNER_MLPS):
        h = jnp.sin(fr * (jnp.dot(h, wi_ref[i], precision=hi, preferred_element_type=F32) + bi_ref[i]))
    o_ref[...] = h


def _filter_mlp(z, w1, b1, w_inner, b_inner, freq):
    pad = lambda a, shape: jnp.zeros(shape, F32).at[tuple(slice(0, s) for s in a.shape)].set(a)
    zp = pad(z, (SEQ, LANES))
    w1p = pad(w1, (LANES, LANES))
    b1p = pad(b1.reshape(1, -1), (1, LANES))
    wip = pad(w_inner, (HYENA_INNER_MLPS, LANES, LANES))
    bip = pad(b_inner.reshape(HYENA_INNER_MLPS, 1, -1), (HYENA_INNER_MLPS, 1, LANES))
    frp = pad(freq.reshape(1, -1), (1, LANES))
    return pl.pallas_call(
        _filter_mlp_kernel,
        out_shape=jax.ShapeDtypeStruct((SEQ, LANES), F32),
        compiler_params=pltpu.CompilerParams(vmem_limit_bytes=VMEM_LIMIT),
        name="filter_mlp",
    )(zp, w1p, b1p, wip, bip, frp)


def _stage_b_rhs(a_ref, k1):
    ar = a_ref[pl.ds(k1, FFT_N2, stride=FFT_PITCH), :]
    ai = a_ref[pl.ds(FFT_N1 + k1, FFT_N2, stride=FFT_PITCH), :]
    return jnp.concatenate([ar, ai], axis=0).astype(BF16)


def _filter_spectrum_kernel(h_ref, wf_ref, wb_ref, dl_ref, fak_ref, t_ref, o_ref, kt_ref, a_ref):
    hi = lax.Precision.HIGHEST
    rows = lax.broadcasted_iota(jnp.int32, (SEQ, LANES), 0)
    tt = rows.astype(F32) * (1.0 / (SEQ - 1))
    decay = jnp.exp(-tt * dl_ref[...])
    kf = jnp.dot(h_ref[...], wf_ref[0, 0], precision=hi, preferred_element_type=F32) * decay
    kb = jnp.dot(h_ref[...], wb_ref[0, 0], precision=hi, preferred_element_type=F32) * decay
    kb = jnp.where(rows < SEQ - 1, kb, 0.0)
    ss = jnp.sum(kf * kf, axis=0, keepdims=True) + jnp.sum(kb * kb, axis=0, keepdims=True)
    scale = lax.rsqrt(ss + EPS)
    kt_ref[0] = kf * scale
    kt_ref[1] = kb * scale

    def stage_a(n2, carry):
        fwd = kt_ref[0, pl.ds(n2, FFT_NZ1, stride=FFT_N2), :]
        bwd = kt_ref[1, pl.ds(FFT_N2 - 1 - n2, FFT_NZ1, stride=FFT_N2), :]
        rhs = jnp.concatenate([fwd, bwd], axis=0).astype(BF16)
        a = jnp.dot(fak_ref[...], rhs, preferred_element_type=F32)
        a_ref[pl.ds(pl.multiple_of(n2 * FFT_PITCH, 8), 2 * FFT_N1), :] = a
        return carry

    lax.fori_loop(0, FFT_N2, stage_a, 0)

    def stage_b(k1, carry):
        b = jnp.dot(t_ref[k1], _stage_b_rhs(a_ref, k1), preferred_element_type=F32)
        o_ref[0, k1] = b.astype(o_ref.dtype)
        return carry

    lax.fori_loop(0, FFT_N1, stage_b, 0)


def _filter_spectrum(hmlp, w_filt, deltas, fak, t):
    wf = w_filt.reshape(HYENA_FILTER_ORDER, HYENA_ORDER, 2, HYENA_WIDTH).transpose(1, 2, 0, 3)
    wf = jnp.zeros((HYENA_ORDER, 2, LANES, HYENA_WIDTH), F32).at[:, :, :HYENA_FILTER_ORDER].set(wf)
    ncb = HYENA_WIDTH // LANES
    return pl.pallas_call(
        _filter_spectrum_kernel,
        grid=(HYENA_ORDER, ncb),
        in_specs=[_const_spec((SEQ, LANES)),
                  pl.BlockSpec((1, 1, LANES, LANES), lambda o, c: (o, 0, 0, c)),
                  pl.BlockSpec((1, 1, LANES, LANES), lambda o, c: (o, 1, 0, c)),
                  pl.BlockSpec((1, LANES), lambda o, c: (0, c)),
                  _const_spec((2 * FFT_N1, FFT_N1)),
                  _const_spec((FFT_N1, 2 * FFT_N2, 2 * FFT_N2))],
        out_specs=pl.BlockSpec((1, FFT_N1, 2 * FFT_N2, LANES), lambda o, c: (o, 0, 0, c)),
        out_shape=jax.ShapeDtypeStruct((HYENA_ORDER, FFT_N1, 2 * FFT_N2, HYENA_WIDTH), BF16),
        scratch_shapes=[pltpu.VMEM((2, SEQ, LANES), F32),
                        pltpu.VMEM((FFT_N2 * FFT_PITCH, LANES), F32)],
        compiler_params=_params("parallel", "parallel"),
        name="filter_spectrum",
    )(hmlp, wf, wf, deltas, fak, t)


def _hyena_conv_kernel(u_ref, xm_ref, kf_ref, sk_ref, fa_ref, t_ref, ti_ref, fd_ref, o_ref, a_ref):
    def slab(ref, bi, n2):
        return ref[bi, pl.ds(n2, FFT_NZ1, stride=FFT_N2), :]

    def stage_a(n2, carry):
        rhs = jnp.concatenate([slab(u_ref, 0, n2), slab(u_ref, 1, n2)], axis=0).astype(BF16)
        a = jnp.dot(fa_ref[...], rhs, preferred_element_type=F32)
        a_ref[pl.ds(pl.multiple_of(n2 * FFT_PITCH, 8), 2 * FFT_N1), :] = a
        return carry

    lax.fori_loop(0, FFT_N2, stage_a, 0)

    def stage_bc(k1, carry):
        b = jnp.dot(t_ref[k1], _stage_b_rhs(a_ref, k1), preferred_element_type=F32)
        kf = kf_ref[0, k1].astype(F32)
        br, bi = b[:FFT_N2], b[FFT_N2:]
        kr, ki = kf[:FFT_N2], kf[FFT_N2:]
        y = jnp.concatenate([br * kr - bi * ki, br * ki + bi * kr], axis=0).astype(BF16)
        c = jnp.dot(ti_ref[k1], y, preferred_element_type=F32)
        a_ref[pl.ds(k1, FFT_N2, stride=FFT_PITCH), :] = c[:FFT_N2]
        a_ref[pl.ds(FFT_N1 + k1, FFT_N2, stride=FFT_PITCH), :] = c[FFT_N2:]
        return carry

    lax.fori_loop(0, FFT_N1, stage_bc, 0)

    sk = sk_ref[0]

    def stage_d(n2, carry):
        cb = a_ref[pl.ds(pl.multiple_of(n2 * FFT_PITCH, 8), 2 * FFT_N1), :].astype(BF16)
        y = jnp.dot(fd_ref[...], cb, preferred_element_type=F32)
        for bi in range(2):
            yb = y[bi * FFT_NZ1:(bi + 1) * FFT_NZ1]
            z = slab(xm_ref, bi, n2) * (yb + sk * slab(u_ref, bi, n2))
            o_ref[bi, pl.ds(n2, FFT_NZ1, stride=FFT_N2), :] = z
        return carry

    lax.fori_loop(0, FFT_N2, stage_d, 0)


def _hyena_conv(u, u_col0, xm, xm_col0, kf, order, skip, consts):
    fa, _, t, ti, fd = consts
    ncb = HYENA_WIDTH // LANES
    return pl.pallas_call(
        _hyena_conv_kernel,
        grid=(ncb,),
        in_specs=[pl.BlockSpec((BATCH, SEQ, LANES), lambda c: (0, 0, u_col0 + c)),
                  pl.BlockSpec((BATCH, SEQ, LANES), lambda c: (0, 0, xm_col0 + c)),
                  pl.BlockSpec((1, FFT_N1, 2 * FFT_N2, LANES), lambda c: (order, 0, 0, c)),
                  pl.BlockSpec((1, 1, LANES), lambda c: (order, 0, c)),
                  _const_spec(fa.shape), _const_spec(t.shape), _const_spec(ti.shape), _const_spec(fd.shape)],
        out_specs=pl.BlockSpec((BATCH, SEQ, LANES), lambda c: (0, 0, c)),
        out_shape=jax.ShapeDtypeStruct((BATCH, SEQ, HYENA_WIDTH), F32),
        scratch_shapes=[pltpu.VMEM((FFT_N2 * FFT_PITCH, LANES), F32)],
        compiler_params=_params("parallel"),
        name="hyena_conv",
    )(u, xm, kf, skip.reshape(HYENA_ORDER, 1, HYENA_WIDTH), fa, t, ti, fd)


def _out_proj_kernel(x_ref, yc_ref, yh_ref, wa_ref, wb_ref, o_ref):
    acc = jnp.dot(yc_ref[...], wa_ref[...], preferred_element_type=F32)
    acc = acc + jnp.dot(yh_ref[...].astype(BF16), wb_ref[...], preferred_element_type=F32)
    o_ref[...] = x_ref[...] + acc


def _out_proj(x, yc, yh, w, tm=1024, tn=1024):
    m, d = x.shape
    kc = yc.shape[1]
    return pl.pallas_call(
        _out_proj_kernel,
        grid=(m // tm, d // tn),
        in_specs=[pl.BlockSpec((tm, tn), lambda i, j: (i, j)),
                  pl.BlockSpec((tm, kc), lambda i, j: (i, 0)),
                  pl.BlockSpec((tm, kc), lambda i, j: (i, 0)),
                  pl.BlockSpec((kc, tn), lambda i, j: (0, j)),
                  pl.BlockSpec((kc, tn), lambda i, j: (1, j))],
        out_specs=pl.BlockSpec((tm, tn), lambda i, j: (i, j)),
        out_shape=jax.ShapeDtypeStruct((m, d), F32),
        compiler_params=_params("parallel", "parallel"),
        name="out_proj",
    )(x, yc, yh, w, w)


ATTN_TM = 512


def _attn_kernel(q_ref, k_ref, v_ref, wo_ref, x_ref, o_ref, ctx_ref):
    hd = XATTN_HEAD_DIM
    for h in range(XATTN_HEADS):
        hs = slice(h * hd, (h + 1) * hd)
        s = lax.dot_general(q_ref[0, :, hs], k_ref[0, :, hs], (((1,), (1,)), ((), ())),
                            preferred_element_type=F32) * (hd ** -0.5)
        s = s - jnp.max(s, axis=-1, keepdims=True)
        e = jnp.exp(s)
        pr = e / jnp.sum(e, axis=-1, keepdims=True)
        ctx_ref[:, hs] = jnp.dot(pr.astype(BF16), v_ref[0, :, hs], preferred_element_type=F32).astype(BF16)
    o_ref[0] = x_ref[0] + jnp.dot(ctx_ref[...], wo_ref[...], preferred_element_type=F32)


def _attention(q, k, v, wo, x):
    b, l, d = x.shape
    tm = ATTN_TM
    return pl.pallas_call(
        _attn_kernel,
        grid=(b, l // tm),
        in_specs=[pl.BlockSpec((1, tm, d), lambda bi, i: (bi, i, 0)),
                  pl.BlockSpec((1, N_MEM, d), lambda bi, i: (bi, 0, 0)),
                  pl.BlockSpec((1, N_MEM, d), lambda bi, i: (bi, 0, 0)),
                  _const_spec((d, d)),
                  pl.BlockSpec((1, tm, d), lambda bi, i: (bi, i, 0))],
        out_specs=pl.BlockSpec((1, tm, d), lambda bi, i: (bi, i, 0)),
        out_shape=jax.ShapeDtypeStruct((b, l, d), F32),
        scratch_shapes=[pltpu.VMEM((tm, d), BF16)],
        compiler_params=_params("parallel", "parallel"),
        name="attention",
    )(q, k, v, wo, x)


FFN_TM = 512
FFN_FC = 512
FFN_HALO = 16


def _rms(x, g):
    ms = jnp.mean(x * x, axis=-1, keepdims=True)
    return x * lax.rsqrt(ms + EPS) * g


def _ffn_kernel(x_ref, xp_ref, xn_ref, g_ref, wg_ref, wv_ref, cw_ref, cb_ref, wd_ref, gf_ref,
                o_ref, h_ref, ge_ref, acc_ref):
    i = pl.program_id(1)
    j = pl.program_id(2)
    nt = pl.num_programs(1)
    nf = pl.num_programs(2)
    tm, hl = FFN_TM, FFN_HALO

    @pl.when(j == 0)
    def _():
        g = g_ref[...]
        h_ref[0:hl, :] = (_rms(xp_ref[0], g) * (i > 0).astype(F32)).astype(BF16)
        h_ref[hl:hl + tm, :] = _rms(x_ref[0], g).astype(BF16)
        h_ref[hl + tm:hl + tm + hl, :] = (_rms(xn_ref[0], g) * (i < nt - 1).astype(F32)).astype(BF16)
        acc_ref[...] = jnp.zeros_like(acc_ref)

    ge_ref[...] = jnp.dot(h_ref[...], wg_ref[...], preferred_element_type=F32)
    val = jnp.dot(h_ref[pl.ds(hl, tm), :], wv_ref[...], preferred_element_type=F32)
    gate = (cw_ref[0:1, :] * ge_ref[pl.ds(hl - 1, tm), :]
            + cw_ref[1:2, :] * ge_ref[pl.ds(hl, tm), :]
            + cw_ref[2:3, :] * ge_ref[pl.ds(hl + 1, tm), :]
            + cb_ref[...])
    act = (gate * jax.nn.sigmoid(gate) * val).astype(BF16)
    acc_ref[...] += jnp.dot(act, wd_ref[...], preferred_element_type=F32)

    @pl.when(j == nf - 1)
    def _():
        o_ref[0] = _rms(x_ref[0] + acc_ref[...], gf_ref[...])


def _ffn(x, g_ffn, w_up, dw_w, dw_b, w_down, g_final):
    b, l, d = x.shape
    tm, fc, hl = FFN_TM, FFN_FC, FFN_HALO
    nf = D_FF // fc
    hb = tm // hl
    last_hb = l // hl - 1
    return pl.pallas_call(
        _ffn_kernel,
        grid=(b, l // tm, nf),
        in_specs=[pl.BlockSpec((1, tm, d), lambda bi, i, j: (bi, i, 0)),
                  pl.BlockSpec((1, hl, d), lambda bi, i, j: (bi, jnp.maximum(i * hb - 1, 0), 0)),
                  pl.BlockSpec((1, hl, d), lambda bi, i, j: (bi, jnp.minimum((i + 1) * hb, last_hb), 0)),
                  pl.BlockSpec((1, d), lambda bi, i, j: (0, 0)),
                  pl.BlockSpec((d, fc), lambda bi, i, j: (0, j)),
                  pl.BlockSpec((d, fc), lambda bi, i, j: (0, nf + j)),
                  pl.BlockSpec((3, fc), lambda bi, i, j: (0, j)),
                  pl.BlockSpec((1, fc), lambda bi, i, j: (0, j)),
                  pl.BlockSpec((fc, d), lambda bi, i, j: (j, 0)),
                  pl.BlockSpec((1, d), lambda bi, i, j: (0, 0))],
        out_specs=pl.BlockSpec((1, tm, d), lambda bi, i, j: (bi, i, 0)),
        out_shape=jax.ShapeDtypeStruct((b, l, d), F32),
        scratch_shapes=[pltpu.VMEM((tm + 2 * hl, d), BF16),
                        pltpu.VMEM((tm + 2 * hl, fc), F32),
                        pltpu.VMEM((tm, d), F32)],
        compiler_params=_params("parallel", "parallel", "arbitrary"),
        name="ffn",
    )(x, x, x, g_ffn.reshape(1, d), w_up, w_up, dw_w, dw_b.reshape(1, -1), w_down, g_final.reshape(1, d))


def _position_features(length):
    t = jnp.linspace(0.0, 1.0, length, dtype=F32)[:, None]
    w = 2.0 * math.pi * jnp.arange(length, dtype=F32)[:, None] / length
    bands = (HYENA_EMB - 1) // 2
    f = jnp.linspace(1e-4, bands - 1, bands, dtype=F32)[None, :]
    return jnp.concatenate([t, jnp.cos(f * w), -jnp.sin(f * w)], axis=-1)


def _decay_rates():
    max_decay = math.log(HYENA_TARGET) / HYENA_FAST_DECAY
    min_decay = math.log(HYENA_TARGET) / HYENA_SLOW_DECAY
    return jnp.abs(jnp.linspace(min_decay, max_decay, HYENA_WIDTH, dtype=F32)).reshape(1, HYENA_WIDTH)


def kernel(x, mem, g_mix, w_in, conf_dw_w, conf_dw_b, conf_ln_g, conf_ln_b, hyena_short_w, hyena_short_b, hyena_w1, hyena_b1, hyena_w_inner, hyena_b_inner, hyena_w_filt, hyena_sin_freq, hyena_skip, w_out, g_xattn, g_mem, w_q, w_k, w_v, w_o, g_ffn, w_ffn_up, ffn_dw_w, ffn_dw_b, w_ffn_down, g_final):
    b, l, d = x.shape
    consts = _dft_constants()
    xf = x.reshape(b * l, d)

    p = _norm_matmul(xf, g_mix[0], w_in[0].astype(BF16), 1024, 1024, F32).reshape(b, l, IN_COLS)
    y_conf = _conformer(p, conf_dw_w[0], conf_dw_b[0], conf_ln_g[0], conf_ln_b[0])
    uc = _short_conv(p, hyena_short_w[0], hyena_short_b[0])

    hmlp = _filter_mlp(_position_features(l), hyena_w1[0], hyena_b1[0], hyena_w_inner[0],
                       hyena_b_inner[0], hyena_sin_freq[0])
    kf = _filter_spectrum(hmlp, hyena_w_filt[0], _decay_rates(), consts[1], consts[2])
    ncb = HYENA_WIDTH // LANES
    z1 = _hyena_conv(uc, 0, uc, ncb, kf, 0, hyena_skip[0], consts)
    z2 = _hyena_conv(z1, 0, uc, 2 * ncb, kf, 1, hyena_skip[0], consts)

    x1 = _out_proj(xf, y_conf.reshape(b * l, CONF_WIDTH), z2.reshape(b * l, HYENA_WIDTH),
                   w_out[0].astype(BF16))

    memf = mem.reshape(b * N_MEM, d)
    k = _norm_matmul(memf, g_mem[0], w_k[0].astype(BF16), b * N_MEM, 1024, BF16).reshape(b, N_MEM, d)
    v = _norm_matmul(memf, g_mem[0], w_v[0].astype(BF16), b * N_MEM, 1024, BF16).reshape(b, N_MEM, d)
    q = _norm_matmul(x1, g_xattn[0], w_q[0].astype(BF16), 1024, 1024, BF16).reshape(b, l, d)
    x2 = _attention(q, k, v, w_o[0].astype(BF16), x1.reshape(b, l, d))

    return _ffn(x2, g_ffn[0], w_ffn_up[0].astype(BF16), ffn_dw_w[0], ffn_dw_b[0],
                w_ffn_down[0].astype(BF16), g_final)
```

```python
import functools
import math

import numpy as np
import jax
import jax.numpy as jnp
from jax import lax
from jax.experimental import pallas as pl
from jax.experimental.pallas import tpu as pltpu

F32 = jnp.float32
BF16 = jnp.bfloat16

D_MODEL = 2048
BATCH = 2
SEQ = 4096
CONF_WIDTH = 1024
HYENA_WIDTH = 1024
CONF_KERNEL = 31
SHORT_KERNEL = 3
HYENA_ORDER = 2
HYENA_EMB = 33
HYENA_FILTER_ORDER = 64
HYENA_INNER_MLPS = 2
HYENA_FAST_DECAY = 0.3
HYENA_SLOW_DECAY = 1.5
HYENA_TARGET = 1e-2
N_MEM = 256
XATTN_HEADS = 4
XATTN_HEAD_DIM = D_MODEL // XATTN_HEADS
D_FF = 5632
EPS = 1e-6
IN_COLS = 2 * CONF_WIDTH + (HYENA_ORDER + 1) * HYENA_WIDTH

LANES = 128
VMEM_LIMIT = 56 * 1024 * 1024

N_FFT = 2 * SEQ
FFT_N1 = 128
FFT_N2 = 64
FFT_NZ1 = SEQ // FFT_N2
FFT_PITCH = 2 * FFT_N1 + 8
FFT_UNROLL = 8
FILTER_RC = 512
FILTER_PITCH = FFT_N2 + 8


def _params(*sem):
    return pltpu.CompilerParams(dimension_semantics=sem, vmem_limit_bytes=VMEM_LIMIT)


def _const_spec(shape):
    zeros = (0,) * len(shape)
    return pl.BlockSpec(shape, lambda *_: zeros, pipeline_mode=pl.Buffered(1))


@functools.lru_cache(maxsize=None)
def _dft_constants():
    k1 = np.arange(FFT_N1)
    f1 = np.exp(-2j * np.pi * np.outer(k1, k1) / FFT_N1)
    fh = f1[:, :FFT_NZ1]
    fa = np.block([[fh.real, -fh.imag], [fh.imag, fh.real]])
    perm = np.concatenate([np.arange(FFT_NZ1), FFT_N1 - 1 - np.arange(FFT_N1 - FFT_NZ1)])
    fp = f1[:, perm]
    fak = np.concatenate([fp.real, fp.imag], axis=0)
    n2 = np.arange(FFT_N2)
    kk = k1[:, None, None] + FFT_N1 * n2[None, :, None]
    ang = (n2[None, None, :] * kk) % N_FFT
    tc = np.exp(-2j * np.pi * ang / N_FFT)
    t = np.concatenate([np.concatenate([tc.real, -tc.imag], 2),
                        np.concatenate([tc.imag, tc.real], 2)], 1)
    ti = np.transpose(t, (0, 2, 1))
    g = np.conj(f1[:, :FFT_NZ1]).T / N_FFT
    fd = np.block([[g.real, -g.imag], [g.imag, g.real]])
    return tuple(jnp.asarray(a, dtype=BF16) for a in (fa, fak, t, ti, fd))


def _norm_matmul_kernel(x_ref, g_ref, w_ref, o_ref, h_ref):
    @pl.when(pl.program_id(1) == 0)
    def _():
        x = x_ref[...]
        ms = jnp.mean(x * x, axis=-1, keepdims=True)
        h_ref[...] = (x * lax.rsqrt(ms + EPS) * g_ref[...]).astype(BF16)

    o_ref[...] = jnp.dot(h_ref[...], w_ref[...], preferred_element_type=F32).astype(o_ref.dtype)


def _norm_matmul(x, g, w, tm, tn, out_dtype):
    m, d = x.shape
    n = w.shape[1]
    return pl.pallas_call(
        _norm_matmul_kernel,
        grid=(m // tm, n // tn),
        in_specs=[pl.BlockSpec((tm, d), lambda i, j: (i, 0)),
                  pl.BlockSpec((1, d), lambda i, j: (0, 0)),
                  pl.BlockSpec((d, tn), lambda i, j: (0, j))],
        out_specs=pl.BlockSpec((tm, tn), lambda i, j: (i, j)),
        out_shape=jax.ShapeDtypeStruct((m, n), out_dtype),
        scratch_shapes=[pltpu.VMEM((tm, d), BF16)],
        compiler_params=_params("parallel", "arbitrary"),
        name="norm_matmul",
    )(x, g.reshape(1, d), w)


CONF_T = 512
CONF_HALO = 16
CONF_RC = 64


def _conformer_kernel(a_ref, g_ref, ap_ref, gp_ref, an_ref, gn_ref, w_ref, b_ref, lg_ref, lb_ref,
                      o_ref, u_ref, y_ref):
    i = pl.program_id(1)
    nt = pl.num_programs(1)
    t = CONF_T
    h = CONF_HALO

    def glu(a, g):
        return a * jax.nn.sigmoid(g)

    prev_ok = (i > 0).astype(F32)
    next_ok = (i < nt - 1).astype(F32)
    nc = CONF_WIDTH // LANES
    for c in range(nc):
        cs = slice(c * LANES, (c + 1) * LANES)
        u_ref[c, 0:h, :] = glu(ap_ref[0, :, cs], gp_ref[0, :, cs]) * prev_ok
        u_ref[c, h:h + t, :] = glu(a_ref[0, :, cs], g_ref[0, :, cs])
        u_ref[c, h + t:h + t + h, :] = glu(an_ref[0, :, cs], gn_ref[0, :, cs]) * next_ok

    pad = CONF_KERNEL // 2

    def conv_chunk(c, carry):
        w = w_ref[c]
        bias = b_ref[c]
        for r in range(t // CONF_RC):
            r0 = r * CONF_RC
            acc = jnp.zeros((CONF_RC, LANES), F32) + bias
            for j in range(CONF_KERNEL):
                acc = acc + w[j:j + 1, :] * u_ref[c, r0 + h - pad + j:r0 + h - pad + j + CONF_RC, :]
            y_ref[c, r0:r0 + CONF_RC, :] = acc
        return carry

    lax.fori_loop(0, nc, conv_chunk, 0)

    def ln_chunk(r, carry):
        r0 = pl.multiple_of(r * CONF_RC, CONF_RC)
        ys = [y_ref[c, pl.ds(r0, CONF_RC), :] for c in range(nc)]
        tot = ys[0]
        for y in ys[1:]:
            tot = tot + y
        mu = jnp.sum(tot, axis=-1, keepdims=True) * (1.0 / CONF_WIDTH)
        yc = [y - mu for y in ys]
        sq = yc[0] * yc[0]
        for y in yc[1:]:
            sq = sq + y * y
        inv = lax.rsqrt(jnp.sum(sq, axis=-1, keepdims=True) * (1.0 / CONF_WIDTH) + EPS)
        for c in range(nc):
            cs = slice(c * LANES, (c + 1) * LANES)
            yn = yc[c] * inv * lg_ref[:, cs] + lb_ref[:, cs]
            o_ref[0, pl.ds(r0, CONF_RC), cs] = (yn * jax.nn.sigmoid(yn)).astype(o_ref.dtype)
        return carry

    lax.fori_loop(0, t // CONF_RC, ln_chunk, 0)


def _conformer(p, dw_w, dw_b, ln_g, ln_b):
    b, l, _ = p.shape
    t, h, c = CONF_T, CONF_HALO, CONF_WIDTH
    nt = l // t
    hb = t // h
    last_hb = l // h - 1
    cur = lambda col: pl.BlockSpec((1, t, c), lambda bi, i: (bi, i, col))
    prev = lambda col: pl.BlockSpec((1, h, c), lambda bi, i: (bi, jnp.maximum(i * hb - 1, 0), col))
    nxt = lambda col: pl.BlockSpec((1, h, c), lambda bi, i: (bi, jnp.minimum((i + 1) * hb, last_hb), col))
    vec = pl.BlockSpec((1, c), lambda bi, i: (0, 0))
    nc = c // LANES
    chunked = lambda rows: pl.BlockSpec((nc, rows, LANES), lambda bi, i: (0, 0, 0))
    w3 = dw_w.reshape(CONF_KERNEL, nc, LANES).transpose(1, 0, 2)
    return pl.pallas_call(
        _conformer_kernel,
        grid=(b, nt),
        in_specs=[cur(0), cur(1), prev(0), prev(1), nxt(0), nxt(1),
                  chunked(CONF_KERNEL), chunked(1), vec, vec],
        out_specs=pl.BlockSpec((1, t, c), lambda bi, i: (bi, i, 0)),
        out_shape=jax.ShapeDtypeStruct((b, l, c), BF16),
        scratch_shapes=[pltpu.VMEM((nc, t + 2 * h, LANES), F32), pltpu.VMEM((nc, t, LANES), F32)],
        compiler_params=_params("parallel", "parallel"),
        name="conformer",
    )(p, p, p, p, p, p, w3, dw_b.reshape(nc, 1, LANES), ln_g.reshape(1, c), ln_b.reshape(1, c))


SHORT_T = 512
SHORT_HALO = 8
SHORT_CW = 1024
SHORT_N1 = SHORT_T // FFT_N2
SHORT_PITCH = FFT_N2 + 8


def _short_conv_kernel(c_ref, p_ref, n_ref, w_ref, b_ref, o_ref, e_ref, s_ref):
    i = pl.program_id(1)
    nt = pl.num_programs(1)
    t, h = SHORT_T, SHORT_HALO
    e_ref[0:h, :] = p_ref[0] * (i > 0).astype(F32)
    e_ref[h:h + t, :] = c_ref[0]
    e_ref[h + t:h + t + h, :] = n_ref[0] * (i < nt - 1).astype(F32)
    nlc = SHORT_CW // LANES
    for r in range(SHORT_N1):
        r0 = r * FFT_N2
        for lc in range(nlc):
            cs = slice(lc * LANES, (lc + 1) * LANES)
            acc = (w_ref[0:1, cs] * e_ref[r0 + h - 1:r0 + h - 1 + FFT_N2, cs]
                   + w_ref[1:2, cs] * e_ref[r0 + h:r0 + h + FFT_N2, cs]
                   + w_ref[2:3, cs] * e_ref[r0 + h + 1:r0 + h + 1 + FFT_N2, cs]
                   + b_ref[:, cs])
            s_ref[lc, r * SHORT_PITCH:r * SHORT_PITCH + FFT_N2, :] = acc

    def gather(n2, carry):
        for lc in range(nlc):
            o_ref[0, n2, :, lc * LANES:(lc + 1) * LANES] = s_ref[lc, pl.ds(n2, SHORT_N1, stride=SHORT_PITCH), :]
        return carry

    lax.fori_loop(0, FFT_N2, gather, 0, unroll=8)


def _short_conv(p, w, bias):
    b, l, _ = p.shape
    t, h, cw = SHORT_T, SHORT_HALO, SHORT_CW
    ncol = (HYENA_ORDER + 1) * HYENA_WIDTH // cw
    col0 = 2 * CONF_WIDTH // cw
    hb = t // h
    last_hb = l // h - 1
    return pl.pallas_call(
        _short_conv_kernel,
        grid=(b, l // t, ncol),
        in_specs=[pl.BlockSpec((1, t, cw), lambda bi, i, c: (bi, i, col0 + c)),
                  pl.BlockSpec((1, h, cw), lambda bi, i, c: (bi, jnp.maximum(i * hb - 1, 0), col0 + c)),
                  pl.BlockSpec((1, h, cw), lambda bi, i, c: (bi, jnp.minimum((i + 1) * hb, last_hb), col0 + c)),
                  pl.BlockSpec((SHORT_KERNEL, cw), lambda bi, i, c: (0, c)),
                  pl.BlockSpec((1, cw), lambda bi, i, c: (0, c))],
        out_specs=pl.BlockSpec((1, FFT_N2, SHORT_N1, cw), lambda bi, i, c: (bi, 0, i, c)),
        out_shape=jax.ShapeDtypeStruct((b, FFT_N2, FFT_NZ1, ncol * cw), F32),
        scratch_shapes=[pltpu.VMEM((t + 2 * h, cw), F32),
                        pltpu.VMEM((cw // LANES, SHORT_N1 * SHORT_PITCH, LANES), F32)],
        compiler_params=_params("parallel", "parallel", "parallel"),
        name="short_conv",
    )(p, p, p, w, bias.reshape(1, -1))


def _filter_mlp_kernel(z_ref, w1_ref, b1_ref, wi_ref, bi_ref, fr_ref, o_ref):
    hi = lax.Precision.HIGHEST
    fr = fr_ref[...]
    h = jnp.sin(fr * (jnp.dot(z_ref[...], w1_ref[...], precision=hi, preferred_element_type=F32) + b1_ref[...]))
    for i in range(HYENA_INNER_MLPS):
        h = jnp.sin(fr * (jnp.dot(h, wi_ref[i], precision=hi, preferred_element_type=F32) + bi_ref[i]))
    o_ref[...] = h


def _filter_mlp(z, w1, b1, w_inner, b_inner, freq):
    pad = lambda a, shape: jnp.zeros(shape, F32).at[tuple(slice(0, s) for s in a.shape)].set(a)
    zp = pad(z, (SEQ, LANES))
    w1p = pad(w1, (LANES, LANES))
    b1p = pad(b1.reshape(1, -1), (1, LANES))
    wip = pad(w_inner, (HYENA_INNER_MLPS, LANES, LANES))
    bip = pad(b_inner.reshape(HYENA_INNER_MLPS, 1, -1), (HYENA_INNER_MLPS, 1, LANES))
    frp = pad(freq.reshape(1, -1), (1, LANES))
    return pl.pallas_call(
        _filter_mlp_kernel,
        out_shape=jax.ShapeDtypeStruct((SEQ, LANES), F32),
        compiler_params=pltpu.CompilerParams(vmem_limit_bytes=VMEM_LIMIT),
        name="filter_mlp",
    )(zp, w1p, b1p, wip, bip, frp)


def _stage_b_rhs(a_ref, k1):
    ar = a_ref[pl.ds(k1, FFT_N2, stride=FFT_PITCH), :]
    ai = a_ref[pl.ds(FFT_N1 + k1, FFT_N2, stride=FFT_PITCH), :]
    return jnp.concatenate([ar, ai], axis=0).astype(BF16)


def _split_bf16(a):
    hi = a.astype(BF16)
    return hi, (a - hi.astype(F32)).astype(BF16)


def _filter_spectrum_kernel(h_ref, w_ref, dl_ref, fak_ref, t_ref, o_ref, kt_ref, a_ref):
    rc = FILTER_RC
    w_hi, w_lo = _split_bf16(jnp.concatenate([w_ref[0, 0], w_ref[0, 1]], axis=1))
    dl = jnp.concatenate([dl_ref[...], dl_ref[...]], axis=1)

    def gen(r, ss):
        r0 = pl.multiple_of(r * rc, rc)
        rows = lax.broadcasted_iota(jnp.int32, (rc, 2 * LANES), 0) + r0
        lanes = lax.broadcasted_iota(jnp.int32, (rc, 2 * LANES), 1)
        decay = jnp.exp(-(rows.astype(F32) * (1.0 / (SEQ - 1))) * dl)
        h_hi, h_lo = _split_bf16(h_ref[pl.ds(r0, rc), :])
        k = jnp.dot(h_hi, w_hi, preferred_element_type=F32)
        k = k + (jnp.dot(h_hi, w_lo, preferred_element_type=F32) + jnp.dot(h_lo, w_hi, preferred_element_type=F32))
        k = k * decay
        k = jnp.where((rows >= SEQ - 1) & (lanes >= LANES), 0.0, k)
        for n in range(rc // FFT_N2):
            dst = pl.ds(pl.multiple_of((r * (rc // FFT_N2) + n) * FILTER_PITCH, 8), FFT_N2)
            kt_ref[0, dst, :] = k[n * FFT_N2:(n + 1) * FFT_N2, :LANES]
            kt_ref[1, dst, :] = k[n * FFT_N2:(n + 1) * FFT_N2, LANES:]
        return ss + jnp.sum(k * k, axis=0, keepdims=True)

    ss = lax.fori_loop(0, SEQ // rc, gen, jnp.zeros((1, 2 * LANES), F32))
    scale = lax.rsqrt(ss[:, :LANES] + ss[:, LANES:] + EPS)

    def stage_a(n2, carry):
        fwd = kt_ref[0, pl.ds(n2, FFT_NZ1, stride=FILTER_PITCH), :]
        bwd = kt_ref[1, pl.ds(FFT_N2 - 1 - n2, FFT_NZ1, stride=FILTER_PITCH), :]
        rhs = (jnp.concatenate([fwd, bwd], axis=0) * scale).astype(BF16)
        a = jnp.dot(fak_ref[...], rhs, preferred_element_type=F32)
        a_ref[pl.ds(pl.multiple_of(n2 * FFT_PITCH, 8), 2 * FFT_N1), :] = a
        return carry

    lax.fori_loop(0, FFT_N2, stage_a, 0, unroll=FFT_UNROLL)

    def stage_b(k1, carry):
        b = jnp.dot(t_ref[k1], _stage_b_rhs(a_ref, k1), preferred_element_type=F32)
        o_ref[0, k1] = b.astype(o_ref.dtype)
        return carry

    lax.fori_loop(0, FFT_N1, stage_b, 0, unroll=FFT_UNROLL)


def _filter_spectrum(hmlp, w_filt, deltas, fak, t):
    wf = w_filt.reshape(HYENA_FILTER_ORDER, HYENA_ORDER, 2, HYENA_WIDTH).transpose(1, 2, 0, 3)
    wf = jnp.zeros((HYENA_ORDER, 2, LANES, HYENA_WIDTH), F32).at[:, :, :HYENA_FILTER_ORDER].set(wf)
    ncb = HYENA_WIDTH // LANES
    return pl.pallas_call(
        _filter_spectrum_kernel,
        grid=(HYENA_ORDER, ncb),
        in_specs=[_const_spec((SEQ, LANES)),
                  pl.BlockSpec((1, 2, LANES, LANES), lambda o, c: (o, 0, 0, c)),
                  pl.BlockSpec((1, LANES), lambda o, c: (0, c)),
                  _const_spec((2 * FFT_N1, FFT_N1)),
                  _const_spec((FFT_N1, 2 * FFT_N2, 2 * FFT_N2))],
        out_specs=pl.BlockSpec((1, FFT_N1, 2 * FFT_N2, LANES), lambda o, c: (o, 0, 0, c)),
        out_shape=jax.ShapeDtypeStruct((HYENA_ORDER, FFT_N1, 2 * FFT_N2, HYENA_WIDTH), BF16),
        scratch_shapes=[pltpu.VMEM((2, FFT_NZ1 * FILTER_PITCH, LANES), F32),
                        pltpu.VMEM((FFT_N2 * FFT_PITCH, LANES), F32)],
        compiler_params=_params("parallel", "parallel"),
        name="filter_spectrum",
    )(hmlp, wf, deltas, fak, t)


def _hyena_conv_kernel(u_ref, xm_ref, kf_ref, sk_ref, fa_ref, t_ref, ti_ref, fd_ref, o_ref, a_ref, y_ref, *,
                       slab_order_out):
    def slab(ref, bi, n2):
        return ref[bi, n2]

    def stage_a(n2, carry):
        rhs = jnp.concatenate([slab(u_ref, 0, n2), slab(u_ref, 1, n2)], axis=0).astype(BF16)
        a = jnp.dot(fa_ref[...], rhs, preferred_element_type=F32)
        a_ref[pl.ds(pl.multiple_of(n2 * FFT_PITCH, 8), 2 * FFT_N1), :] = a
        return carry

    lax.fori_loop(0, FFT_N2, stage_a, 0, unroll=FFT_UNROLL)

    def stage_b(k1, carry):
        b = jnp.dot(t_ref[k1], _stage_b_rhs(a_ref, k1), preferred_element_type=F32)
        kf = kf_ref[0, k1].astype(F32)
        br, bi = b[:FFT_N2], b[FFT_N2:]
        kr, ki = kf[:FFT_N2], kf[FFT_N2:]
        y_ref[k1] = jnp.concatenate([br * kr - bi * ki, br * ki + bi * kr], axis=0).astype(BF16)
        return carry

    lax.fori_loop(0, FFT_N1, stage_b, 0, unroll=FFT_UNROLL)

    def stage_c(k1, carry):
        c = jnp.dot(ti_ref[k1], y_ref[k1], preferred_element_type=F32)
        a_ref[pl.ds(k1, FFT_N2, stride=FFT_PITCH), :] = c[:FFT_N2]
        a_ref[pl.ds(FFT_N1 + k1, FFT_N2, stride=FFT_PITCH), :] = c[FFT_N2:]
        return carry

    lax.fori_loop(0, FFT_N1, stage_c, 0, unroll=FFT_UNROLL)

    sk = sk_ref[0]

    def stage_d(n2, carry):
        cb = a_ref[pl.ds(pl.multiple_of(n2 * FFT_PITCH, 8), 2 * FFT_N1), :].astype(BF16)
        y = jnp.dot(fd_ref[...], cb, preferred_element_type=F32)
        for bi in range(2):
            yb = y[bi * FFT_NZ1:(bi + 1) * FFT_NZ1]
            z = slab(xm_ref, bi, n2) * (yb + sk * slab(u_ref, bi, n2))
            if slab_order_out:
                o_ref[bi, n2] = z
            else:
                o_ref[bi, pl.ds(n2, FFT_NZ1, stride=FFT_N2), :] = z
        return carry

    lax.fori_loop(0, FFT_N2, stage_d, 0, unroll=FFT_UNROLL)


def _hyena_conv(u, u_col0, xm, xm_col0, kf, order, skip, consts, slab_order_out):
    fa, _, t, ti, fd = consts
    ncb = HYENA_WIDTH // LANES
    slab_block = (BATCH, FFT_N2, FFT_NZ1, LANES)
    if slab_order_out:
        out_spec = pl.BlockSpec(slab_block, lambda c: (0, 0, 0, c))
        out_shape = jax.ShapeDtypeStruct((BATCH, FFT_N2, FFT_NZ1, HYENA_WIDTH), F32)
    else:
        out_spec = pl.BlockSpec((BATCH, SEQ, LANES), lambda c: (0, 0, c))
        out_shape = jax.ShapeDtypeStruct((BATCH, SEQ, HYENA_WIDTH), F32)
    return pl.pallas_call(
        functools.partial(_hyena_conv_kernel, slab_order_out=slab_order_out),
        grid=(ncb,),
        in_specs=[pl.BlockSpec(slab_block, lambda c: (0, 0, 0, u_col0 + c)),
                  pl.BlockSpec(slab_block, lambda c: (0, 0, 0, xm_col0 + c)),
                  pl.BlockSpec((1, FFT_N1, 2 * FFT_N2, LANES), lambda c: (order, 0, 0, c)),
                  pl.BlockSpec((1, 1, LANES), lambda c: (order, 0, c)),
                  _const_spec(fa.shape), _const_spec(t.shape), _const_spec(ti.shape), _const_spec(fd.shape)],
        out_specs=out_spec,
        out_shape=out_shape,
        scratch_shapes=[pltpu.VMEM((FFT_N2 * FFT_PITCH, LANES), F32),
                        pltpu.VMEM((FFT_N1, 2 * FFT_N2, LANES), BF16)],
        compiler_params=_params("parallel"),
        name="hyena_conv",
    )(u, xm, kf, skip.reshape(HYENA_ORDER, 1, HYENA_WIDTH), fa, t, ti, fd)


def _out_proj_kernel(x_ref, yc_ref, yh_ref, wa_ref, wb_ref, o_ref):
    acc = jnp.dot(yc_ref[...], wa_ref[...], preferred_element_type=F32)
    acc = acc + jnp.dot(yh_ref[...].astype(BF16), wb_ref[...], preferred_element_type=F32)
    o_ref[...] = x_ref[...] + acc


def _out_proj(x, yc, yh, w, tm=1024, tn=1024):
    m, d = x.shape
    kc = yc.shape[1]
    return pl.pallas_call(
        _out_proj_kernel,
        grid=(m // tm, d // tn),
        in_specs=[pl.BlockSpec((tm, tn), lambda i, j: (i, j)),
                  pl.BlockSpec((tm, kc), lambda i, j: (i, 0)),
                  pl.BlockSpec((tm, kc), lambda i, j: (i, 0)),
                  pl.BlockSpec((kc, tn), lambda i, j: (0, j)),
                  pl.BlockSpec((kc, tn), lambda i, j: (1, j))],
        out_specs=pl.BlockSpec((tm, tn), lambda i, j: (i, j)),
        out_shape=jax.ShapeDtypeStruct((m, d), F32),
        compiler_params=_params("parallel", "parallel"),
        name="out_proj",
    )(x, yc, yh, w, w)


ATTN_TM = 512


def _attn_kernel(q_ref, k_ref, v_ref, wo_ref, x_ref, o_ref, ctx_ref):
    hd = XATTN_HEAD_DIM
    for h in range(XATTN_HEADS):
        hs = slice(h * hd, (h + 1) * hd)
        s = lax.dot_general(q_ref[0, :, hs], k_ref[0, :, hs], (((1,), (1,)), ((), ())),
                            preferred_element_type=F32) * (hd ** -0.5)
        s = s - jnp.max(s, axis=-1, keepdims=True)
        e = jnp.exp(s)
        pr = e / jnp.sum(e, axis=-1, keepdims=True)
        ctx_ref[:, hs] = jnp.dot(pr.astype(BF16), v_ref[0, :, hs], preferred_element_type=F32).astype(BF16)
    o_ref[0] = x_ref[0] + jnp.dot(ctx_ref[...], wo_ref[...], preferred_element_type=F32)


def _attention(q, k, v, wo, x):
    b, l, d = x.shape
    tm = ATTN_TM
    return pl.pallas_call(
        _attn_kernel,
        grid=(b, l // tm),
        in_specs=[pl.BlockSpec((1, tm, d), lambda bi, i: (bi, i, 0)),
                  pl.BlockSpec((1, N_MEM, d), lambda bi, i: (bi, 0, 0)),
                  pl.BlockSpec((1, N_MEM, d), lambda bi, i: (bi, 0, 0)),
                  _const_spec((d, d)),
                  pl.BlockSpec((1, tm, d), lambda bi, i: (bi, i, 0))],
        out_specs=pl.BlockSpec((1, tm, d), lambda bi, i: (bi, i, 0)),
        out_shape=jax.ShapeDtypeStruct((b, l, d), F32),
        scratch_shapes=[pltpu.VMEM((tm, d), BF16)],
        compiler_params=_params("parallel", "parallel"),
        name="attention",
    )(q, k, v, wo, x)


FFN_TM = 512
FFN_FC = 512
FFN_HALO = 16


def _rms(x, g):
    ms = jnp.mean(x * x, axis=-1, keepdims=True)
    return x * lax.rsqrt(ms + EPS) * g


def _ffn_kernel(x_ref, xp_ref, xn_ref, g_ref, wg_ref, wv_ref, cw_ref, cb_ref, wd_ref, gf_ref,
                o_ref, h_ref, ge_ref, acc_ref):
    i = pl.program_id(1)
    j = pl.program_id(2)
    nt = pl.num_programs(1)
    nf = pl.num_programs(2)
    tm, hl = FFN_TM, FFN_HALO

    @pl.when(j == 0)
    def _():
        g = g_ref[...]
        h_ref[0:hl, :] = (_rms(xp_ref[0], g) * (i > 0).astype(F32)).astype(BF16)
        h_ref[hl:hl + tm, :] = _rms(x_ref[0], g).astype(BF16)
        h_ref[hl + tm:hl + tm + hl, :] = (_rms(xn_ref[0], g) * (i < nt - 1).astype(F32)).astype(BF16)
        acc_ref[...] = jnp.zeros_like(acc_ref)

    ge_ref[...] = jnp.dot(h_ref[...], wg_ref[...], preferred_element_type=F32)
    val = jnp.dot(h_ref[pl.ds(hl, tm), :], wv_ref[...], preferred_element_type=F32)
    gate = (cw_ref[0:1, :] * ge_ref[pl.ds(hl - 1, tm), :]
            + cw_ref[1:2, :] * ge_ref[pl.ds(hl, tm), :]
            + cw_ref[2:3, :] * ge_ref[pl.ds(hl + 1, tm), :]
            + cb_ref[...])
    act = (gate * jax.nn.sigmoid(gate) * val).astype(BF16)
    acc_ref[...] += jnp.dot(act, wd_ref[...], preferred_element_type=F32)

    @pl.when(j == nf - 1)
    def _():
        o_ref[0] = _rms(x_ref[0] + acc_ref[...], gf_ref[...])


def _ffn(x, g_ffn, w_up, dw_w, dw_b, w_down, g_final):
    b, l, d = x.shape
    tm, fc, hl = FFN_TM, FFN_FC, FFN_HALO
    nf = D_FF // fc
    hb = tm // hl
    last_hb = l // hl - 1
    return pl.pallas_call(
        _ffn_kernel,
        grid=(b, l // tm, nf),
        in_specs=[pl.BlockSpec((1, tm, d), lambda bi, i, j: (bi, i, 0)),
                  pl.BlockSpec((1, hl, d), lambda bi, i, j: (bi, jnp.maximum(i * hb - 1, 0), 0)),
                  pl.BlockSpec((1, hl, d), lambda bi, i, j: (bi, jnp.minimum((i + 1) * hb, last_hb), 0)),
                  pl.BlockSpec((1, d), lambda bi, i, j: (0, 0)),
                  pl.BlockSpec((d, fc), lambda bi, i, j: (0, j)),
                  pl.BlockSpec((d, fc), lambda bi, i, j: (0, nf + j)),
                  pl.BlockSpec((3, fc), lambda bi, i, j: (0, j)),
                  pl.BlockSpec((1, fc), lambda bi, i, j: (0, j)),
                  pl.BlockSpec((fc, d), lambda bi, i, j: (j, 0)),
                  pl.BlockSpec((1, d), lambda bi, i, j: (0, 0))],
        out_specs=pl.BlockSpec((1, tm, d), lambda bi, i, j: (bi, i, 0)),
        out_shape=jax.ShapeDtypeStruct((b, l, d), F32),
        scratch_shapes=[pltpu.VMEM((tm + 2 * hl, d), BF16),
                        pltpu.VMEM((tm + 2 * hl, fc), F32),
                        pltpu.VMEM((tm, d), F32)],
        compiler_params=_params("parallel", "parallel", "arbitrary"),
        name="ffn",
    )(x, x, x, g_ffn.reshape(1, d), w_up, w_up, dw_w, dw_b.reshape(1, -1), w_down, g_final.reshape(1, d))


def _position_features(length):
    t = jnp.linspace(0.0, 1.0, length, dtype=F32)[:, None]
    w = 2.0 * math.pi * jnp.arange(length, dtype=F32)[:, None] / length
    bands = (HYENA_EMB - 1) // 2
    f = jnp.linspace(1e-4, bands - 1, bands, dtype=F32)[None, :]
    return jnp.concatenate([t, jnp.cos(f * w), -jnp.sin(f * w)], axis=-1)


def _decay_rates():
    max_decay = math.log(HYENA_TARGET) / HYENA_FAST_DECAY
    min_decay = math.log(HYENA_TARGET) / HYENA_SLOW_DECAY
    return jnp.abs(jnp.linspace(min_decay, max_decay, HYENA_WIDTH, dtype=F32)).reshape(1, HYENA_WIDTH)


def kernel(x, mem, g_mix, w_in, conf_dw_w, conf_dw_b, conf_ln_g, conf_ln_b, hyena_short_w, hyena_short_b, hyena_w1, hyena_b1, hyena_w_inner, hyena_b_inner, hyena_w_filt, hyena_sin_freq, hyena_skip, w_out, g_xattn, g_mem, w_q, w_k, w_v, w_o, g_ffn, w_ffn_up, ffn_dw_w, ffn_dw_b, w_ffn_down, g_final):
    b, l, d = x.shape
    consts = _dft_constants()
    xf = x.reshape(b * l, d)

    p = _norm_matmul(xf, g_mix[0], w_in[0].astype(BF16), 1024, 1024, F32).reshape(b, l, IN_COLS)
    y_conf = _conformer(p, conf_dw_w[0], conf_dw_b[0], conf_ln_g[0], conf_ln_b[0])
    uc = _short_conv(p, hyena_short_w[0], hyena_short_b[0])

    hmlp = _filter_mlp(_position_features(l), hyena_w1[0], hyena_b1[0], hyena_w_inner[0],
                       hyena_b_inner[0], hyena_sin_freq[0])
    kf = _filter_spectrum(hmlp, hyena_w_filt[0], _decay_rates(), consts[1], consts[2])
    ncb = HYENA_WIDTH // LANES
    z1 = _hyena_conv(uc, 0, uc, ncb, kf, 0, hyena_skip[0], consts, slab_order_out=True)
    z2 = _hyena_conv(z1, 0, uc, 2 * ncb, kf, 1, hyena_skip[0], consts, slab_order_out=False)

    x1 = _out_proj(xf, y_conf.reshape(b * l, CONF_WIDTH), z2.reshape(b * l, HYENA_WIDTH),
                   w_out[0].astype(BF16))

    memf = mem.reshape(b * N_MEM, d)
    k = _norm_matmul(memf, g_mem[0], w_k[0].astype(BF16), b * N_MEM, 1024, BF16).reshape(b, N_MEM, d)
    v = _norm_matmul(memf, g_mem[0], w_v[0].astype(BF16), b * N_MEM, 1024, BF16).reshape(b, N_MEM, d)
    q = _norm_matmul(x1, g_xattn[0], w_q[0].astype(BF16), 1024, 1024, BF16).reshape(b, l, d)
    x2 = _attention(q, k, v, w_o[0].astype(BF16), x1.reshape(b, l, d))

    return _ffn(x2, g_ffn[0], w_ffn_up[0].astype(BF16), ffn_dw_w[0], ffn_dw_b[0],
                w_ffn_down[0].astype(BF16), g_final)
```

```python
import functools
import math

import numpy as np
import jax
import jax.numpy as jnp
from jax import lax
from jax.experimental import pallas as pl
from jax.experimental.pallas import tpu as pltpu

F32 = jnp.float32
BF16 = jnp.bfloat16

D_MODEL = 2048
BATCH = 2
SEQ = 4096
CONF_WIDTH = 1024
HYENA_WIDTH = 1024
CONF_KERNEL = 31
SHORT_KERNEL = 3
HYENA_ORDER = 2
HYENA_EMB = 33
HYENA_FILTER_ORDER = 64
HYENA_INNER_MLPS = 2
HYENA_FAST_DECAY = 0.3
HYENA_SLOW_DECAY = 1.5
HYENA_TARGET = 1e-2
N_MEM = 256
XATTN_HEADS = 4
XATTN_HEAD_DIM = D_MODEL // XATTN_HEADS
D_FF = 5632
EPS = 1e-6
IN_COLS = 2 * CONF_WIDTH + (HYENA_ORDER + 1) * HYENA_WIDTH

LANES = 128
VMEM_LIMIT = 56 * 1024 * 1024

N_FFT = 2 * SEQ
FFT_N1 = 128
FFT_N2 = 64
FFT_NZ1 = SEQ // FFT_N2
FFT_PITCH = 2 * FFT_N1 + 8
FFT_UNROLL = 8
FILTER_RC = 512
FILTER_PITCH = FFT_N2 + 8


def _params(*sem, vmem_limit=VMEM_LIMIT):
    return pltpu.CompilerParams(dimension_semantics=sem, vmem_limit_bytes=vmem_limit)


def _const_spec(shape, index=None):
    index = (0,) * len(shape) if index is None else index
    return pl.BlockSpec(shape, lambda *_: index, pipeline_mode=pl.Buffered(1))


def _sigmoid(x):
    return 0.5 * (jnp.tanh(0.5 * x) + 1.0)


def _rms(x, g):
    ms = jnp.mean(x * x, axis=-1, keepdims=True)
    return x * lax.rsqrt(ms + EPS) * g


@functools.lru_cache(maxsize=None)
def _dft_constants():
    k1 = np.arange(FFT_N1)
    f1 = np.exp(-2j * np.pi * np.outer(k1, k1) / FFT_N1)
    fh = f1[:, :FFT_NZ1]
    fa = np.block([[fh.real, -fh.imag], [fh.imag, fh.real]])
    perm = np.concatenate([np.arange(FFT_NZ1), FFT_N1 - 1 - np.arange(FFT_N1 - FFT_NZ1)])
    fp = f1[:, perm]
    fak = np.concatenate([fp.real, fp.imag], axis=0)
    n2 = np.arange(FFT_N2)
    kk = k1[:, None, None] + FFT_N1 * n2[None, :, None]
    ang = (n2[None, None, :] * kk) % N_FFT
    tc = np.exp(-2j * np.pi * ang / N_FFT)
    t = np.concatenate([np.concatenate([tc.real, -tc.imag], 2),
                        np.concatenate([tc.imag, tc.real], 2)], 1)
    ti = np.transpose(t, (0, 2, 1))
    g = np.conj(f1[:, :FFT_NZ1]).T / N_FFT
    fd = np.block([[g.real, -g.imag], [g.imag, g.real]])
    return tuple(jnp.asarray(a, dtype=BF16) for a in (fa, fak, t, ti, fd))


def _norm_matmul_kernel(x_ref, g_ref, w_ref, o_ref, h_ref):
    @pl.when(pl.program_id(1) == 0)
    def _():
        h_ref[...] = _rms(x_ref[...], g_ref[...]).astype(BF16)

    o_ref[...] = jnp.dot(h_ref[...], w_ref[...], preferred_element_type=F32).astype(o_ref.dtype)


def _norm_matmul(x, g, w, tm, tn, out_dtype):
    m, d = x.shape
    n = w.shape[1]
    return pl.pallas_call(
        _norm_matmul_kernel,
        grid=(m // tm, n // tn),
        in_specs=[pl.BlockSpec((tm, d), lambda i, j: (i, 0)),
                  pl.BlockSpec((1, d), lambda i, j: (0, 0)),
                  pl.BlockSpec((d, tn), lambda i, j: (0, j))],
        out_specs=pl.BlockSpec((tm, tn), lambda i, j: (i, j)),
        out_shape=jax.ShapeDtypeStruct((m, n), out_dtype),
        scratch_shapes=[pltpu.VMEM((tm, d), BF16)],
        compiler_params=_params("parallel", "arbitrary"),
        name="norm_matmul",
    )(x, g.reshape(1, d), w)


CIN_TM = 1024
CIN_TN = 512


def _conf_in_kernel(x_ref, g_ref, wa_ref, wg_ref, o_ref, h_ref):
    @pl.when(pl.program_id(1) == 0)
    def _():
        h_ref[...] = _rms(x_ref[...], g_ref[...]).astype(BF16)

    h = h_ref[...]
    a = jnp.dot(h, wa_ref[...], preferred_element_type=F32)
    g = jnp.dot(h, wg_ref[...], preferred_element_type=F32)
    o_ref[...] = (a * _sigmoid(g)).astype(o_ref.dtype)


def _conf_in(x, g, w_in):
    m, d = x.shape
    tm, tn = CIN_TM, CIN_TN
    ncol = CONF_WIDTH // tn
    return pl.pallas_call(
        _conf_in_kernel,
        grid=(m // tm, ncol),
        in_specs=[pl.BlockSpec((tm, d), lambda i, j: (i, 0)),
                  pl.BlockSpec((1, d), lambda i, j: (0, 0)),
                  pl.BlockSpec((d, tn), lambda i, j: (0, j)),
                  pl.BlockSpec((d, tn), lambda i, j: (0, ncol + j))],
        out_specs=pl.BlockSpec((tm, tn), lambda i, j: (i, j)),
        out_shape=jax.ShapeDtypeStruct((m, CONF_WIDTH), BF16),
        scratch_shapes=[pltpu.VMEM((tm, d), BF16)],
        compiler_params=_params("parallel", "arbitrary"),
        name="conf_in",
    )(x, g.reshape(1, d), w_in, w_in)


CONF_T = 512
CONF_HALO = 16
CONF_RC = 64


def _conformer_kernel(u_in, up_in, un_in, w_ref, b_ref, lg_ref, lb_ref, o_ref, u_ref, y_ref):
    i = pl.program_id(1)
    nt = pl.num_programs(1)
    t = CONF_T
    h = CONF_HALO
    prev_ok = (i > 0).astype(F32)
    next_ok = (i < nt - 1).astype(F32)
    nc = CONF_WIDTH // LANES
    for c in range(nc):
        cs = slice(c * LANES, (c + 1) * LANES)
        u_ref[c, 0:h, :] = up_in[0, :, cs].astype(F32) * prev_ok
        u_ref[c, h:h + t, :] = u_in[0, :, cs].astype(F32)
        u_ref[c, h + t:h + t + h, :] = un_in[0, :, cs].astype(F32) * next_ok

    pad = CONF_KERNEL // 2

    def conv_chunk(c, carry):
        w = w_ref[c]
        bias = b_ref[c]
        for r in range(t // CONF_RC):
            r0 = r * CONF_RC
            acc = jnp.zeros((CONF_RC, LANES), F32) + bias
            for j in range(CONF_KERNEL):
                acc = acc + w[j:j + 1, :] * u_ref[c, r0 + h - pad + j:r0 + h - pad + j + CONF_RC, :]
            y_ref[c, r0:r0 + CONF_RC, :] = acc
        return carry

    lax.fori_loop(0, nc, conv_chunk, 0)

    def ln_chunk(r, carry):
        r0 = pl.multiple_of(r * CONF_RC, CONF_RC)
        ys = [y_ref[c, pl.ds(r0, CONF_RC), :] for c in range(nc)]
        tot = ys[0]
        for y in ys[1:]:
            tot = tot + y
        mu = jnp.sum(tot, axis=-1, keepdims=True) * (1.0 / CONF_WIDTH)
        yc = [y - mu for y in ys]
        sq = yc[0] * yc[0]
        for y in yc[1:]:
            sq = sq + y * y
        inv = lax.rsqrt(jnp.sum(sq, axis=-1, keepdims=True) * (1.0 / CONF_WIDTH) + EPS)
        for c in range(nc):
            cs = slice(c * LANES, (c + 1) * LANES)
            yn = yc[c] * inv * lg_ref[:, cs] + lb_ref[:, cs]
            o_ref[0, pl.ds(r0, CONF_RC), cs] = (yn * _sigmoid(yn)).astype(o_ref.dtype)
        return carry

    lax.fori_loop(0, t // CONF_RC, ln_chunk, 0, unroll=2)


def _conformer(u, dw_w, dw_b, ln_g, ln_b):
    b, l, c = u.shape
    t, h = CONF_T, CONF_HALO
    nt = l // t
    hb = t // h
    last_hb = l // h - 1
    vec = pl.BlockSpec((1, c), lambda bi, i: (0, 0))
    nc = c // LANES
    chunked = lambda rows: pl.BlockSpec((nc, rows, LANES), lambda bi, i: (0, 0, 0))
    w3 = dw_w.reshape(CONF_KERNEL, nc, LANES).transpose(1, 0, 2)
    return pl.pallas_call(
        _conformer_kernel,
        grid=(b, nt),
        in_specs=[pl.BlockSpec((1, t, c), lambda bi, i: (bi, i, 0)),
                  pl.BlockSpec((1, h, c), lambda bi, i: (bi, jnp.maximum(i * hb - 1, 0), 0)),
                  pl.BlockSpec((1, h, c), lambda bi, i: (bi, jnp.minimum((i + 1) * hb, last_hb), 0)),
                  chunked(CONF_KERNEL), chunked(1), vec, vec],
        out_specs=pl.BlockSpec((1, t, c), lambda bi, i: (bi, i, 0)),
        out_shape=jax.ShapeDtypeStruct((b, l, c), BF16),
        scratch_shapes=[pltpu.VMEM((nc, t + 2 * h, LANES), F32), pltpu.VMEM((nc, t, LANES), F32)],
        compiler_params=_params("parallel", "parallel"),
        name="conformer",
    )(u, u, u, w3, dw_b.reshape(nc, 1, LANES), ln_g.reshape(1, c), ln_b.reshape(1, c))


HIN_TM = 1024
HIN_TN = 1024
HIN_HALO = 16
HIN_N1 = HIN_TM // FFT_N2
HIN_PITCH = FFT_N2 + 8


def _hyena_in_kernel(x_ref, xp_ref, xn_ref, g_ref, w_ref, cw_ref, cb_ref, o_ref, h_ref, e_ref, s_ref):
    i = pl.program_id(1)
    j = pl.program_id(2)
    nt = pl.num_programs(1)
    tm, hl = HIN_TM, HIN_HALO

    @pl.when(j == 0)
    def _():
        g = g_ref[...]
        h_ref[0:hl, :] = (_rms(xp_ref[0], g) * (i > 0).astype(F32)).astype(BF16)
        h_ref[hl:hl + tm, :] = _rms(x_ref[0], g).astype(BF16)
        h_ref[hl + tm:hl + tm + hl, :] = (_rms(xn_ref[0], g) * (i < nt - 1).astype(F32)).astype(BF16)

    e_ref[...] = jnp.dot(h_ref[...], w_ref[...], preferred_element_type=F32)
    nlc = HIN_TN // LANES
    for r in range(HIN_N1):
        r0 = hl + r * FFT_N2
        for lc in range(nlc):
            cs = slice(lc * LANES, (lc + 1) * LANES)
            acc = (cw_ref[0:1, cs] * e_ref[r0 - 1:r0 - 1 + FFT_N2, cs]
                   + cw_ref[1:2, cs] * e_ref[r0:r0 + FFT_N2, cs]
                   + cw_ref[2:3, cs] * e_ref[r0 + 1:r0 + 1 + FFT_N2, cs]
                   + cb_ref[:, cs])
            s_ref[lc, r * HIN_PITCH:r * HIN_PITCH + FFT_N2, :] = acc

    def gather(n2, carry):
        for lc in range(nlc):
            rows = s_ref[lc, pl.ds(n2, HIN_N1, stride=HIN_PITCH), :]
            o_ref[0, n2, :, lc * LANES:(lc + 1) * LANES] = rows.astype(o_ref.dtype)
        return carry

    lax.fori_loop(0, FFT_N2, gather, 0, unroll=8)


def _hyena_in(x, g, w_in, cw, cb):
    b, l, d = x.shape
    tm, tn, hl = HIN_TM, HIN_TN, HIN_HALO
    ncol = (HYENA_ORDER + 1) * HYENA_WIDTH // tn
    col0 = 2 * CONF_WIDTH // tn
    hb = tm // hl
    last_hb = l // hl - 1
    return pl.pallas_call(
        _hyena_in_kernel,
        grid=(b, l // tm, ncol),
        in_specs=[pl.BlockSpec((1, tm, d), lambda bi, i, j: (bi, i, 0)),
                  pl.BlockSpec((1, hl, d), lambda bi, i, j: (bi, jnp.maximum(i * hb - 1, 0), 0)),
                  pl.BlockSpec((1, hl, d), lambda bi, i, j: (bi, jnp.minimum((i + 1) * hb, last_hb), 0)),
                  pl.BlockSpec((1, d), lambda bi, i, j: (0, 0)),
                  pl.BlockSpec((d, tn), lambda bi, i, j: (0, col0 + j)),
                  pl.BlockSpec((SHORT_KERNEL, tn), lambda bi, i, j: (0, j)),
                  pl.BlockSpec((1, tn), lambda bi, i, j: (0, j))],
        out_specs=pl.BlockSpec((1, FFT_N2, HIN_N1, tn), lambda bi, i, j: (bi, 0, i, j)),
        out_shape=jax.ShapeDtypeStruct((b, FFT_N2, FFT_NZ1, ncol * tn), BF16),
        scratch_shapes=[pltpu.VMEM((tm + 2 * hl, d), BF16),
                        pltpu.VMEM((tm + 2 * hl, tn), F32),
                        pltpu.VMEM((tn // LANES, HIN_N1 * HIN_PITCH, LANES), F32)],
        compiler_params=_params("parallel", "parallel", "arbitrary"),
        name="hyena_in",
    )(x, x, x, g.reshape(1, d), w_in, cw, cb.reshape(1, -1))


def _filter_mlp_kernel(z_ref, w1_ref, b1_ref, wi_ref, bi_ref, fr_ref, o_ref):
    hi = lax.Precision.HIGHEST
    fr = fr_ref[...]
    h = jnp.sin(fr * (jnp.dot(z_ref[...], w1_ref[...], precision=hi, preferred_element_type=F32) + b1_ref[...]))
    for i in range(HYENA_INNER_MLPS):
        h = jnp.sin(fr * (jnp.dot(h, wi_ref[i], precision=hi, preferred_element_type=F32) + bi_ref[i]))
    o_ref[...] = h


def _filter_mlp(z, w1, b1, w_inner, b_inner, freq):
    pad = lambda a, shape: jnp.zeros(shape, F32).at[tuple(slice(0, s) for s in a.shape)].set(a)
    zp = pad(z, (SEQ, LANES))
    w1p = pad(w1, (LANES, LANES))
    b1p = pad(b1.reshape(1, -1), (1, LANES))
    wip = pad(w_inner, (HYENA_INNER_MLPS, LANES, LANES))
    bip = pad(b_inner.reshape(HYENA_INNER_MLPS, 1, -1), (HYENA_INNER_MLPS, 1, LANES))
    frp = pad(freq.reshape(1, -1), (1, LANES))
    return pl.pallas_call(
        _filter_mlp_kernel,
        out_shape=jax.ShapeDtypeStruct((SEQ, LANES), F32),
        compiler_params=pltpu.CompilerParams(vmem_limit_bytes=VMEM_LIMIT),
        name="filter_mlp",
    )(zp, w1p, b1p, wip, bip, frp)


def _stage_b_rhs(a_ref, k1):
    ar = a_ref[pl.ds(k1, FFT_N2, stride=FFT_PITCH), :]
    ai = a_ref[pl.ds(FFT_N1 + k1, FFT_N2, stride=FFT_PITCH), :]
    return jnp.concatenate([ar, ai], axis=0).astype(BF16)


def _split_bf16(a):
    hi = a.astype(BF16)
    return hi, (a - hi.astype(F32)).astype(BF16)


def _filter_spectrum_kernel(h_ref, w_ref, dl_ref, fak_ref, t_ref, o_ref, kt_ref, a_ref):
    rc = FILTER_RC
    w_hi, w_lo = _split_bf16(jnp.concatenate([w_ref[0, 0], w_ref[0, 1]], axis=1))
    dl = jnp.concatenate([dl_ref[...], dl_ref[...]], axis=1)

    def gen(r, ss):
        r0 = pl.multiple_of(r * rc, rc)
        rows = lax.broadcasted_iota(jnp.int32, (rc, 2 * LANES), 0) + r0
        lanes = lax.broadcasted_iota(jnp.int32, (rc, 2 * LANES), 1)
        decay = jnp.exp(-(rows.astype(F32) * (1.0 / (SEQ - 1))) * dl)
        h_hi, h_lo = _split_bf16(h_ref[pl.ds(r0, rc), :])
        k = jnp.dot(h_hi, w_hi, preferred_element_type=F32)
        k = k + (jnp.dot(h_hi, w_lo, preferred_element_type=F32) + jnp.dot(h_lo, w_hi, preferred_element_type=F32))
        k = k * decay
        k = jnp.where((rows >= SEQ - 1) & (lanes >= LANES), 0.0, k)
        for n in range(rc // FFT_N2):
            dst = pl.ds(pl.multiple_of((r * (rc // FFT_N2) + n) * FILTER_PITCH, 8), FFT_N2)
            kt_ref[0, dst, :] = k[n * FFT_N2:(n + 1) * FFT_N2, :LANES]
            kt_ref[1, dst, :] = k[n * FFT_N2:(n + 1) * FFT_N2, LANES:]
        return ss + jnp.sum(k * k, axis=0, keepdims=True)

    ss = lax.fori_loop(0, SEQ // rc, gen, jnp.zeros((1, 2 * LANES), F32))
    scale = lax.rsqrt(ss[:, :LANES] + ss[:, LANES:] + EPS)

    def stage_a(n2, carry):
        fwd = kt_ref[0, pl.ds(n2, FFT_NZ1, stride=FILTER_PITCH), :]
        bwd = kt_ref[1, pl.ds(FFT_N2 - 1 - n2, FFT_NZ1, stride=FILTER_PITCH), :]
        rhs = (jnp.concatenate([fwd, bwd], axis=0) * scale).astype(BF16)
        a = jnp.dot(fak_ref[...], rhs, preferred_element_type=F32)
        a_ref[pl.ds(pl.multiple_of(n2 * FFT_PITCH, 8), 2 * FFT_N1), :] = a
        return carry

    lax.fori_loop(0, FFT_N2, stage_a, 0, unroll=FFT_UNROLL)

    def stage_b(k1, carry):
        b = jnp.dot(t_ref[k1], _stage_b_rhs(a_ref, k1), preferred_element_type=F32)
        o_ref[0, k1] = b.astype(o_ref.dtype)
        return carry

    lax.fori_loop(0, FFT_N1, stage_b, 0, unroll=FFT_UNROLL)


def _filter_spectrum(hmlp, w_filt, deltas, fak, t):
    wf = w_filt.reshape(HYENA_FILTER_ORDER, HYENA_ORDER, 2, HYENA_WIDTH).transpose(1, 2, 0, 3)
    wf = jnp.zeros((HYENA_ORDER, 2, LANES, HYENA_WIDTH), F32).at[:, :, :HYENA_FILTER_ORDER].set(wf)
    ncb = HYENA_WIDTH // LANES
    return pl.pallas_call(
        _filter_spectrum_kernel,
        grid=(HYENA_ORDER, ncb),
        in_specs=[_const_spec((SEQ, LANES)),
                  pl.BlockSpec((1, 2, LANES, LANES), lambda o, c: (o, 0, 0, c)),
                  pl.BlockSpec((1, LANES), lambda o, c: (0, c)),
                  _const_spec((2 * FFT_N1, FFT_N1)),
                  _const_spec((FFT_N1, 2 * FFT_N2, 2 * FFT_N2))],
        out_specs=pl.BlockSpec((1, FFT_N1, 2 * FFT_N2, LANES), lambda o, c: (o, 0, 0, c)),
        out_shape=jax.ShapeDtypeStruct((HYENA_ORDER, FFT_N1, 2 * FFT_N2, HYENA_WIDTH), BF16),
        scratch_shapes=[pltpu.VMEM((2, FFT_NZ1 * FILTER_PITCH, LANES), F32),
                        pltpu.VMEM((FFT_N2 * FFT_PITCH, LANES), F32)],
        compiler_params=_params("parallel", "parallel"),
        name="filter_spectrum",
    )(hmlp, wf, deltas, fak, t)


def _hyena_conv_kernel(u_ref, xm_ref, kf_ref, sk_ref, fa_ref, t_ref, ti_ref, fd_ref, o_ref, a_ref, y_ref, *,
                       slab_order_out):
    def stage_a(n2, carry):
        rhs = jnp.concatenate([u_ref[0, n2], u_ref[1, n2]], axis=0).astype(BF16)
        a = jnp.dot(fa_ref[...], rhs, preferred_element_type=F32)
        a_ref[pl.ds(pl.multiple_of(n2 * FFT_PITCH, 8), 2 * FFT_N1), :] = a
        return carry

    lax.fori_loop(0, FFT_N2, stage_a, 0, unroll=FFT_UNROLL)

    def stage_b(k1, carry):
        b = jnp.dot(t_ref[k1], _stage_b_rhs(a_ref, k1), preferred_element_type=F32)
        kf = kf_ref[0, k1].astype(F32)
        br, bi = b[:FFT_N2], b[FFT_N2:]
        kr, ki = kf[:FFT_N2], kf[FFT_N2:]
        y_ref[k1] = jnp.concatenate([br * kr - bi * ki, br * ki + bi * kr], axis=0).astype(BF16)
        return carry

    lax.fori_loop(0, FFT_N1, stage_b, 0, unroll=FFT_UNROLL)

    def stage_c(k1, carry):
        c = jnp.dot(ti_ref[k1], y_ref[k1], preferred_element_type=F32)
        a_ref[pl.ds(k1, FFT_N2, stride=FFT_PITCH), :] = c[:FFT_N2]
        a_ref[pl.ds(FFT_N1 + k1, FFT_N2, stride=FFT_PITCH), :] = c[FFT_N2:]
        return carry

    lax.fori_loop(0, FFT_N1, stage_c, 0, unroll=FFT_UNROLL)

    sk = sk_ref[0]

    def stage_d(n2, carry):
        cb = a_ref[pl.ds(pl.multiple_of(n2 * FFT_PITCH, 8), 2 * FFT_N1), :].astype(BF16)
        y = jnp.dot(fd_ref[...], cb, preferred_element_type=F32)
        for bi in range(2):
            yb = y[bi * FFT_NZ1:(bi + 1) * FFT_NZ1]
            z = xm_ref[bi, n2].astype(F32) * (yb + sk * u_ref[bi, n2].astype(F32))
            if slab_order_out:
                o_ref[bi, n2] = z.astype(o_ref.dtype)
            else:
                o_ref[bi, pl.ds(n2, FFT_NZ1, stride=FFT_N2), :] = z
        return carry

    lax.fori_loop(0, FFT_N2, stage_d, 0, unroll=FFT_UNROLL)


def _hyena_conv(u, u_col0, xm, xm_col0, kf, order, skip, consts, slab_order_out):
    fa, _, t, ti, fd = consts
    ncb = HYENA_WIDTH // LANES
    slab_block = (BATCH, FFT_N2, FFT_NZ1, LANES)
    if slab_order_out:
        out_spec = pl.BlockSpec(slab_block, lambda c: (0, 0, 0, c))
        out_shape = jax.ShapeDtypeStruct((BATCH, FFT_N2, FFT_NZ1, HYENA_WIDTH), BF16)
    else:
        out_spec = pl.BlockSpec((BATCH, SEQ, LANES), lambda c: (0, 0, c))
        out_shape = jax.ShapeDtypeStruct((BATCH, SEQ, HYENA_WIDTH), F32)
    return pl.pallas_call(
        functools.partial(_hyena_conv_kernel, slab_order_out=slab_order_out),
        grid=(ncb,),
        in_specs=[pl.BlockSpec(slab_block, lambda c: (0, 0, 0, u_col0 + c)),
                  pl.BlockSpec(slab_block, lambda c: (0, 0, 0, xm_col0 + c)),
                  pl.BlockSpec((1, FFT_N1, 2 * FFT_N2, LANES), lambda c: (order, 0, 0, c)),
                  pl.BlockSpec((1, 1, LANES), lambda c: (order, 0, c)),
                  _const_spec(fa.shape), _const_spec(t.shape), _const_spec(ti.shape), _const_spec(fd.shape)],
        out_specs=out_spec,
        out_shape=out_shape,
        scratch_shapes=[pltpu.VMEM((FFT_N2 * FFT_PITCH, LANES), F32),
                        pltpu.VMEM((FFT_N1, 2 * FFT_N2, LANES), BF16)],
        compiler_params=_params("parallel"),
        name="hyena_conv",
    )(u, xm, kf, skip.reshape(HYENA_ORDER, 1, HYENA_WIDTH), fa, t, ti, fd)


MA_TM = 256


def _mix_attn_kernel(x_ref, yc_ref, yh_ref, wa_ref, wb_ref, g_ref, wq_ref, k_ref, v_ref, wo_ref,
                     o_ref, x1_ref, q_ref, ctx_ref):
    mix = jnp.dot(yc_ref[0], wa_ref[...], preferred_element_type=F32)
    mix = mix + jnp.dot(yh_ref[0].astype(BF16), wb_ref[...], preferred_element_type=F32)
    x1 = x_ref[0] + mix
    x1_ref[...] = x1
    q_ref[...] = jnp.dot(_rms(x1, g_ref[...]).astype(BF16), wq_ref[...],
                         preferred_element_type=F32).astype(BF16)
    hd = XATTN_HEAD_DIM
    for h in range(XATTN_HEADS):
        hs = slice(h * hd, (h + 1) * hd)
        s = lax.dot_general(q_ref[:, hs], k_ref[0, :, hs], (((1,), (1,)), ((), ())),
                            preferred_element_type=F32) * (hd ** -0.5)
        s = s - jnp.max(s, axis=-1, keepdims=True)
        e = jnp.exp(s)
        pr = e / jnp.sum(e, axis=-1, keepdims=True)
        ctx_ref[:, hs] = jnp.dot(pr.astype(BF16), v_ref[0, :, hs], preferred_element_type=F32).astype(BF16)
    o_ref[0] = x1_ref[...] + jnp.dot(ctx_ref[...], wo_ref[...], preferred_element_type=F32)


def _mix_attn(x, yc, yh, w_out, g, wq, k, v, wo):
    b, l, d = x.shape
    tm = MA_TM
    kc = yc.shape[-1]
    tok = lambda width: pl.BlockSpec((1, tm, width), lambda bi, i: (bi, i, 0))
    mem = pl.BlockSpec((1, N_MEM, d), lambda bi, i: (bi, 0, 0))
    return pl.pallas_call(
        _mix_attn_kernel,
        grid=(b, l // tm),
        in_specs=[tok(d), tok(kc), tok(kc),
                  _const_spec((kc, d), (0, 0)), _const_spec((kc, d), (1, 0)),
                  pl.BlockSpec((1, d), lambda bi, i: (0, 0)),
                  _const_spec((d, d)), mem, mem, _const_spec((d, d))],
        out_specs=tok(d),
        out_shape=jax.ShapeDtypeStruct((b, l, d), F32),
        scratch_shapes=[pltpu.VMEM((tm, d), F32), pltpu.VMEM((tm, d), BF16), pltpu.VMEM((tm, d), BF16)],
        compiler_params=_params("parallel", "parallel"),
        name="mix_attn",
    )(x, yc, yh, w_out, w_out, g.reshape(1, d), wq, k, v, wo)


FFN_TM = 1024
FFN_FC = 512
FFN_VMEM_LIMIT = 62 * 1024 * 1024
FFN_HALO = 16


def _ffn_kernel(x_ref, xp_ref, xn_ref, g_ref, wg_ref, wv_ref, cw_ref, cb_ref, wd_ref, gf_ref,
                o_ref, h_ref, ge_ref):
    i = pl.program_id(1)
    j = pl.program_id(2)
    nt = pl.num_programs(1)
    nf = pl.num_programs(2)
    tm, hl = FFN_TM, FFN_HALO

    @pl.when(j == 0)
    def _():
        g = g_ref[...]
        h_ref[0:hl, :] = (_rms(xp_ref[0], g) * (i > 0).astype(F32)).astype(BF16)
        h_ref[hl:hl + tm, :] = _rms(x_ref[0], g).astype(BF16)
        h_ref[hl + tm:hl + tm + hl, :] = (_rms(xn_ref[0], g) * (i < nt - 1).astype(F32)).astype(BF16)
        o_ref[...] = jnp.zeros_like(o_ref)

    ge_ref[...] = jnp.dot(h_ref[...], wg_ref[...], preferred_element_type=F32)
    val = jnp.dot(h_ref[pl.ds(hl, tm), :], wv_ref[...], preferred_element_type=F32)
    gate = (cw_ref[0:1, :] * ge_ref[pl.ds(hl - 1, tm), :]
            + cw_ref[1:2, :] * ge_ref[pl.ds(hl, tm), :]
            + cw_ref[2:3, :] * ge_ref[pl.ds(hl + 1, tm), :]
            + cb_ref[...])
    act = (gate * _sigmoid(gate) * val).astype(BF16)
    o_ref[0] += jnp.dot(act, wd_ref[...], preferred_element_type=F32)

    @pl.when(j == nf - 1)
    def _():
        o_ref[0] = _rms(x_ref[0] + o_ref[0], gf_ref[...])


def _ffn(x, g_ffn, w_up, dw_w, dw_b, w_down, g_final):
    b, l, d = x.shape
    tm, fc, hl = FFN_TM, FFN_FC, FFN_HALO
    nf = D_FF // fc
    hb = tm // hl
    last_hb = l // hl - 1
    return pl.pallas_call(
        _ffn_kernel,
        grid=(b, l // tm, nf),
        in_specs=[pl.BlockSpec((1, tm, d), lambda bi, i, j: (bi, i, 0)),
                  pl.BlockSpec((1, hl, d), lambda bi, i, j: (bi, jnp.maximum(i * hb - 1, 0), 0)),
                  pl.BlockSpec((1, hl, d), lambda bi, i, j: (bi, jnp.minimum((i + 1) * hb, last_hb), 0)),
                  pl.BlockSpec((1, d), lambda bi, i, j: (0, 0)),
                  pl.BlockSpec((d, fc), lambda bi, i, j: (0, j)),
                  pl.BlockSpec((d, fc), lambda bi, i, j: (0, nf + j)),
                  pl.BlockSpec((3, fc), lambda bi, i, j: (0, j)),
                  pl.BlockSpec((1, fc), lambda bi, i, j: (0, j)),
                  pl.BlockSpec((fc, d), lambda bi, i, j: (j, 0)),
                  pl.BlockSpec((1, d), lambda bi, i, j: (0, 0))],
        out_specs=pl.BlockSpec((1, tm, d), lambda bi, i, j: (bi, i, 0)),
        out_shape=jax.ShapeDtypeStruct((b, l, d), F32),
        scratch_shapes=[pltpu.VMEM((tm + 2 * hl, d), BF16),
                        pltpu.VMEM((tm + 2 * hl, fc), F32)],
        compiler_params=_params("parallel", "parallel", "arbitrary", vmem_limit=FFN_VMEM_LIMIT),
        name="ffn",
    )(x, x, x, g_ffn.reshape(1, d), w_up, w_up, dw_w, dw_b.reshape(1, -1), w_down, g_final.reshape(1, d))


def _position_features(length):
    t = jnp.linspace(0.0, 1.0, length, dtype=F32)[:, None]
    w = 2.0 * math.pi * jnp.arange(length, dtype=F32)[:, None] / length
    bands = (HYENA_EMB - 1) // 2
    f = jnp.linspace(1e-4, bands - 1, bands, dtype=F32)[None, :]
    return jnp.concatenate([t, jnp.cos(f * w), -jnp.sin(f * w)], axis=-1)


def _decay_rates():
    max_decay = math.log(HYENA_TARGET) / HYENA_FAST_DECAY
    min_decay = math.log(HYENA_TARGET) / HYENA_SLOW_DECAY
    return jnp.abs(jnp.linspace(min_decay, max_decay, HYENA_WIDTH, dtype=F32)).reshape(1, HYENA_WIDTH)


def kernel(x, mem, g_mix, w_in, conf_dw_w, conf_dw_b, conf_ln_g, conf_ln_b, hyena_short_w, hyena_short_b, hyena_w1, hyena_b1, hyena_w_inner, hyena_b_inner, hyena_w_filt, hyena_sin_freq, hyena_skip, w_out, g_xattn, g_mem, w_q, w_k, w_v, w_o, g_ffn, w_ffn_up, ffn_dw_w, ffn_dw_b, w_ffn_down, g_final):
    b, l, d = x.shape
    consts = _dft_constants()
    w_in_b = w_in[0].astype(BF16)

    u_conf = _conf_in(x.reshape(b * l, d), g_mix[0], w_in_b).reshape(b, l, CONF_WIDTH)
    y_conf = _conformer(u_conf, conf_dw_w[0], conf_dw_b[0], conf_ln_g[0], conf_ln_b[0])

    uc = _hyena_in(x, g_mix[0], w_in_b, hyena_short_w[0], hyena_short_b[0])
    hmlp = _filter_mlp(_position_features(l), hyena_w1[0], hyena_b1[0], hyena_w_inner[0],
                       hyena_b_inner[0], hyena_sin_freq[0])
    kf = _filter_spectrum(hmlp, hyena_w_filt[0], _decay_rates(), consts[1], consts[2])
    ncb = HYENA_WIDTH // LANES
    z1 = _hyena_conv(uc, 0, uc, ncb, kf, 0, hyena_skip[0], consts, slab_order_out=True)
    z2 = _hyena_conv(z1, 0, uc, 2 * ncb, kf, 1, hyena_skip[0], consts, slab_order_out=False)

    memf = mem.reshape(b * N_MEM, d)
    k = _norm_matmul(memf, g_mem[0], w_k[0].astype(BF16), b * N_MEM, 1024, BF16).reshape(b, N_MEM, d)
    v = _norm_matmul(memf, g_mem[0], w_v[0].astype(BF16), b * N_MEM, 1024, BF16).reshape(b, N_MEM, d)
    x2 = _mix_attn(x, y_conf, z2, w_out[0].astype(BF16), g_xattn[0], w_q[0].astype(BF16), k, v,
                   w_o[0].astype(BF16))

    return _ffn(x2, g_ffn[0], w_ffn_up[0].astype(BF16), ffn_dw_w[0], ffn_dw_b[0],
                w_ffn_down[0].astype(BF16), g_final)
```

```python
import functools
import math

import numpy as np
import jax
import jax.numpy as jnp
from jax import lax
from jax.experimental import pallas as pl
from jax.experimental.pallas import tpu as pltpu

F32 = jnp.float32
BF16 = jnp.bfloat16

D_MODEL = 2048
BATCH = 2
SEQ = 4096
CONF_WIDTH = 1024
HYENA_WIDTH = 1024
CONF_KERNEL = 31
SHORT_KERNEL = 3
HYENA_ORDER = 2
HYENA_EMB = 33
HYENA_FILTER_ORDER = 64
HYENA_INNER_MLPS = 2
HYENA_FAST_DECAY = 0.3
HYENA_SLOW_DECAY = 1.5
HYENA_TARGET = 1e-2
N_MEM = 256
XATTN_HEADS = 4
XATTN_HEAD_DIM = D_MODEL // XATTN_HEADS
D_FF = 5632
EPS = 1e-6
IN_COLS = 2 * CONF_WIDTH + (HYENA_ORDER + 1) * HYENA_WIDTH

LANES = 128
VMEM_LIMIT = 56 * 1024 * 1024

N_FFT = 2 * SEQ
FFT_N1 = 128
FFT_N2 = 64
FFT_NZ1 = SEQ // FFT_N2
FFT_PITCH = 2 * FFT_N1 + 8
FFT_UNROLL = 16
FILTER_RC = 512
FILTER_PITCH = FFT_N2 + 8


def _params(*sem, vmem_limit=VMEM_LIMIT):
    return pltpu.CompilerParams(dimension_semantics=sem, vmem_limit_bytes=vmem_limit)


def _const_spec(shape, index=None):
    index = (0,) * len(shape) if index is None else index
    return pl.BlockSpec(shape, lambda *_: index, pipeline_mode=pl.Buffered(1))


def _sigmoid(x):
    return 0.5 * (jnp.tanh(0.5 * x) + 1.0)


def _rms(x, g):
    ms = jnp.mean(x * x, axis=-1, keepdims=True)
    return x * lax.rsqrt(ms + EPS) * g


@functools.lru_cache(maxsize=None)
def _dft_constants():
    k1 = np.arange(FFT_N1)
    f1 = np.exp(-2j * np.pi * np.outer(k1, k1) / FFT_N1)
    fh = f1[:, :FFT_NZ1]
    fa = np.block([[fh.real, -fh.imag], [fh.imag, fh.real]])
    perm = np.concatenate([np.arange(FFT_NZ1), FFT_N1 - 1 - np.arange(FFT_N1 - FFT_NZ1)])
    fp = f1[:, perm]
    fak = np.concatenate([fp.real, fp.imag], axis=0)
    n2 = np.arange(FFT_N2)
    kk = k1[:, None, None] + FFT_N1 * n2[None, :, None]
    ang = (n2[None, None, :] * kk) % N_FFT
    tc = np.exp(-2j * np.pi * ang / N_FFT)
    t = np.concatenate([np.concatenate([tc.real, -tc.imag], 2),
                        np.concatenate([tc.imag, tc.real], 2)], 1)
    ti = np.transpose(t, (0, 2, 1))
    g = np.conj(f1[:, :FFT_NZ1]).T / N_FFT
    fd = np.block([[g.real, -g.imag], [g.imag, g.real]])
    return tuple(jnp.asarray(a, dtype=BF16) for a in (fa, fak, t, ti, fd))


def _norm_matmul_kernel(x_ref, g_ref, w_ref, o_ref, h_ref):
    @pl.when(pl.program_id(1) == 0)
    def _():
        h_ref[...] = _rms(x_ref[...], g_ref[...]).astype(BF16)

    o_ref[...] = jnp.dot(h_ref[...], w_ref[...].astype(BF16), preferred_element_type=F32).astype(o_ref.dtype)


def _norm_matmul(x, g, w, tm, tn, out_dtype):
    m, d = x.shape
    n = w.shape[1]
    return pl.pallas_call(
        _norm_matmul_kernel,
        grid=(m // tm, n // tn),
        in_specs=[pl.BlockSpec((tm, d), lambda i, j: (i, 0)),
                  pl.BlockSpec((1, d), lambda i, j: (0, 0)),
                  pl.BlockSpec((d, tn), lambda i, j: (0, j))],
        out_specs=pl.BlockSpec((tm, tn), lambda i, j: (i, j)),
        out_shape=jax.ShapeDtypeStruct((m, n), out_dtype),
        scratch_shapes=[pltpu.VMEM((tm, d), BF16)],
        compiler_params=_params("parallel", "arbitrary"),
        name="norm_matmul",
    )(x, g.reshape(1, d), w)


CIN_TM = 1024
CIN_TN = 512


def _conf_in_kernel(x_ref, g_ref, wa_ref, wg_ref, o_ref, h_ref):
    @pl.when(pl.program_id(1) == 0)
    def _():
        h_ref[...] = _rms(x_ref[...], g_ref[...]).astype(BF16)

    h = h_ref[...]
    a = jnp.dot(h, wa_ref[...], preferred_element_type=F32)
    g = jnp.dot(h, wg_ref[...], preferred_element_type=F32)
    o_ref[...] = (a * _sigmoid(g)).astype(o_ref.dtype)


def _conf_in(x, g, w_in):
    m, d = x.shape
    tm, tn = CIN_TM, CIN_TN
    ncol = CONF_WIDTH // tn
    return pl.pallas_call(
        _conf_in_kernel,
        grid=(m // tm, ncol),
        in_specs=[pl.BlockSpec((tm, d), lambda i, j: (i, 0)),
                  pl.BlockSpec((1, d), lambda i, j: (0, 0)),
                  pl.BlockSpec((d, tn), lambda i, j: (0, j)),
                  pl.BlockSpec((d, tn), lambda i, j: (0, ncol + j))],
        out_specs=pl.BlockSpec((tm, tn), lambda i, j: (i, j)),
        out_shape=jax.ShapeDtypeStruct((m, CONF_WIDTH), BF16),
        scratch_shapes=[pltpu.VMEM((tm, d), BF16)],
        compiler_params=_params("parallel", "arbitrary"),
        name="conf_in",
    )(x, g.reshape(1, d), w_in, w_in)


CONF_T = 512
CONF_HALO = 16
CONF_RC = 64


def _conformer_kernel(u_in, up_in, un_in, w_ref, b_ref, lg_ref, lb_ref, o_ref, u_ref, y_ref):
    i = pl.program_id(1)
    nt = pl.num_programs(1)
    t = CONF_T
    h = CONF_HALO
    prev_ok = (i > 0).astype(F32)
    next_ok = (i < nt - 1).astype(F32)
    nc = CONF_WIDTH // LANES
    for c in range(nc):
        cs = slice(c * LANES, (c + 1) * LANES)
        u_ref[c, 0:h, :] = up_in[0, :, cs].astype(F32) * prev_ok
        u_ref[c, h:h + t, :] = u_in[0, :, cs].astype(F32)
        u_ref[c, h + t:h + t + h, :] = un_in[0, :, cs].astype(F32) * next_ok

    pad = CONF_KERNEL // 2

    def conv_chunk(c, carry):
        w = w_ref[c]
        bias = b_ref[c]
        for r in range(t // CONF_RC):
            r0 = r * CONF_RC
            acc = jnp.zeros((CONF_RC, LANES), F32) + bias
            for j in range(CONF_KERNEL):
                acc = acc + w[j:j + 1, :] * u_ref[c, r0 + h - pad + j:r0 + h - pad + j + CONF_RC, :]
            y_ref[c, r0:r0 + CONF_RC, :] = acc
        return carry

    lax.fori_loop(0, nc, conv_chunk, 0)

    def ln_chunk(r, carry):
        r0 = pl.multiple_of(r * CONF_RC, CONF_RC)
        ys = [y_ref[c, pl.ds(r0, CONF_RC), :] for c in range(nc)]
        tot = ys[0]
        for y in ys[1:]:
            tot = tot + y
        mu = jnp.sum(tot, axis=-1, keepdims=True) * (1.0 / CONF_WIDTH)
        yc = [y - mu for y in ys]
        sq = yc[0] * yc[0]
        for y in yc[1:]:
            sq = sq + y * y
        inv = lax.rsqrt(jnp.sum(sq, axis=-1, keepdims=True) * (1.0 / CONF_WIDTH) + EPS)
        for c in range(nc):
            cs = slice(c * LANES, (c + 1) * LANES)
            yn = yc[c] * inv * lg_ref[:, cs] + lb_ref[:, cs]
            o_ref[0, pl.ds(r0, CONF_RC), cs] = (yn * _sigmoid(yn)).astype(o_ref.dtype)
        return carry

    lax.fori_loop(0, t // CONF_RC, ln_chunk, 0, unroll=2)


def _conformer(u, dw_w, dw_b, ln_g, ln_b):
    b, l, c = u.shape
    t, h = CONF_T, CONF_HALO
    nt = l // t
    hb = t // h
    last_hb = l // h - 1
    vec = pl.BlockSpec((1, c), lambda bi, i: (0, 0))
    nc = c // LANES
    chunked = lambda rows: pl.BlockSpec((nc, rows, LANES), lambda bi, i: (0, 0, 0))
    w3 = dw_w.reshape(CONF_KERNEL, nc, LANES).transpose(1, 0, 2)
    return pl.pallas_call(
        _conformer_kernel,
        grid=(b, nt),
        in_specs=[pl.BlockSpec((1, t, c), lambda bi, i: (bi, i, 0)),
                  pl.BlockSpec((1, h, c), lambda bi, i: (bi, jnp.maximum(i * hb - 1, 0), 0)),
                  pl.BlockSpec((1, h, c), lambda bi, i: (bi, jnp.minimum((i + 1) * hb, last_hb), 0)),
                  chunked(CONF_KERNEL), chunked(1), vec, vec],
        out_specs=pl.BlockSpec((1, t, c), lambda bi, i: (bi, i, 0)),
        out_shape=jax.ShapeDtypeStruct((b, l, c), BF16),
        scratch_shapes=[pltpu.VMEM((nc, t + 2 * h, LANES), F32), pltpu.VMEM((nc, t, LANES), F32)],
        compiler_params=_params("parallel", "parallel"),
        name="conformer",
    )(u, u, u, w3, dw_b.reshape(nc, 1, LANES), ln_g.reshape(1, c), ln_b.reshape(1, c))


HIN_TM = 1024
HIN_TN = 1024
HIN_HALO = 16
HIN_N1 = HIN_TM // FFT_N2
HIN_PITCH = FFT_N2 + 8


def _hyena_in_kernel(x_ref, xp_ref, xn_ref, g_ref, w_ref, cw_ref, cb_ref, o_ref, h_ref, e_ref, s_ref):
    i = pl.program_id(1)
    j = pl.program_id(2)
    nt = pl.num_programs(1)
    tm, hl = HIN_TM, HIN_HALO

    @pl.when(j == 0)
    def _():
        g = g_ref[...]
        h_ref[0:hl, :] = (_rms(xp_ref[0], g) * (i > 0).astype(F32)).astype(BF16)
        h_ref[hl:hl + tm, :] = _rms(x_ref[0], g).astype(BF16)
        h_ref[hl + tm:hl + tm + hl, :] = (_rms(xn_ref[0], g) * (i < nt - 1).astype(F32)).astype(BF16)

    e_ref[...] = jnp.dot(h_ref[...], w_ref[...], preferred_element_type=F32)
    nlc = HIN_TN // LANES
    for r in range(HIN_N1):
        r0 = hl + r * FFT_N2
        for lc in range(nlc):
            cs = slice(lc * LANES, (lc + 1) * LANES)
            acc = (cw_ref[0:1, cs] * e_ref[r0 - 1:r0 - 1 + FFT_N2, cs]
                   + cw_ref[1:2, cs] * e_ref[r0:r0 + FFT_N2, cs]
                   + cw_ref[2:3, cs] * e_ref[r0 + 1:r0 + 1 + FFT_N2, cs]
                   + cb_ref[:, cs])
            s_ref[lc, r * HIN_PITCH:r * HIN_PITCH + FFT_N2, :] = acc

    def gather(n2, carry):
        for lc in range(nlc):
            rows = s_ref[lc, pl.ds(n2, HIN_N1, stride=HIN_PITCH), :]
            o_ref[0, n2, :, lc * LANES:(lc + 1) * LANES] = rows.astype(o_ref.dtype)
        return carry

    lax.fori_loop(0, FFT_N2, gather, 0, unroll=8)


def _hyena_in(x, g, w_in, cw, cb):
    b, l, d = x.shape
    tm, tn, hl = HIN_TM, HIN_TN, HIN_HALO
    ncol = (HYENA_ORDER + 1) * HYENA_WIDTH // tn
    col0 = 2 * CONF_WIDTH // tn
    hb = tm // hl
    last_hb = l // hl - 1
    return pl.pallas_call(
        _hyena_in_kernel,
        grid=(b, l // tm, ncol),
        in_specs=[pl.BlockSpec((1, tm, d), lambda bi, i, j: (bi, i, 0)),
                  pl.BlockSpec((1, hl, d), lambda bi, i, j: (bi, jnp.maximum(i * hb - 1, 0), 0)),
                  pl.BlockSpec((1, hl, d), lambda bi, i, j: (bi, jnp.minimum((i + 1) * hb, last_hb), 0)),
                  pl.BlockSpec((1, d), lambda bi, i, j: (0, 0)),
                  pl.BlockSpec((d, tn), lambda bi, i, j: (0, col0 + j)),
                  pl.BlockSpec((SHORT_KERNEL, tn), lambda bi, i, j: (0, j)),
                  pl.BlockSpec((1, tn), lambda bi, i, j: (0, j))],
        out_specs=pl.BlockSpec((1, FFT_N2, HIN_N1, tn), lambda bi, i, j: (bi, 0, i, j)),
        out_shape=jax.ShapeDtypeStruct((b, FFT_N2, FFT_NZ1, ncol * tn), BF16),
        scratch_shapes=[pltpu.VMEM((tm + 2 * hl, d), BF16),
                        pltpu.VMEM((tm + 2 * hl, tn), F32),
                        pltpu.VMEM((tn // LANES, HIN_N1 * HIN_PITCH, LANES), F32)],
        compiler_params=_params("parallel", "parallel", "arbitrary"),
        name="hyena_in",
    )(x, x, x, g.reshape(1, d), w_in, cw, cb.reshape(1, -1))


def _filter_mlp_kernel(z_ref, w1_ref, b1_ref, wi_ref, bi_ref, fr_ref, o_ref):
    hi = lax.Precision.HIGHEST
    fr = fr_ref[...]
    h = jnp.sin(fr * (jnp.dot(z_ref[...], w1_ref[...], precision=hi, preferred_element_type=F32) + b1_ref[...]))
    for i in range(HYENA_INNER_MLPS):
        h = jnp.sin(fr * (jnp.dot(h, wi_ref[i], precision=hi, preferred_element_type=F32) + bi_ref[i]))
    o_ref[...] = h


def _filter_mlp(z, w1, b1, w_inner, b_inner, freq):
    pad = lambda a, shape: jnp.zeros(shape, F32).at[tuple(slice(0, s) for s in a.shape)].set(a)
    zp = pad(z, (SEQ, LANES))
    w1p = pad(w1, (LANES, LANES))
    b1p = pad(b1.reshape(1, -1), (1, LANES))
    wip = pad(w_inner, (HYENA_INNER_MLPS, LANES, LANES))
    bip = pad(b_inner.reshape(HYENA_INNER_MLPS, 1, -1), (HYENA_INNER_MLPS, 1, LANES))
    frp = pad(freq.reshape(1, -1), (1, LANES))
    return pl.pallas_call(
        _filter_mlp_kernel,
        out_shape=jax.ShapeDtypeStruct((SEQ, LANES), F32),
        compiler_params=pltpu.CompilerParams(vmem_limit_bytes=VMEM_LIMIT),
        name="filter_mlp",
    )(zp, w1p, b1p, wip, bip, frp)


def _stage_b_rhs(a_ref, k1):
    ar = a_ref[pl.ds(k1, FFT_N2, stride=FFT_PITCH), :]
    ai = a_ref[pl.ds(FFT_N1 + k1, FFT_N2, stride=FFT_PITCH), :]
    return jnp.concatenate([ar, ai], axis=0).astype(BF16)


def _slab_rows(n2):
    return pl.ds(pl.multiple_of(n2 * FFT_PITCH, 8), 2 * FFT_N1)


def _filter_spectrum_kernel(h_ref, w_ref, dl_ref, fak_ref, t_ref, o_ref, kt_ref, a_ref):
    rc = FILTER_RC
    w = jnp.concatenate([w_ref[0, 0], w_ref[0, 1]], axis=1).astype(BF16)
    dl = jnp.concatenate([dl_ref[...], dl_ref[...]], axis=1)

    def gen(r, ss):
        r0 = pl.multiple_of(r * rc, rc)
        rows = lax.broadcasted_iota(jnp.int32, (rc, 2 * LANES), 0) + r0
        lanes = lax.broadcasted_iota(jnp.int32, (rc, 2 * LANES), 1)
        decay = jnp.exp(-(rows.astype(F32) * (1.0 / (SEQ - 1))) * dl)
        k = jnp.dot(h_ref[pl.ds(r0, rc), :].astype(BF16), w, preferred_element_type=F32)
        k = k * decay
        k = jnp.where((rows >= SEQ - 1) & (lanes >= LANES), 0.0, k)
        for n in range(rc // FFT_N2):
            dst = pl.ds(pl.multiple_of((r * (rc // FFT_N2) + n) * FILTER_PITCH, 8), FFT_N2)
            kt_ref[0, dst, :] = k[n * FFT_N2:(n + 1) * FFT_N2, :LANES]
            kt_ref[1, dst, :] = k[n * FFT_N2:(n + 1) * FFT_N2, LANES:]
        return ss + jnp.sum(k * k, axis=0, keepdims=True)

    ss = lax.fori_loop(0, SEQ // rc, gen, jnp.zeros((1, 2 * LANES), F32))
    scale = lax.rsqrt(ss[:, :LANES] + ss[:, LANES:] + EPS)

    def filter_slab(n2):
        fwd = kt_ref[0, pl.ds(n2, FFT_NZ1, stride=FILTER_PITCH), :]
        bwd = kt_ref[1, pl.ds(FFT_N2 - 1 - n2, FFT_NZ1, stride=FILTER_PITCH), :]
        return (jnp.concatenate([fwd, bwd], axis=0) * scale).astype(BF16)

    def stage_a(p, carry):
        n2 = 2 * p
        rhs = jnp.concatenate([filter_slab(n2), filter_slab(n2 + 1)], axis=1)
        a = jnp.dot(fak_ref[...], rhs, preferred_element_type=F32)
        a_ref[_slab_rows(n2), :] = a[:, :LANES]
        a_ref[_slab_rows(n2 + 1), :] = a[:, LANES:]
        return carry

    lax.fori_loop(0, FFT_N2 // 2, stage_a, 0, unroll=FFT_UNROLL // 2)

    def stage_b(k1, carry):
        b = jnp.dot(t_ref[k1], _stage_b_rhs(a_ref, k1), preferred_element_type=F32)
        o_ref[0, k1] = b.astype(o_ref.dtype)
        return carry

    lax.fori_loop(0, FFT_N1, stage_b, 0, unroll=FFT_UNROLL)


def _filter_spectrum(hmlp, w_filt, deltas, fak, t):
    wf = w_filt.reshape(HYENA_FILTER_ORDER, HYENA_ORDER, 2, HYENA_WIDTH).transpose(1, 2, 0, 3)
    wf = jnp.zeros((HYENA_ORDER, 2, LANES, HYENA_WIDTH), F32).at[:, :, :HYENA_FILTER_ORDER].set(wf)
    ncb = HYENA_WIDTH // LANES
    return pl.pallas_call(
        _filter_spectrum_kernel,
        grid=(HYENA_ORDER, ncb),
        in_specs=[_const_spec((SEQ, LANES)),
                  pl.BlockSpec((1, 2, LANES, LANES), lambda o, c: (o, 0, 0, c)),
                  pl.BlockSpec((1, LANES), lambda o, c: (0, c)),
                  _const_spec((2 * FFT_N1, FFT_N1)),
                  _const_spec((FFT_N1, 2 * FFT_N2, 2 * FFT_N2))],
        out_specs=pl.BlockSpec((1, FFT_N1, 2 * FFT_N2, LANES), lambda o, c: (o, 0, 0, c)),
        out_shape=jax.ShapeDtypeStruct((HYENA_ORDER, FFT_N1, 2 * FFT_N2, HYENA_WIDTH), F32),
        scratch_shapes=[pltpu.VMEM((2, FFT_NZ1 * FILTER_PITCH, LANES), F32),
                        pltpu.VMEM((FFT_N2 * FFT_PITCH, LANES), F32)],
        compiler_params=_params("parallel", "parallel"),
        name="filter_spectrum",
    )(hmlp, wf, deltas, fak, t)


def _hyena_conv_kernel(u_ref, xm_ref, kf_ref, sk_ref, fa_ref, t_ref, ti_ref, fd_ref, o_ref, a_ref, y_ref, *,
                       slab_order_out):
    def stage_a(p, carry):
        n2 = 2 * p
        rhs = jnp.concatenate(
            [jnp.concatenate([u_ref[0, n], u_ref[1, n]], axis=0) for n in (n2, n2 + 1)], axis=1).astype(BF16)
        a = jnp.dot(fa_ref[...], rhs, preferred_element_type=F32)
        a_ref[_slab_rows(n2), :] = a[:, :LANES]
        a_ref[_slab_rows(n2 + 1), :] = a[:, LANES:]
        return carry

    lax.fori_loop(0, FFT_N2 // 2, stage_a, 0, unroll=FFT_UNROLL // 2)

    def stage_b(k1, carry):
        b = jnp.dot(t_ref[k1], _stage_b_rhs(a_ref, k1), preferred_element_type=F32)
        kf = kf_ref[0, k1]
        br, bi = b[:FFT_N2], b[FFT_N2:]
        kr, ki = kf[:FFT_N2], kf[FFT_N2:]
        y_ref[k1] = jnp.concatenate([br * kr - bi * ki, br * ki + bi * kr], axis=0).astype(BF16)
        return carry

    lax.fori_loop(0, FFT_N1, stage_b, 0, unroll=FFT_UNROLL)

    def stage_c(k1, carry):
        c = jnp.dot(ti_ref[k1], y_ref[k1], preferred_element_type=F32)
        a_ref[pl.ds(k1, FFT_N2, stride=FFT_PITCH), :] = c[:FFT_N2]
        a_ref[pl.ds(FFT_N1 + k1, FFT_N2, stride=FFT_PITCH), :] = c[FFT_N2:]
        return carry

    lax.fori_loop(0, FFT_N1, stage_c, 0, unroll=FFT_UNROLL)

    sk = sk_ref[0]

    def stage_d(p, carry):
        n2 = 2 * p
        cb = jnp.concatenate([a_ref[_slab_rows(n2), :], a_ref[_slab_rows(n2 + 1), :]], axis=1).astype(BF16)
        y = jnp.dot(fd_ref[...], cb, preferred_element_type=F32)
        for s in range(2):
            for bi in range(2):
                yb = y[bi * FFT_NZ1:(bi + 1) * FFT_NZ1, s * LANES:(s + 1) * LANES]
                z = xm_ref[bi, n2 + s].astype(F32) * (yb + sk * u_ref[bi, n2 + s].astype(F32))
                if slab_order_out:
                    o_ref[bi, n2 + s] = z.astype(o_ref.dtype)
                else:
                    o_ref[bi, pl.ds(n2 + s, FFT_NZ1, stride=FFT_N2), :] = z
        return carry

    lax.fori_loop(0, FFT_N2 // 2, stage_d, 0, unroll=FFT_UNROLL // 2)


def _hyena_conv(u, u_col0, xm, xm_col0, kf, order, skip, consts, slab_order_out):
    fa, _, t, ti, fd = consts
    ncb = HYENA_WIDTH // LANES
    slab_block = (BATCH, FFT_N2, FFT_NZ1, LANES)
    if slab_order_out:
        out_spec = pl.BlockSpec(slab_block, lambda c: (0, 0, 0, c))
        out_shape = jax.ShapeDtypeStruct((BATCH, FFT_N2, FFT_NZ1, HYENA_WIDTH), BF16)
    else:
        out_spec = pl.BlockSpec((BATCH, SEQ, LANES), lambda c: (0, 0, c))
        out_shape = jax.ShapeDtypeStruct((BATCH, SEQ, HYENA_WIDTH), F32)
    return pl.pallas_call(
        functools.partial(_hyena_conv_kernel, slab_order_out=slab_order_out),
        grid=(ncb,),
        in_specs=[pl.BlockSpec(slab_block, lambda c: (0, 0, 0, u_col0 + c)),
                  pl.BlockSpec(slab_block, lambda c: (0, 0, 0, xm_col0 + c)),
                  pl.BlockSpec((1, FFT_N1, 2 * FFT_N2, LANES), lambda c: (order, 0, 0, c)),
                  pl.BlockSpec((1, 1, LANES), lambda c: (order, 0, c)),
                  _const_spec(fa.shape), _const_spec(t.shape), _const_spec(ti.shape), _const_spec(fd.shape)],
        out_specs=out_spec,
        out_shape=out_shape,
        scratch_shapes=[pltpu.VMEM((FFT_N2 * FFT_PITCH, LANES), F32),
                        pltpu.VMEM((FFT_N1, 2 * FFT_N2, LANES), BF16)],
        compiler_params=_params("parallel"),
        name="hyena_conv",
    )(u, xm, kf, skip.reshape(HYENA_ORDER, 1, HYENA_WIDTH), fa, t, ti, fd)


MA_TM = 256


def _mix_attn_kernel(x_ref, yc_ref, yh_ref, wa_ref, wb_ref, g_ref, wq_ref, k_ref, v_ref, wo_ref,
                     o_ref, x1_ref, q_ref, ctx_ref):
    mix = jnp.dot(yc_ref[0], wa_ref[...], preferred_element_type=F32)
    mix = mix + jnp.dot(yh_ref[0].astype(BF16), wb_ref[...], preferred_element_type=F32)
    x1 = x_ref[0] + mix
    x1_ref[...] = x1
    q_ref[...] = jnp.dot(_rms(x1, g_ref[...]).astype(BF16), wq_ref[...],
                         preferred_element_type=F32).astype(BF16)
    hd = XATTN_HEAD_DIM
    for h in range(XATTN_HEADS):
        hs = slice(h * hd, (h + 1) * hd)
        s = lax.dot_general(q_ref[:, hs], k_ref[0, :, hs], (((1,), (1,)), ((), ())),
                            preferred_element_type=F32) * (hd ** -0.5)
        s = s - jnp.max(s, axis=-1, keepdims=True)
        e = jnp.exp(s)
        pr = e / jnp.sum(e, axis=-1, keepdims=True)
        ctx_ref[:, hs] = jnp.dot(pr.astype(BF16), v_ref[0, :, hs], preferred_element_type=F32).astype(BF16)
    o_ref[0] = x1_ref[...] + jnp.dot(ctx_ref[...], wo_ref[...], preferred_element_type=F32)


def _mix_attn(x, yc, yh, w_out, g, wq, k, v, wo):
    b, l, d = x.shape
    tm = MA_TM
    kc = yc.shape[-1]
    tok = lambda width: pl.BlockSpec((1, tm, width), lambda bi, i: (bi, i, 0))
    mem = pl.BlockSpec((1, N_MEM, d), lambda bi, i: (bi, 0, 0))
    return pl.pallas_call(
        _mix_attn_kernel,
        grid=(b, l // tm),
        in_specs=[tok(d), tok(kc), tok(kc),
                  _const_spec((kc, d), (0, 0)), _const_spec((kc, d), (1, 0)),
                  pl.BlockSpec((1, d), lambda bi, i: (0, 0)),
                  _const_spec((d, d)), mem, mem, _const_spec((d, d))],
        out_specs=tok(d),
        out_shape=jax.ShapeDtypeStruct((b, l, d), F32),
        scratch_shapes=[pltpu.VMEM((tm, d), F32), pltpu.VMEM((tm, d), BF16), pltpu.VMEM((tm, d), BF16)],
        compiler_params=_params("parallel", "parallel"),
        name="mix_attn",
    )(x, yc, yh, w_out, w_out, g.reshape(1, d), wq, k, v, wo)


FFN_TM = 1024
FFN_FC = 512
FFN_VMEM_LIMIT = 62 * 1024 * 1024
FFN_HALO = 16


def _ffn_kernel(x_ref, xp_ref, xn_ref, g_ref, wg_ref, wv_ref, cw_ref, cb_ref, wd_ref, gf_ref,
                o_ref, h_ref, ge_ref):
    i = pl.program_id(1)
    j = pl.program_id(2)
    nt = pl.num_programs(1)
    nf = pl.num_programs(2)
    tm, hl = FFN_TM, FFN_HALO

    @pl.when(j == 0)
    def _():
        g = g_ref[...]
        h_ref[0:hl, :] = (_rms(xp_ref[0], g) * (i > 0).astype(F32)).astype(BF16)
        h_ref[hl:hl + tm, :] = _rms(x_ref[0], g).astype(BF16)
        h_ref[hl + tm:hl + tm + hl, :] = (_rms(xn_ref[0], g) * (i < nt - 1).astype(F32)).astype(BF16)
        o_ref[...] = jnp.zeros_like(o_ref)

    ge_ref[...] = jnp.dot(h_ref[...], wg_ref[...], preferred_element_type=F32)
    val = jnp.dot(h_ref[pl.ds(hl, tm), :], wv_ref[...], preferred_element_type=F32)
    gate = (cw_ref[0:1, :] * ge_ref[pl.ds(hl - 1, tm), :]
            + cw_ref[1:2, :] * ge_ref[pl.ds(hl, tm), :]
            + cw_ref[2:3, :] * ge_ref[pl.ds(hl + 1, tm), :]
            + cb_ref[...])
    act = (gate * _sigmoid(gate) * val).astype(BF16)
    o_ref[0] += jnp.dot(act, wd_ref[...], preferred_element_type=F32)

    @pl.when(j == nf - 1)
    def _():
        o_ref[0] = _rms(x_ref[0] + o_ref[0], gf_ref[...])


def _ffn(x, g_ffn, w_up, dw_w, dw_b, w_down, g_final):
    b, l, d = x.shape
    tm, fc, hl = FFN_TM, FFN_FC, FFN_HALO
    nf = D_FF // fc
    hb = tm // hl
    last_hb = l // hl - 1
    return pl.pallas_call(
        _ffn_kernel,
        grid=(b, l // tm, nf),
        in_specs=[pl.BlockSpec((1, tm, d), lambda bi, i, j: (bi, i, 0)),
                  pl.BlockSpec((1, hl, d), lambda bi, i, j: (bi, jnp.maximum(i * hb - 1, 0), 0)),
                  pl.BlockSpec((1, hl, d), lambda bi, i, j: (bi, jnp.minimum((i + 1) * hb, last_hb), 0)),
                  pl.BlockSpec((1, d), lambda bi, i, j: (0, 0)),
                  pl.BlockSpec((d, fc), lambda bi, i, j: (0, j)),
                  pl.BlockSpec((d, fc), lambda bi, i, j: (0, nf + j)),
                  pl.BlockSpec((3, fc), lambda bi, i, j: (0, j)),
                  pl.BlockSpec((1, fc), lambda bi, i, j: (0, j)),
                  pl.BlockSpec((fc, d), lambda bi, i, j: (j, 0)),
                  pl.BlockSpec((1, d), lambda bi, i, j: (0, 0))],
        out_specs=pl.BlockSpec((1, tm, d), lambda bi, i, j: (bi, i, 0)),
        out_shape=jax.ShapeDtypeStruct((b, l, d), F32),
        scratch_shapes=[pltpu.VMEM((tm + 2 * hl, d), BF16),
                        pltpu.VMEM((tm + 2 * hl, fc), F32)],
        compiler_params=_params("parallel", "parallel", "arbitrary", vmem_limit=FFN_VMEM_LIMIT),
        name="ffn",
    )(x, x, x, g_ffn.reshape(1, d), w_up, w_up, dw_w, dw_b.reshape(1, -1), w_down, g_final.reshape(1, d))


def _position_features(length):
    t = jnp.linspace(0.0, 1.0, length, dtype=F32)[:, None]
    w = 2.0 * math.pi * jnp.arange(length, dtype=F32)[:, None] / length
    bands = (HYENA_EMB - 1) // 2
    f = jnp.linspace(1e-4, bands - 1, bands, dtype=F32)[None, :]
    return jnp.concatenate([t, jnp.cos(f * w), -jnp.sin(f * w)], axis=-1)


def _decay_rates():
    max_decay = math.log(HYENA_TARGET) / HYENA_FAST_DECAY
    min_decay = math.log(HYENA_TARGET) / HYENA_SLOW_DECAY
    return jnp.abs(jnp.linspace(min_decay, max_decay, HYENA_WIDTH, dtype=F32)).reshape(1, HYENA_WIDTH)


def kernel(x, mem, g_mix, w_in, conf_dw_w, conf_dw_b, conf_ln_g, conf_ln_b, hyena_short_w, hyena_short_b, hyena_w1, hyena_b1, hyena_w_inner, hyena_b_inner, hyena_w_filt, hyena_sin_freq, hyena_skip, w_out, g_xattn, g_mem, w_q, w_k, w_v, w_o, g_ffn, w_ffn_up, ffn_dw_w, ffn_dw_b, w_ffn_down, g_final):
    b, l, d = x.shape
    consts = _dft_constants()
    w_in_b = w_in[0].astype(BF16)

    u_conf = _conf_in(x.reshape(b * l, d), g_mix[0], w_in_b).reshape(b, l, CONF_WIDTH)
    y_conf = _conformer(u_conf, conf_dw_w[0], conf_dw_b[0], conf_ln_g[0], conf_ln_b[0])

    uc = _hyena_in(x, g_mix[0], w_in_b, hyena_short_w[0], hyena_short_b[0])
    hmlp = _filter_mlp(_position_features(l), hyena_w1[0], hyena_b1[0], hyena_w_inner[0],
                       hyena_b_inner[0], hyena_sin_freq[0])
    kf = _filter_spectrum(hmlp, hyena_w_filt[0], _decay_rates(), consts[1], consts[2])
    ncb = HYENA_WIDTH // LANES
    z1 = _hyena_conv(uc, 0, uc, ncb, kf, 0, hyena_skip[0], consts, slab_order_out=True)
    z2 = _hyena_conv(z1, 0, uc, 2 * ncb, kf, 1, hyena_skip[0], consts, slab_order_out=False)

    memf = mem.reshape(b * N_MEM, d)
    k = _norm_matmul(memf, g_mem[0], w_k[0], b * N_MEM, 1024, BF16).reshape(b, N_MEM, d)
    v = _norm_matmul(memf, g_mem[0], w_v[0], b * N_MEM, 1024, BF16).reshape(b, N_MEM, d)
    x2 = _mix_attn(x, y_conf, z2, w_out[0].astype(BF16), g_xattn[0], w_q[0].astype(BF16), k, v,
                   w_o[0].astype(BF16))

    return _ffn(x2, g_ffn[0], w_ffn_up[0].astype(BF16), ffn_dw_w[0], ffn_dw_b[0],
                w_ffn_down[0].astype(BF16), g_final)
```

```python
import functools
import math

import numpy as np
import jax
import jax.numpy as jnp
from jax import lax
from jax.experimental import pallas as pl
from jax.experimental.pallas import tpu as pltpu

F32 = jnp.float32
BF16 = jnp.bfloat16

D_MODEL = 2048
BATCH = 2
SEQ = 4096
CONF_WIDTH = 1024
HYENA_WIDTH = 1024
CONF_KERNEL = 31
SHORT_KERNEL = 3
HYENA_ORDER = 2
HYENA_EMB = 33
HYENA_FILTER_ORDER = 64
HYENA_INNER_MLPS = 2
HYENA_FAST_DECAY = 0.3
HYENA_SLOW_DECAY = 1.5
HYENA_TARGET = 1e-2
N_MEM = 256
XATTN_HEADS = 4
XATTN_HEAD_DIM = D_MODEL // XATTN_HEADS
D_FF = 5632
EPS = 1e-6
IN_COLS = 2 * CONF_WIDTH + (HYENA_ORDER + 1) * HYENA_WIDTH

LANES = 128
VMEM_LIMIT = 56 * 1024 * 1024

N_FFT = 2 * SEQ
FFT_N1 = 128
FFT_N2 = 64
FFT_NZ1 = SEQ // FFT_N2
FFT_PITCH = 2 * FFT_N1 + 8
FFT_UNROLL = 16
FILTER_RC = 512
FILTER_PITCH = FFT_N2 + 8


def _params(*sem, vmem_limit=VMEM_LIMIT):
    return pltpu.CompilerParams(dimension_semantics=sem, vmem_limit_bytes=vmem_limit)


def _const_spec(shape, index=None):
    index = (0,) * len(shape) if index is None else index
    return pl.BlockSpec(shape, lambda *_: index, pipeline_mode=pl.Buffered(1))


def _sigmoid(x):
    return 0.5 * (jnp.tanh(0.5 * x) + 1.0)


def _cast_specs(weights, grid_shape):
    nsteps = math.prod(grid_shape)

    def step(*g):
        s = g[0]
        for gi, n in zip(g[1:], grid_shape[1:]):
            s = s * n + gi
        return (s, 0)

    specs, shapes = [], []
    for w in weights:
        rows, cols = w.shape
        assert rows % (16 * nsteps) == 0, (w.shape, nsteps)
        specs.append(pl.BlockSpec((rows // nsteps, cols), step))
        shapes.append(jax.ShapeDtypeStruct(w.shape, BF16))
    return specs, shapes


def _cast_blocks(in_refs, out_refs):
    for i_ref, o_ref in zip(in_refs, out_refs):
        o_ref[...] = i_ref[...].astype(BF16)


def _rms(x, g):
    ms = jnp.mean(x * x, axis=-1, keepdims=True)
    return x * lax.rsqrt(ms + EPS) * g


@functools.lru_cache(maxsize=None)
def _dft_constants():
    k1 = np.arange(FFT_N1)
    f1 = np.exp(-2j * np.pi * np.outer(k1, k1) / FFT_N1)
    fh = f1[:, :FFT_NZ1]
    fa = np.block([[fh.real, -fh.imag], [fh.imag, fh.real]])
    perm = np.concatenate([np.arange(FFT_NZ1), FFT_N1 - 1 - np.arange(FFT_N1 - FFT_NZ1)])
    fp = f1[:, perm]
    fak = np.concatenate([fp.real, fp.imag], axis=0)
    n2 = np.arange(FFT_N2)
    kk = k1[:, None, None] + FFT_N1 * n2[None, :, None]
    ang = (n2[None, None, :] * kk) % N_FFT
    tc = np.exp(-2j * np.pi * ang / N_FFT)
    t = np.concatenate([np.concatenate([tc.real, -tc.imag], 2),
                        np.concatenate([tc.imag, tc.real], 2)], 1)
    ti = np.transpose(t, (0, 2, 1))
    g = np.conj(f1[:, :FFT_NZ1]).T / N_FFT
    fd = np.block([[g.real, -g.imag], [g.imag, g.real]])
    return tuple(jnp.asarray(a, dtype=BF16) for a in (fa, fak, t, ti, fd))


def _norm_matmul_kernel(x_ref, g_ref, w_ref, o_ref, h_ref):
    @pl.when(pl.program_id(1) == 0)
    def _():
        h_ref[...] = _rms(x_ref[...], g_ref[...]).astype(BF16)

    o_ref[...] = jnp.dot(h_ref[...], w_ref[...].astype(BF16), preferred_element_type=F32).astype(o_ref.dtype)


def _norm_matmul(x, g, w, tm, tn, out_dtype):
    m, d = x.shape
    n = w.shape[1]
    return pl.pallas_call(
        _norm_matmul_kernel,
        grid=(m // tm, n // tn),
        in_specs=[pl.BlockSpec((tm, d), lambda i, j: (i, 0)),
                  pl.BlockSpec((1, d), lambda i, j: (0, 0)),
                  pl.BlockSpec((d, tn), lambda i, j: (0, j))],
        out_specs=pl.BlockSpec((tm, tn), lambda i, j: (i, j)),
        out_shape=jax.ShapeDtypeStruct((m, n), out_dtype),
        scratch_shapes=[pltpu.VMEM((tm, d), BF16)],
        compiler_params=_params("parallel", "arbitrary"),
        name="norm_matmul",
    )(x, g.reshape(1, d), w)


CIN_TM = 1024
CIN_TN = 512


def _conf_in_kernel(x_ref, g_ref, wa_ref, wg_ref, o_ref, h_ref):
    @pl.when(pl.program_id(1) == 0)
    def _():
        h_ref[...] = _rms(x_ref[...], g_ref[...]).astype(BF16)

    h = h_ref[...]
    a = jnp.dot(h, wa_ref[...], preferred_element_type=F32)
    g = jnp.dot(h, wg_ref[...], preferred_element_type=F32)
    o_ref[...] = (a * _sigmoid(g)).astype(o_ref.dtype)


def _conf_in(x, g, w_in):
    m, d = x.shape
    tm, tn = CIN_TM, CIN_TN
    ncol = CONF_WIDTH // tn
    return pl.pallas_call(
        _conf_in_kernel,
        grid=(m // tm, ncol),
        in_specs=[pl.BlockSpec((tm, d), lambda i, j: (i, 0)),
                  pl.BlockSpec((1, d), lambda i, j: (0, 0)),
                  pl.BlockSpec((d, tn), lambda i, j: (0, j)),
                  pl.BlockSpec((d, tn), lambda i, j: (0, ncol + j))],
        out_specs=pl.BlockSpec((tm, tn), lambda i, j: (i, j)),
        out_shape=jax.ShapeDtypeStruct((m, CONF_WIDTH), BF16),
        scratch_shapes=[pltpu.VMEM((tm, d), BF16)],
        compiler_params=_params("parallel", "arbitrary"),
        name="conf_in",
    )(x, g.reshape(1, d), w_in, w_in)


CONF_T = 512
CONF_HALO = 16
CONF_RC = 64


def _conformer_kernel(u_in, up_in, un_in, w_ref, b_ref, lg_ref, lb_ref, o_ref, u_ref, y_ref):
    i = pl.program_id(1)
    nt = pl.num_programs(1)
    t = CONF_T
    h = CONF_HALO
    prev_ok = (i > 0).astype(F32)
    next_ok = (i < nt - 1).astype(F32)
    nc = CONF_WIDTH // LANES
    for c in range(nc):
        cs = slice(c * LANES, (c + 1) * LANES)
        u_ref[c, 0:h, :] = up_in[0, :, cs].astype(F32) * prev_ok
        u_ref[c, h:h + t, :] = u_in[0, :, cs].astype(F32)
        u_ref[c, h + t:h + t + h, :] = un_in[0, :, cs].astype(F32) * next_ok

    pad = CONF_KERNEL // 2

    def conv_chunk(c, carry):
        w = w_ref[c]
        bias = b_ref[c]
        for r in range(t // CONF_RC):
            r0 = r * CONF_RC
            acc = jnp.zeros((CONF_RC, LANES), F32) + bias
            for j in range(CONF_KERNEL):
                acc = acc + w[j:j + 1, :] * u_ref[c, r0 + h - pad + j:r0 + h - pad + j + CONF_RC, :]
            y_ref[c, r0:r0 + CONF_RC, :] = acc
        return carry

    lax.fori_loop(0, nc, conv_chunk, 0)

    def ln_chunk(r, carry):
        r0 = pl.multiple_of(r * CONF_RC, CONF_RC)
        ys = [y_ref[c, pl.ds(r0, CONF_RC), :] for c in range(nc)]
        tot = ys[0]
        for y in ys[1:]:
            tot = tot + y
        mu = jnp.sum(tot, axis=-1, keepdims=True) * (1.0 / CONF_WIDTH)
        yc = [y - mu for y in ys]
        sq = yc[0] * yc[0]
        for y in yc[1:]:
            sq = sq + y * y
        inv = lax.rsqrt(jnp.sum(sq, axis=-1, keepdims=True) * (1.0 / CONF_WIDTH) + EPS)
        for c in range(nc):
            cs = slice(c * LANES, (c + 1) * LANES)
            yn = yc[c] * inv * lg_ref[:, cs] + lb_ref[:, cs]
            o_ref[0, pl.ds(r0, CONF_RC), cs] = (yn * _sigmoid(yn)).astype(o_ref.dtype)
        return carry

    lax.fori_loop(0, t // CONF_RC, ln_chunk, 0, unroll=2)


def _conformer(u, dw_w, dw_b, ln_g, ln_b):
    b, l, c = u.shape
    t, h = CONF_T, CONF_HALO
    nt = l // t
    hb = t // h
    last_hb = l // h - 1
    vec = pl.BlockSpec((1, c), lambda bi, i: (0, 0))
    nc = c // LANES
    chunked = lambda rows: pl.BlockSpec((nc, rows, LANES), lambda bi, i: (0, 0, 0))
    w3 = dw_w.reshape(CONF_KERNEL, nc, LANES).transpose(1, 0, 2)
    return pl.pallas_call(
        _conformer_kernel,
        grid=(b, nt),
        in_specs=[pl.BlockSpec((1, t, c), lambda bi, i: (bi, i, 0)),
                  pl.BlockSpec((1, h, c), lambda bi, i: (bi, jnp.maximum(i * hb - 1, 0), 0)),
                  pl.BlockSpec((1, h, c), lambda bi, i: (bi, jnp.minimum((i + 1) * hb, last_hb), 0)),
                  chunked(CONF_KERNEL), chunked(1), vec, vec],
        out_specs=pl.BlockSpec((1, t, c), lambda bi, i: (bi, i, 0)),
        out_shape=jax.ShapeDtypeStruct((b, l, c), BF16),
        scratch_shapes=[pltpu.VMEM((nc, t + 2 * h, LANES), F32), pltpu.VMEM((nc, t, LANES), F32)],
        compiler_params=_params("parallel", "parallel"),
        name="conformer",
    )(u, u, u, w3, dw_b.reshape(nc, 1, LANES), ln_g.reshape(1, c), ln_b.reshape(1, c))


HIN_TM = 1024
HIN_TN = 1024
HIN_HALO = 16
HIN_N1 = HIN_TM // FFT_N2
HIN_PITCH = FFT_N2 + 8


def _hyena_in_kernel(x_ref, xp_ref, xn_ref, g_ref, w_ref, cw_ref, cb_ref, o_ref, h_ref, e_ref, s_ref):
    i = pl.program_id(1)
    j = pl.program_id(2)
    nt = pl.num_programs(1)
    tm, hl = HIN_TM, HIN_HALO

    @pl.when(j == 0)
    def _():
        g = g_ref[...]
        h_ref[0:hl, :] = (_rms(xp_ref[0], g) * (i > 0).astype(F32)).astype(BF16)
        h_ref[hl:hl + tm, :] = _rms(x_ref[0], g).astype(BF16)
        h_ref[hl + tm:hl + tm + hl, :] = (_rms(xn_ref[0], g) * (i < nt - 1).astype(F32)).astype(BF16)

    e_ref[...] = jnp.dot(h_ref[...], w_ref[...], preferred_element_type=F32)
    nlc = HIN_TN // LANES
    for r in range(HIN_N1):
        r0 = hl + r * FFT_N2
        for lc in range(nlc):
            cs = slice(lc * LANES, (lc + 1) * LANES)
            acc = (cw_ref[0:1, cs] * e_ref[r0 - 1:r0 - 1 + FFT_N2, cs]
                   + cw_ref[1:2, cs] * e_ref[r0:r0 + FFT_N2, cs]
                   + cw_ref[2:3, cs] * e_ref[r0 + 1:r0 + 1 + FFT_N2, cs]
                   + cb_ref[:, cs])
            s_ref[lc, r * HIN_PITCH:r * HIN_PITCH + FFT_N2, :] = acc

    def gather(n2, carry):
        for lc in range(nlc):
            rows = s_ref[lc, pl.ds(n2, HIN_N1, stride=HIN_PITCH), :]
            o_ref[0, n2, :, lc * LANES:(lc + 1) * LANES] = rows.astype(o_ref.dtype)
        return carry

    lax.fori_loop(0, FFT_N2, gather, 0, unroll=8)


def _hyena_in(x, g, w_in, cw, cb):
    b, l, d = x.shape
    tm, tn, hl = HIN_TM, HIN_TN, HIN_HALO
    ncol = (HYENA_ORDER + 1) * HYENA_WIDTH // tn
    col0 = 2 * CONF_WIDTH // tn
    hb = tm // hl
    last_hb = l // hl - 1
    return pl.pallas_call(
        _hyena_in_kernel,
        grid=(b, l // tm, ncol),
        in_specs=[pl.BlockSpec((1, tm, d), lambda bi, i, j: (bi, i, 0)),
                  pl.BlockSpec((1, hl, d), lambda bi, i, j: (bi, jnp.maximum(i * hb - 1, 0), 0)),
                  pl.BlockSpec((1, hl, d), lambda bi, i, j: (bi, jnp.minimum((i + 1) * hb, last_hb), 0)),
                  pl.BlockSpec((1, d), lambda bi, i, j: (0, 0)),
                  pl.BlockSpec((d, tn), lambda bi, i, j: (0, col0 + j)),
                  pl.BlockSpec((SHORT_KERNEL, tn), lambda bi, i, j: (0, j)),
                  pl.BlockSpec((1, tn), lambda bi, i, j: (0, j))],
        out_specs=pl.BlockSpec((1, FFT_N2, HIN_N1, tn), lambda bi, i, j: (bi, 0, i, j)),
        out_shape=jax.ShapeDtypeStruct((b, FFT_N2, FFT_NZ1, ncol * tn), BF16),
        scratch_shapes=[pltpu.VMEM((tm + 2 * hl, d), BF16),
                        pltpu.VMEM((tm + 2 * hl, tn), F32),
                        pltpu.VMEM((tn // LANES, HIN_N1 * HIN_PITCH, LANES), F32)],
        compiler_params=_params("parallel", "parallel", "arbitrary"),
        name="hyena_in",
    )(x, x, x, g.reshape(1, d), w_in, cw, cb.reshape(1, -1))


def _filter_mlp_kernel(z_ref, w1_ref, b1_ref, wi_ref, bi_ref, fr_ref, o_ref):
    hi = lax.Precision.HIGHEST
    fr = fr_ref[...]
    h = jnp.sin(fr * (jnp.dot(z_ref[...], w1_ref[...], precision=hi, preferred_element_type=F32) + b1_ref[...]))
    for i in range(HYENA_INNER_MLPS):
        h = jnp.sin(fr * (jnp.dot(h, wi_ref[i], precision=hi, preferred_element_type=F32) + bi_ref[i]))
    o_ref[...] = h


def _filter_mlp(z, w1, b1, w_inner, b_inner, freq):
    pad = lambda a, shape: jnp.zeros(shape, F32).at[tuple(slice(0, s) for s in a.shape)].set(a)
    zp = pad(z, (SEQ, LANES))
    w1p = pad(w1, (LANES, LANES))
    b1p = pad(b1.reshape(1, -1), (1, LANES))
    wip = pad(w_inner, (HYENA_INNER_MLPS, LANES, LANES))
    bip = pad(b_inner.reshape(HYENA_INNER_MLPS, 1, -1), (HYENA_INNER_MLPS, 1, LANES))
    frp = pad(freq.reshape(1, -1), (1, LANES))
    return pl.pallas_call(
        _filter_mlp_kernel,
        out_shape=jax.ShapeDtypeStruct((SEQ, LANES), F32),
        compiler_params=pltpu.CompilerParams(vmem_limit_bytes=VMEM_LIMIT),
        name="filter_mlp",
    )(zp, w1p, b1p, wip, bip, frp)


def _stage_b_rhs(a_ref, k1):
    ar = a_ref[pl.ds(k1, FFT_N2, stride=FFT_PITCH), :]
    ai = a_ref[pl.ds(FFT_N1 + k1, FFT_N2, stride=FFT_PITCH), :]
    return jnp.concatenate([ar, ai], axis=0).astype(BF16)


def _slab_rows(n2):
    return pl.ds(pl.multiple_of(n2 * FFT_PITCH, 8), 2 * FFT_N1)


def _filter_spectrum_kernel(h_ref, w_ref, dl_ref, fak_ref, t_ref, *refs, n_cast):
    cast_in, o_ref, cast_out = refs[:n_cast], refs[n_cast], refs[n_cast + 1:2 * n_cast + 1]
    kt_ref, a_ref = refs[2 * n_cast + 1:]
    _cast_blocks(cast_in, cast_out)
    rc = FILTER_RC
    w = jnp.concatenate([w_ref[0, 0], w_ref[0, 1]], axis=1).astype(BF16)
    dl = jnp.concatenate([dl_ref[...], dl_ref[...]], axis=1)

    def gen(r, ss):
        r0 = pl.multiple_of(r * rc, rc)
        rows = lax.broadcasted_iota(jnp.int32, (rc, 2 * LANES), 0) + r0
        lanes = lax.broadcasted_iota(jnp.int32, (rc, 2 * LANES), 1)
        decay = jnp.exp(-(rows.astype(F32) * (1.0 / (SEQ - 1))) * dl)
        k = jnp.dot(h_ref[pl.ds(r0, rc), :].astype(BF16), w, preferred_element_type=F32)
        k = k * decay
        k = jnp.where((rows >= SEQ - 1) & (lanes >= LANES), 0.0, k)
        for n in range(rc // FFT_N2):
            dst = pl.ds(pl.multiple_of((r * (rc // FFT_N2) + n) * FILTER_PITCH, 8), FFT_N2)
            kt_ref[0, dst, :] = k[n * FFT_N2:(n + 1) * FFT_N2, :LANES]
            kt_ref[1, dst, :] = k[n * FFT_N2:(n + 1) * FFT_N2, LANES:]
        return ss + jnp.sum(k * k, axis=0, keepdims=True)

    ss = lax.fori_loop(0, SEQ // rc, gen, jnp.zeros((1, 2 * LANES), F32))
    scale = lax.rsqrt(ss[:, :LANES] + ss[:, LANES:] + EPS)

    def filter_slab(n2):
        fwd = kt_ref[0, pl.ds(n2, FFT_NZ1, stride=FILTER_PITCH), :]
        bwd = kt_ref[1, pl.ds(FFT_N2 - 1 - n2, FFT_NZ1, stride=FILTER_PITCH), :]
        return (jnp.concatenate([fwd, bwd], axis=0) * scale).astype(BF16)

    def stage_a(p, carry):
        n2 = 2 * p
        rhs = jnp.concatenate([filter_slab(n2), filter_slab(n2 + 1)], axis=1)
        a = jnp.dot(fak_ref[...], rhs, preferred_element_type=F32)
        a_ref[_slab_rows(n2), :] = a[:, :LANES]
        a_ref[_slab_rows(n2 + 1), :] = a[:, LANES:]
        return carry

    lax.fori_loop(0, FFT_N2 // 2, stage_a, 0, unroll=FFT_UNROLL // 2)

    def stage_b(k1, carry):
        b = jnp.dot(t_ref[k1], _stage_b_rhs(a_ref, k1), preferred_element_type=F32)
        o_ref[0, k1] = b.astype(o_ref.dtype)
        return carry

    lax.fori_loop(0, FFT_N1, stage_b, 0, unroll=FFT_UNROLL)


def _filter_spectrum(hmlp, w_filt, deltas, fak, t, cast_weights):
    wf = w_filt.reshape(HYENA_FILTER_ORDER, HYENA_ORDER, 2, HYENA_WIDTH).transpose(1, 2, 0, 3)
    wf = jnp.zeros((HYENA_ORDER, 2, LANES, HYENA_WIDTH), F32).at[:, :, :HYENA_FILTER_ORDER].set(wf)
    ncb = HYENA_WIDTH // LANES
    grid = (HYENA_ORDER, ncb)
    cast_specs, cast_shapes = _cast_specs(cast_weights, grid)
    outs = pl.pallas_call(
        functools.partial(_filter_spectrum_kernel, n_cast=len(cast_weights)),
        grid=grid,
        in_specs=[_const_spec((SEQ, LANES)),
                  pl.BlockSpec((1, 2, LANES, LANES), lambda o, c: (o, 0, 0, c)),
                  pl.BlockSpec((1, LANES), lambda o, c: (0, c)),
                  _const_spec((2 * FFT_N1, FFT_N1)),
                  _const_spec((FFT_N1, 2 * FFT_N2, 2 * FFT_N2))] + cast_specs,
        out_specs=[pl.BlockSpec((1, FFT_N1, 2 * FFT_N2, LANES), lambda o, c: (o, 0, 0, c))] + cast_specs,
        out_shape=[jax.ShapeDtypeStruct((HYENA_ORDER, FFT_N1, 2 * FFT_N2, HYENA_WIDTH), F32)] + cast_shapes,
        scratch_shapes=[pltpu.VMEM((2, FFT_NZ1 * FILTER_PITCH, LANES), F32),
                        pltpu.VMEM((FFT_N2 * FFT_PITCH, LANES), F32)],
        compiler_params=_params("arbitrary", "arbitrary"),
        name="filter_spectrum",
    )(hmlp, wf, deltas, fak, t, *cast_weights)
    return outs[0], outs[1:]


def _hyena_conv_kernel(u_ref, xm_ref, kf_ref, sk_ref, fa_ref, t_ref, ti_ref, fd_ref, o_ref, a_ref, y_ref, *,
                       slab_order_out):
    def stage_a(p, carry):
        n2 = 2 * p
        rhs = jnp.concatenate(
            [jnp.concatenate([u_ref[0, n], u_ref[1, n]], axis=0) for n in (n2, n2 + 1)], axis=1).astype(BF16)
        a = jnp.dot(fa_ref[...], rhs, preferred_element_type=F32)
        a_ref[_slab_rows(n2), :] = a[:, :LANES]
        a_ref[_slab_rows(n2 + 1), :] = a[:, LANES:]
        return carry

    lax.fori_loop(0, FFT_N2 // 2, stage_a, 0, unroll=FFT_UNROLL // 2)

    def stage_b(k1, carry):
        b = jnp.dot(t_ref[k1], _stage_b_rhs(a_ref, k1), preferred_element_type=F32)
        kf = kf_ref[0, k1]
        br, bi = b[:FFT_N2], b[FFT_N2:]
        kr, ki = kf[:FFT_N2], kf[FFT_N2:]
        y_ref[k1] = jnp.concatenate([br * kr - bi * ki, br * ki + bi * kr], axis=0).astype(BF16)
        return carry

    lax.fori_loop(0, FFT_N1, stage_b, 0, unroll=FFT_UNROLL)

    def stage_c(k1, carry):
        c = jnp.dot(ti_ref[k1], y_ref[k1], preferred_element_type=F32)
        a_ref[pl.ds(k1, FFT_N2, stride=FFT_PITCH), :] = c[:FFT_N2]
        a_ref[pl.ds(FFT_N1 + k1, FFT_N2, stride=FFT_PITCH), :] = c[FFT_N2:]
        return carry

    lax.fori_loop(0, FFT_N1, stage_c, 0, unroll=FFT_UNROLL)

    sk = sk_ref[0]

    def stage_d(p, carry):
        n2 = 2 * p
        cb = jnp.concatenate([a_ref[_slab_rows(n2), :], a_ref[_slab_rows(n2 + 1), :]], axis=1).astype(BF16)
        y = jnp.dot(fd_ref[...], cb, preferred_element_type=F32)
        for s in range(2):
            for bi in range(2):
                yb = y[bi * FFT_NZ1:(bi + 1) * FFT_NZ1, s * LANES:(s + 1) * LANES]
                z = xm_ref[bi, n2 + s].astype(F32) * (yb + sk * u_ref[bi, n2 + s].astype(F32))
                if slab_order_out:
                    o_ref[bi, n2 + s] = z.astype(o_ref.dtype)
                else:
                    o_ref[bi, pl.ds(n2 + s, FFT_NZ1, stride=FFT_N2), :] = z
        return carry

    lax.fori_loop(0, FFT_N2 // 2, stage_d, 0, unroll=FFT_UNROLL // 2)


def _hyena_conv(u, u_col0, xm, xm_col0, kf, order, skip, consts, slab_order_out):
    fa, _, t, ti, fd = consts
    ncb = HYENA_WIDTH // LANES
    slab_block = (BATCH, FFT_N2, FFT_NZ1, LANES)
    if slab_order_out:
        out_spec = pl.BlockSpec(slab_block, lambda c: (0, 0, 0, c))
        out_shape = jax.ShapeDtypeStruct((BATCH, FFT_N2, FFT_NZ1, HYENA_WIDTH), BF16)
    else:
        out_spec = pl.BlockSpec((BATCH, SEQ, LANES), lambda c: (0, 0, c))
        out_shape = jax.ShapeDtypeStruct((BATCH, SEQ, HYENA_WIDTH), F32)
    return pl.pallas_call(
        functools.partial(_hyena_conv_kernel, slab_order_out=slab_order_out),
        grid=(ncb,),
        in_specs=[pl.BlockSpec(slab_block, lambda c: (0, 0, 0, u_col0 + c)),
                  pl.BlockSpec(slab_block, lambda c: (0, 0, 0, xm_col0 + c)),
                  pl.BlockSpec((1, FFT_N1, 2 * FFT_N2, LANES), lambda c: (order, 0, 0, c)),
                  pl.BlockSpec((1, 1, LANES), lambda c: (order, 0, c)),
                  _const_spec(fa.shape), _const_spec(t.shape), _const_spec(ti.shape), _const_spec(fd.shape)],
        out_specs=out_spec,
        out_shape=out_shape,
        scratch_shapes=[pltpu.VMEM((FFT_N2 * FFT_PITCH, LANES), F32),
                        pltpu.VMEM((FFT_N1, 2 * FFT_N2, LANES), BF16)],
        compiler_params=_params("parallel"),
        name="hyena_conv",
    )(u, xm, kf, skip.reshape(HYENA_ORDER, 1, HYENA_WIDTH), fa, t, ti, fd)


MA_TM = 256
MA_VMEM_LIMIT = 60 * 1024 * 1024


def _mix_attn_kernel(x_ref, yc_ref, yh_ref, wa_ref, wb_ref, g_ref, wq_ref, k_ref, v_ref, wo_ref, *refs, n_cast):
    cast_in, o_ref, cast_out = refs[:n_cast], refs[n_cast], refs[n_cast + 1:2 * n_cast + 1]
    x1_ref, q_ref, ctx_ref = refs[2 * n_cast + 1:]
    _cast_blocks(cast_in, cast_out)
    mix = jnp.dot(yc_ref[0], wa_ref[...], preferred_element_type=F32)
    mix = mix + jnp.dot(yh_ref[0].astype(BF16), wb_ref[...], preferred_element_type=F32)
    x1 = x_ref[0] + mix
    x1_ref[...] = x1
    q_ref[...] = jnp.dot(_rms(x1, g_ref[...]).astype(BF16), wq_ref[...],
                         preferred_element_type=F32).astype(BF16)
    hd = XATTN_HEAD_DIM
    for h in range(XATTN_HEADS):
        hs = slice(h * hd, (h + 1) * hd)
        s = lax.dot_general(q_ref[:, hs], k_ref[0, :, hs], (((1,), (1,)), ((), ())),
                            preferred_element_type=F32) * (hd ** -0.5)
        s = s - jnp.max(s, axis=-1, keepdims=True)
        e = jnp.exp(s)
        pr = e / jnp.sum(e, axis=-1, keepdims=True)
        ctx_ref[:, hs] = jnp.dot(pr.astype(BF16), v_ref[0, :, hs], preferred_element_type=F32).astype(BF16)
    o_ref[0] = x1_ref[...] + jnp.dot(ctx_ref[...], wo_ref[...], preferred_element_type=F32)


def _mix_attn(x, yc, yh, w_out, g, wq, k, v, wo, cast_weights):
    b, l, d = x.shape
    tm = MA_TM
    kc = yc.shape[-1]
    grid = (b, l // tm)
    tok = lambda width: pl.BlockSpec((1, tm, width), lambda bi, i: (bi, i, 0))
    mem = pl.BlockSpec((1, N_MEM, d), lambda bi, i: (bi, 0, 0))
    cast_specs, cast_shapes = _cast_specs(cast_weights, grid)
    outs = pl.pallas_call(
        functools.partial(_mix_attn_kernel, n_cast=len(cast_weights)),
        grid=grid,
        in_specs=[tok(d), tok(kc), tok(kc),
                  _const_spec((kc, d), (0, 0)), _const_spec((kc, d), (1, 0)),
                  pl.BlockSpec((1, d), lambda bi, i: (0, 0)),
                  _const_spec((d, d)), mem, mem, _const_spec((d, d))] + cast_specs,
        out_specs=[tok(d)] + cast_specs,
        out_shape=[jax.ShapeDtypeStruct((b, l, d), F32)] + cast_shapes,
        scratch_shapes=[pltpu.VMEM((tm, d), F32), pltpu.VMEM((tm, d), BF16), pltpu.VMEM((tm, d), BF16)],
        compiler_params=_params("arbitrary", "arbitrary", vmem_limit=MA_VMEM_LIMIT),
        name="mix_attn",
    )(x, yc, yh, w_out, w_out, g.reshape(1, d), wq, k, v, wo, *cast_weights)
    return outs[0], outs[1:]


FFN_TM = 1024
FFN_FC = 512
FFN_VMEM_LIMIT = 62 * 1024 * 1024
FFN_HALO = 16


def _ffn_kernel(x_ref, xp_ref, xn_ref, g_ref, wg_ref, wv_ref, cw_ref, cb_ref, wd_ref, gf_ref,
                o_ref, h_ref, ge_ref):
    i = pl.program_id(1)
    j = pl.program_id(2)
    nt = pl.num_programs(1)
    nf = pl.num_programs(2)
    tm, hl = FFN_TM, FFN_HALO

    @pl.when(j == 0)
    def _():
        g = g_ref[...]
        h_ref[0:hl, :] = (_rms(xp_ref[0], g) * (i > 0).astype(F32)).astype(BF16)
        h_ref[hl:hl + tm, :] = _rms(x_ref[0], g).astype(BF16)
        h_ref[hl + tm:hl + tm + hl, :] = (_rms(xn_ref[0], g) * (i < nt - 1).astype(F32)).astype(BF16)
        o_ref[...] = jnp.zeros_like(o_ref)

    ge_ref[...] = jnp.dot(h_ref[...], wg_ref[...], preferred_element_type=F32)
    val = jnp.dot(h_ref[pl.ds(hl, tm), :], wv_ref[...], preferred_element_type=F32)
    gate = (cw_ref[0:1, :] * ge_ref[pl.ds(hl - 1, tm), :]
            + cw_ref[1:2, :] * ge_ref[pl.ds(hl, tm), :]
            + cw_ref[2:3, :] * ge_ref[pl.ds(hl + 1, tm), :]
            + cb_ref[...])
    act = (gate * _sigmoid(gate) * val).astype(BF16)
    o_ref[0] += jnp.dot(act, wd_ref[...], preferred_element_type=F32)

    @pl.when(j == nf - 1)
    def _():
        o_ref[0] = _rms(x_ref[0] + o_ref[0], gf_ref[...])


def _ffn(x, g_ffn, w_up, dw_w, dw_b, w_down, g_final):
    b, l, d = x.shape
    tm, fc, hl = FFN_TM, FFN_FC, FFN_HALO
    nf = D_FF // fc
    hb = tm // hl
    last_hb = l // hl - 1
    return pl.pallas_call(
        _ffn_kernel,
        grid=(b, l // tm, nf),
        in_specs=[pl.BlockSpec((1, tm, d), lambda bi, i, j: (bi, i, 0)),
                  pl.BlockSpec((1, hl, d), lambda bi, i, j: (bi, jnp.maximum(i * hb - 1, 0), 0)),
                  pl.BlockSpec((1, hl, d), lambda bi, i, j: (bi, jnp.minimum((i + 1) * hb, last_hb), 0)),
                  pl.BlockSpec((1, d), lambda bi, i, j: (0, 0)),
                  pl.BlockSpec((d, fc), lambda bi, i, j: (0, j)),
                  pl.BlockSpec((d, fc), lambda bi, i, j: (0, nf + j)),
                  pl.BlockSpec((3, fc), lambda bi, i, j: (0, j)),
                  pl.BlockSpec((1, fc), lambda bi, i, j: (0, j)),
                  pl.BlockSpec((fc, d), lambda bi, i, j: (j, 0)),
                  pl.BlockSpec((1, d), lambda bi, i, j: (0, 0))],
        out_specs=pl.BlockSpec((1, tm, d), lambda bi, i, j: (bi, i, 0)),
        out_shape=jax.ShapeDtypeStruct((b, l, d), F32),
        scratch_shapes=[pltpu.VMEM((tm + 2 * hl, d), BF16),
                        pltpu.VMEM((tm + 2 * hl, fc), F32)],
        compiler_params=_params("parallel", "parallel", "arbitrary", vmem_limit=FFN_VMEM_LIMIT),
        name="ffn",
    )(x, x, x, g_ffn.reshape(1, d), w_up, w_up, dw_w, dw_b.reshape(1, -1), w_down, g_final.reshape(1, d))


def _position_features(length):
    t = jnp.linspace(0.0, 1.0, length, dtype=F32)[:, None]
    w = 2.0 * math.pi * jnp.arange(length, dtype=F32)[:, None] / length
    bands = (HYENA_EMB - 1) // 2
    f = jnp.linspace(1e-4, bands - 1, bands, dtype=F32)[None, :]
    return jnp.concatenate([t, jnp.cos(f * w), -jnp.sin(f * w)], axis=-1)


def _decay_rates():
    max_decay = math.log(HYENA_TARGET) / HYENA_FAST_DECAY
    min_decay = math.log(HYENA_TARGET) / HYENA_SLOW_DECAY
    return jnp.abs(jnp.linspace(min_decay, max_decay, HYENA_WIDTH, dtype=F32)).reshape(1, HYENA_WIDTH)


def kernel(x, mem, g_mix, w_in, conf_dw_w, conf_dw_b, conf_ln_g, conf_ln_b, hyena_short_w, hyena_short_b, hyena_w1, hyena_b1, hyena_w_inner, hyena_b_inner, hyena_w_filt, hyena_sin_freq, hyena_skip, w_out, g_xattn, g_mem, w_q, w_k, w_v, w_o, g_ffn, w_ffn_up, ffn_dw_w, ffn_dw_b, w_ffn_down, g_final):
    b, l, d = x.shape
    consts = _dft_constants()

    hmlp = _filter_mlp(_position_features(l), hyena_w1[0], hyena_b1[0], hyena_w_inner[0],
                       hyena_b_inner[0], hyena_sin_freq[0])
    kf, (w_in_b, w_out_b, w_q_b, w_o_b) = _filter_spectrum(
        hmlp, hyena_w_filt[0], _decay_rates(), consts[1], consts[2], [w_in[0], w_out[0], w_q[0], w_o[0]])

    u_conf = _conf_in(x.reshape(b * l, d), g_mix[0], w_in_b).reshape(b, l, CONF_WIDTH)
    y_conf = _conformer(u_conf, conf_dw_w[0], conf_dw_b[0], conf_ln_g[0], conf_ln_b[0])

    uc = _hyena_in(x, g_mix[0], w_in_b, hyena_short_w[0], hyena_short_b[0])
    ncb = HYENA_WIDTH // LANES
    z1 = _hyena_conv(uc, 0, uc, ncb, kf, 0, hyena_skip[0], consts, slab_order_out=True)
    z2 = _hyena_conv(z1, 0, uc, 2 * ncb, kf, 1, hyena_skip[0], consts, slab_order_out=False)

    memf = mem.reshape(b * N_MEM, d)
    k = _norm_matmul(memf, g_mem[0], w_k[0], b * N_MEM, 1024, BF16).reshape(b, N_MEM, d)
    v = _norm_matmul(memf, g_mem[0], w_v[0], b * N_MEM, 1024, BF16).reshape(b, N_MEM, d)
    x2, (w_up_b, w_down_b) = _mix_attn(x, y_conf, z2, w_out_b, g_xattn[0], w_q_b, k, v, w_o_b,
                                       [w_ffn_up[0], w_ffn_down[0]])

    return _ffn(x2, g_ffn[0], w_up_b, ffn_dw_w[0], ffn_dw_b[0], w_down_b, g_final)
```

```python
import functools
import math

import numpy as np
import jax
import jax.numpy as jnp
from jax import lax
from jax.experimental import pallas as pl
from jax.experimental.pallas import tpu as pltpu

F32 = jnp.float32
BF16 = jnp.bfloat16

D_MODEL = 2048
BATCH = 2
SEQ = 4096
CONF_WIDTH = 1024
HYENA_WIDTH = 1024
CONF_KERNEL = 31
SHORT_KERNEL = 3
HYENA_ORDER = 2
HYENA_EMB = 33
HYENA_FILTER_ORDER = 64
HYENA_INNER_MLPS = 2
HYENA_FAST_DECAY = 0.3
HYENA_SLOW_DECAY = 1.5
HYENA_TARGET = 1e-2
N_MEM = 256
XATTN_HEADS = 4
XATTN_HEAD_DIM = D_MODEL // XATTN_HEADS
D_FF = 5632
EPS = 1e-6
IN_COLS = 2 * CONF_WIDTH + (HYENA_ORDER + 1) * HYENA_WIDTH

LANES = 128
VMEM_LIMIT = 56 * 1024 * 1024

N_FFT = 2 * SEQ
FFT_N1 = 128
FFT_N2 = 64
FFT_NZ1 = SEQ // FFT_N2
FFT_PITCH = 2 * FFT_N1 + 8
FFT_UNROLL = 16
FILTER_RC = 512
FILTER_PITCH = FFT_N2 + 8


def _params(*sem, vmem_limit=VMEM_LIMIT):
    return pltpu.CompilerParams(dimension_semantics=sem, vmem_limit_bytes=vmem_limit)


def _const_spec(shape, index=None):
    index = (0,) * len(shape) if index is None else index
    return pl.BlockSpec(shape, lambda *_: index, pipeline_mode=pl.Buffered(1))


def _sigmoid(x):
    return 0.5 * (jnp.tanh(0.5 * x) + 1.0)


def _cast_specs(weights, grid_shape):
    nsteps = math.prod(grid_shape)

    def step(*g):
        s = g[0]
        for gi, n in zip(g[1:], grid_shape[1:]):
            s = s * n + gi
        return (s, 0)

    specs, shapes = [], []
    for w in weights:
        rows, cols = w.shape
        assert rows % (16 * nsteps) == 0, (w.shape, nsteps)
        specs.append(pl.BlockSpec((rows // nsteps, cols), step))
        shapes.append(jax.ShapeDtypeStruct(w.shape, BF16))
    return specs, shapes


def _cast_blocks(in_refs, out_refs):
    for i_ref, o_ref in zip(in_refs, out_refs):
        o_ref[...] = i_ref[...].astype(BF16)


def _rms(x, g):
    ms = jnp.mean(x * x, axis=-1, keepdims=True)
    return x * lax.rsqrt(ms + EPS) * g


@functools.lru_cache(maxsize=None)
def _dft_constants():
    k1 = np.arange(FFT_N1)
    f1 = np.exp(-2j * np.pi * np.outer(k1, k1) / FFT_N1)
    fh = f1[:, :FFT_NZ1]
    fa = np.block([[fh.real, -fh.imag], [fh.imag, fh.real]])
    perm = np.concatenate([np.arange(FFT_NZ1), FFT_N1 - 1 - np.arange(FFT_N1 - FFT_NZ1)])
    fp = f1[:, perm]
    fak = np.concatenate([fp.real, fp.imag], axis=0)
    n2 = np.arange(FFT_N2)
    kk = k1[:, None, None] + FFT_N1 * n2[None, :, None]
    ang = (n2[None, None, :] * kk) % N_FFT
    tc = np.exp(-2j * np.pi * ang / N_FFT)
    t = np.concatenate([np.concatenate([tc.real, -tc.imag], 2),
                        np.concatenate([tc.imag, tc.real], 2)], 1)
    ti = np.transpose(t, (0, 2, 1))
    g = np.conj(f1[:, :FFT_NZ1]).T / N_FFT
    fd = np.block([[g.real, -g.imag], [g.imag, g.real]])
    return tuple(jnp.asarray(a, dtype=BF16) for a in (fa, fak, t, ti, fd))


def _norm_matmul_kernel(x_ref, g_ref, w_ref, o_ref, h_ref):
    @pl.when(pl.program_id(1) == 0)
    def _():
        h_ref[...] = _rms(x_ref[...], g_ref[...]).astype(BF16)

    o_ref[...] = jnp.dot(h_ref[...], w_ref[...].astype(BF16), preferred_element_type=F32).astype(o_ref.dtype)


def _norm_matmul(x, g, w, tm, tn, out_dtype):
    m, d = x.shape
    n = w.shape[1]
    return pl.pallas_call(
        _norm_matmul_kernel,
        grid=(m // tm, n // tn),
        in_specs=[pl.BlockSpec((tm, d), lambda i, j: (i, 0)),
                  pl.BlockSpec((1, d), lambda i, j: (0, 0)),
                  pl.BlockSpec((d, tn), lambda i, j: (0, j))],
        out_specs=pl.BlockSpec((tm, tn), lambda i, j: (i, j)),
        out_shape=jax.ShapeDtypeStruct((m, n), out_dtype),
        scratch_shapes=[pltpu.VMEM((tm, d), BF16)],
        compiler_params=_params("parallel", "arbitrary"),
        name="norm_matmul",
    )(x, g.reshape(1, d), w)


CIN_TM = 1024
CIN_TN = 512


def _conf_in_kernel(x_ref, g_ref, wa_ref, wg_ref, o_ref, h_ref):
    @pl.when(pl.program_id(1) == 0)
    def _():
        h_ref[...] = _rms(x_ref[...], g_ref[...]).astype(BF16)

    h = h_ref[...]
    a = jnp.dot(h, wa_ref[...], preferred_element_type=F32)
    g = jnp.dot(h, wg_ref[...], preferred_element_type=F32)
    o_ref[...] = (a * _sigmoid(g)).astype(o_ref.dtype)


def _conf_in(x, g, w_in):
    m, d = x.shape
    tm, tn = CIN_TM, CIN_TN
    ncol = CONF_WIDTH // tn
    return pl.pallas_call(
        _conf_in_kernel,
        grid=(m // tm, ncol),
        in_specs=[pl.BlockSpec((tm, d), lambda i, j: (i, 0)),
                  pl.BlockSpec((1, d), lambda i, j: (0, 0)),
                  pl.BlockSpec((d, tn), lambda i, j: (0, j)),
                  pl.BlockSpec((d, tn), lambda i, j: (0, ncol + j))],
        out_specs=[pl.BlockSpec((tm, tn), lambda i, j: (i, j)),
                   pl.BlockSpec((tm, d), lambda i, j: (i, 0))],
        out_shape=[jax.ShapeDtypeStruct((m, CONF_WIDTH), BF16), jax.ShapeDtypeStruct((m, d), BF16)],
        compiler_params=_params("parallel", "arbitrary"),
        name="conf_in",
    )(x, g.reshape(1, d), w_in, w_in)


CONF_T = 512
CONF_HALO = 16
CONF_RC = 64


def _conformer_kernel(u_in, up_in, un_in, w_ref, b_ref, lg_ref, lb_ref, o_ref, u_ref, y_ref):
    i = pl.program_id(1)
    nt = pl.num_programs(1)
    t = CONF_T
    h = CONF_HALO
    prev_ok = (i > 0).astype(F32)
    next_ok = (i < nt - 1).astype(F32)
    nc = CONF_WIDTH // LANES
    for c in range(nc):
        cs = slice(c * LANES, (c + 1) * LANES)
        u_ref[c, 0:h, :] = up_in[0, :, cs].astype(F32) * prev_ok
        u_ref[c, h:h + t, :] = u_in[0, :, cs].astype(F32)
        u_ref[c, h + t:h + t + h, :] = un_in[0, :, cs].astype(F32) * next_ok

    pad = CONF_KERNEL // 2

    def conv_chunk(c, carry):
        w = w_ref[c]
        bias = b_ref[c]
        for r in range(t // CONF_RC):
            r0 = r * CONF_RC
            acc = jnp.zeros((CONF_RC, LANES), F32) + bias
            for j in range(CONF_KERNEL):
                acc = acc + w[j:j + 1, :] * u_ref[c, r0 + h - pad + j:r0 + h - pad + j + CONF_RC, :]
            y_ref[c, r0:r0 + CONF_RC, :] = acc
        return carry

    lax.fori_loop(0, nc, conv_chunk, 0)

    def ln_chunk(r, carry):
        r0 = pl.multiple_of(r * CONF_RC, CONF_RC)
        ys = [y_ref[c, pl.ds(r0, CONF_RC), :] for c in range(nc)]
        tot = ys[0]
        for y in ys[1:]:
            tot = tot + y
        mu = jnp.sum(tot, axis=-1, keepdims=True) * (1.0 / CONF_WIDTH)
        yc = [y - mu for y in ys]
        sq = yc[0] * yc[0]
        for y in yc[1:]:
            sq = sq + y * y
        inv = lax.rsqrt(jnp.sum(sq, axis=-1, keepdims=True) * (1.0 / CONF_WIDTH) + EPS)
        for c in range(nc):
            cs = slice(c * LANES, (c + 1) * LANES)
            yn = yc[c] * inv * lg_ref[:, cs] + lb_ref[:, cs]
            o_ref[0, pl.ds(r0, CONF_RC), cs] = (yn * _sigmoid(yn)).astype(o_ref.dtype)
        return carry

    lax.fori_loop(0, t // CONF_RC, ln_chunk, 0, unroll=2)


def _conformer(u, dw_w, dw_b, ln_g, ln_b):
    b, l, c = u.shape
    t, h = CONF_T, CONF_HALO
    nt = l // t
    hb = t // h
    last_hb = l // h - 1
    vec = pl.BlockSpec((1, c), lambda bi, i: (0, 0))
    nc = c // LANES
    chunked = lambda rows: pl.BlockSpec((nc, rows, LANES), lambda bi, i: (0, 0, 0))
    w3 = dw_w.reshape(CONF_KERNEL, nc, LANES).transpose(1, 0, 2)
    return pl.pallas_call(
        _conformer_kernel,
        grid=(b, nt),
        in_specs=[pl.BlockSpec((1, t, c), lambda bi, i: (bi, i, 0)),
                  pl.BlockSpec((1, h, c), lambda bi, i: (bi, jnp.maximum(i * hb - 1, 0), 0)),
                  pl.BlockSpec((1, h, c), lambda bi, i: (bi, jnp.minimum((i + 1) * hb, last_hb), 0)),
                  chunked(CONF_KERNEL), chunked(1), vec, vec],
        out_specs=pl.BlockSpec((1, t, c), lambda bi, i: (bi, i, 0)),
        out_shape=jax.ShapeDtypeStruct((b, l, c), BF16),
        scratch_shapes=[pltpu.VMEM((nc, t + 2 * h, LANES), F32), pltpu.VMEM((nc, t, LANES), F32)],
        compiler_params=_params("parallel", "parallel"),
        name="conformer",
    )(u, u, u, w3, dw_b.reshape(nc, 1, LANES), ln_g.reshape(1, c), ln_b.reshape(1, c))


HIN_TM = 1024
HIN_TN = 1024
HIN_HALO = 16
HIN_N1 = HIN_TM // FFT_N2
HIN_PITCH = FFT_N2 + 8


def _hyena_in_kernel(hc_ref, hp_ref, hn_ref, w_ref, cw_ref, cb_ref, o_ref, h_ref, e_ref, s_ref):
    i = pl.program_id(1)
    j = pl.program_id(2)
    nt = pl.num_programs(1)
    tm, hl = HIN_TM, HIN_HALO

    @pl.when(j == 0)
    def _():
        h_ref[0:hl, :] = jnp.where(i > 0, hp_ref[0], jnp.zeros_like(hp_ref[0]))
        h_ref[hl:hl + tm, :] = hc_ref[0]
        h_ref[hl + tm:hl + tm + hl, :] = jnp.where(i < nt - 1, hn_ref[0], jnp.zeros_like(hn_ref[0]))

    e_ref[...] = jnp.dot(h_ref[...], w_ref[...], preferred_element_type=F32)
    nlc = HIN_TN // LANES
    for r in range(HIN_N1):
        r0 = hl + r * FFT_N2
        for lc in range(nlc):
            cs = slice(lc * LANES, (lc + 1) * LANES)
            acc = (cw_ref[0:1, cs] * e_ref[r0 - 1:r0 - 1 + FFT_N2, cs]
                   + cw_ref[1:2, cs] * e_ref[r0:r0 + FFT_N2, cs]
                   + cw_ref[2:3, cs] * e_ref[r0 + 1:r0 + 1 + FFT_N2, cs]
                   + cb_ref[:, cs])
            s_ref[lc, r * HIN_PITCH:r * HIN_PITCH + FFT_N2, :] = acc

    def gather(n2, carry):
        for lc in range(nlc):
            rows = s_ref[lc, pl.ds(n2, HIN_N1, stride=HIN_PITCH), :]
            o_ref[0, n2, :, lc * LANES:(lc + 1) * LANES] = rows.astype(o_ref.dtype)
        return carry

    lax.fori_loop(0, FFT_N2, gather, 0, unroll=8)


def _hyena_in(hn, w_in, cw, cb):
    b, l, d = hn.shape
    tm, tn, hl = HIN_TM, HIN_TN, HIN_HALO
    ncol = (HYENA_ORDER + 1) * HYENA_WIDTH // tn
    col0 = 2 * CONF_WIDTH // tn
    hb = tm // hl
    last_hb = l // hl - 1
    return pl.pallas_call(
        _hyena_in_kernel,
        grid=(b, l // tm, ncol),
        in_specs=[pl.BlockSpec((1, tm, d), lambda bi, i, j: (bi, i, 0)),
                  pl.BlockSpec((1, hl, d), lambda bi, i, j: (bi, jnp.maximum(i * hb - 1, 0), 0)),
                  pl.BlockSpec((1, hl, d), lambda bi, i, j: (bi, jnp.minimum((i + 1) * hb, last_hb), 0)),
                  pl.BlockSpec((d, tn), lambda bi, i, j: (0, col0 + j)),
                  pl.BlockSpec((SHORT_KERNEL, tn), lambda bi, i, j: (0, j)),
                  pl.BlockSpec((1, tn), lambda bi, i, j: (0, j))],
        out_specs=pl.BlockSpec((1, FFT_N2, HIN_N1, tn), lambda bi, i, j: (bi, 0, i, j)),
        out_shape=jax.ShapeDtypeStruct((b, FFT_N2, FFT_NZ1, ncol * tn), BF16),
        scratch_shapes=[pltpu.VMEM((tm + 2 * hl, d), BF16),
                        pltpu.VMEM((tm + 2 * hl, tn), F32),
                        pltpu.VMEM((tn // LANES, HIN_N1 * HIN_PITCH, LANES), F32)],
        compiler_params=_params("parallel", "parallel", "arbitrary"),
        name="hyena_in",
    )(hn, hn, hn, w_in, cw, cb.reshape(1, -1))


def _filter_mlp_kernel(z_ref, w1_ref, b1_ref, wi_ref, bi_ref, fr_ref, o_ref):
    hi = lax.Precision.HIGHEST
    fr = fr_ref[...]
    h = jnp.sin(fr * (jnp.dot(z_ref[...], w1_ref[...], precision=hi, preferred_element_type=F32) + b1_ref[...]))
    for i in range(HYENA_INNER_MLPS):
        h = jnp.sin(fr * (jnp.dot(h, wi_ref[i], precision=hi, preferred_element_type=F32) + bi_ref[i]))
    o_ref[...] = h


def _filter_mlp(z, w1, b1, w_inner, b_inner, freq):
    half, hw = SEQ // 2, LANES // 2

    def two_blocks(a):
        out = jnp.zeros(a.shape[:-2] + (LANES, LANES), F32)
        out = out.at[..., :a.shape[-2], :a.shape[-1]].set(a)
        return out.at[..., hw:hw + a.shape[-2], hw:hw + a.shape[-1]].set(a)

    def twice(a):
        out = jnp.zeros(a.shape[:-1] + (1, LANES), F32)
        out = out.at[..., 0, :a.shape[-1]].set(a)
        return out.at[..., 0, hw:hw + a.shape[-1]].set(a)

    zp = jnp.zeros((half, LANES), F32).at[:, :HYENA_EMB].set(z[:half]).at[:, hw:hw + HYENA_EMB].set(z[half:])
    return pl.pallas_call(
        _filter_mlp_kernel,
        out_shape=jax.ShapeDtypeStruct((half, LANES), F32),
        compiler_params=pltpu.CompilerParams(vmem_limit_bytes=VMEM_LIMIT),
        name="filter_mlp",
    )(zp, two_blocks(w1), twice(b1), two_blocks(w_inner), twice(b_inner), twice(freq))


def _stage_b_rhs(a_ref, k1):
    ar = a_ref[pl.ds(k1, FFT_N2, stride=FFT_PITCH), :]
    ai = a_ref[pl.ds(FFT_N1 + k1, FFT_N2, stride=FFT_PITCH), :]
    return jnp.concatenate([ar, ai], axis=0).astype(BF16)


def _slab_rows(n2):
    return pl.ds(pl.multiple_of(n2 * FFT_PITCH, 8), 2 * FFT_N1)


def _filter_spectrum_kernel(h_ref, w_ref, dl_ref, fak_ref, t_ref, *refs, n_cast):
    cast_in, o_ref, cast_out = refs[:n_cast], refs[n_cast], refs[n_cast + 1:2 * n_cast + 1]
    kt_ref, a_ref = refs[2 * n_cast + 1:]
    _cast_blocks(cast_in, cast_out)
    rc = FILTER_RC
    half_chunks = SEQ // 2 // rc
    ws = [jnp.concatenate([w_ref[0, 0, p], w_ref[0, 1, p]], axis=1).astype(BF16) for p in range(2)]
    dl = jnp.concatenate([dl_ref[...], dl_ref[...]], axis=1)

    def gen(r, ss, p):
        r0 = pl.multiple_of(r * rc, rc)
        rows = lax.broadcasted_iota(jnp.int32, (rc, 2 * LANES), 0) + r0
        lanes = lax.broadcasted_iota(jnp.int32, (rc, 2 * LANES), 1)
        decay = jnp.exp(-(rows.astype(F32) * (1.0 / (SEQ - 1))) * dl)
        h = h_ref[pl.ds(pl.multiple_of(r0 - p * (SEQ // 2), rc), rc), :]
        k = jnp.dot(h.astype(BF16), ws[p], preferred_element_type=F32)
        k = k * decay
        k = jnp.where((rows >= SEQ - 1) & (lanes >= LANES), 0.0, k)
        for n in range(rc // FFT_N2):
            dst = pl.ds(pl.multiple_of((r * (rc // FFT_N2) + n) * FILTER_PITCH, 8), FFT_N2)
            kt_ref[0, dst, :] = k[n * FFT_N2:(n + 1) * FFT_N2, :LANES]
            kt_ref[1, dst, :] = k[n * FFT_N2:(n + 1) * FFT_N2, LANES:]
        return ss + jnp.sum(k * k, axis=0, keepdims=True)

    ss = jnp.zeros((1, 2 * LANES), F32)
    for p in range(2):
        ss = lax.fori_loop(p * half_chunks, (p + 1) * half_chunks, functools.partial(gen, p=p), ss)
    scale = lax.rsqrt(ss[:, :LANES] + ss[:, LANES:] + EPS)

    def filter_slab(n2):
        fwd = kt_ref[0, pl.ds(n2, FFT_NZ1, stride=FILTER_PITCH), :]
        bwd = kt_ref[1, pl.ds(FFT_N2 - 1 - n2, FFT_NZ1, stride=FILTER_PITCH), :]
        return (jnp.concatenate([fwd, bwd], axis=0) * scale).astype(BF16)

    def stage_a(p, carry):
        n2 = 2 * p
        rhs = jnp.concatenate([filter_slab(n2), filter_slab(n2 + 1)], axis=1)
        a = jnp.dot(fak_ref[...], rhs, preferred_element_type=F32)
        a_ref[_slab_rows(n2), :] = a[:, :LANES]
        a_ref[_slab_rows(n2 + 1), :] = a[:, LANES:]
        return carry

    lax.fori_loop(0, FFT_N2 // 2, stage_a, 0, unroll=FFT_UNROLL // 2)

    def stage_b(k1, carry):
        b = jnp.dot(t_ref[k1], _stage_b_rhs(a_ref, k1), preferred_element_type=F32)
        o_ref[0, k1] = b.astype(o_ref.dtype)
        return carry

    lax.fori_loop(0, FFT_N1, stage_b, 0, unroll=FFT_UNROLL)


def _filter_spectrum(hmlp, w_filt, deltas, fak, t, cast_weights):
    wf = w_filt.reshape(HYENA_FILTER_ORDER, HYENA_ORDER, 2, HYENA_WIDTH).transpose(1, 2, 0, 3)
    hw = LANES // 2
    wf = (jnp.zeros((HYENA_ORDER, 2, 2, LANES, HYENA_WIDTH), F32)
          .at[:, :, 0, :HYENA_FILTER_ORDER].set(wf).at[:, :, 1, hw:hw + HYENA_FILTER_ORDER].set(wf))
    ncb = HYENA_WIDTH // LANES
    grid = (HYENA_ORDER, ncb)
    cast_specs, cast_shapes = _cast_specs(cast_weights, grid)
    outs = pl.pallas_call(
        functools.partial(_filter_spectrum_kernel, n_cast=len(cast_weights)),
        grid=grid,
        in_specs=[_const_spec((SEQ // 2, LANES)),
                  pl.BlockSpec((1, 2, 2, LANES, LANES), lambda o, c: (o, 0, 0, 0, c)),
                  pl.BlockSpec((1, LANES), lambda o, c: (0, c)),
                  _const_spec((2 * FFT_N1, FFT_N1)),
                  _const_spec((FFT_N1, 2 * FFT_N2, 2 * FFT_N2))] + cast_specs,
        out_specs=[pl.BlockSpec((1, FFT_N1, 2 * FFT_N2, LANES), lambda o, c: (o, 0, 0, c))] + cast_specs,
        out_shape=[jax.ShapeDtypeStruct((HYENA_ORDER, FFT_N1, 2 * FFT_N2, HYENA_WIDTH), F32)] + cast_shapes,
        scratch_shapes=[pltpu.VMEM((2, FFT_NZ1 * FILTER_PITCH, LANES), F32),
                        pltpu.VMEM((FFT_N2 * FFT_PITCH, LANES), F32)],
        compiler_params=_params("arbitrary", "arbitrary"),
        name="filter_spectrum",
    )(hmlp, wf, deltas, fak, t, *cast_weights)
    return outs[0], outs[1:]


def _hyena_conv_kernel(u_ref, xm_ref, kf_ref, sk_ref, fa_ref, t_ref, ti_ref, fd_ref, o_ref, a_ref, y_ref, *,
                       slab_order_out):
    def stage_a(p, carry):
        n2 = 2 * p
        rhs = jnp.concatenate(
            [jnp.concatenate([u_ref[0, n], u_ref[1, n]], axis=0) for n in (n2, n2 + 1)], axis=1).astype(BF16)
        a = jnp.dot(fa_ref[...], rhs, preferred_element_type=F32)
        a_ref[_slab_rows(n2), :] = a[:, :LANES]
        a_ref[_slab_rows(n2 + 1), :] = a[:, LANES:]
        return carry

    lax.fori_loop(0, FFT_N2 // 2, stage_a, 0, unroll=FFT_UNROLL // 2)

    def stage_b(k1, carry):
        b = jnp.dot(t_ref[k1], _stage_b_rhs(a_ref, k1), preferred_element_type=F32)
        kf = kf_ref[0, k1]
        br, bi = b[:FFT_N2], b[FFT_N2:]
        kr, ki = kf[:FFT_N2], kf[FFT_N2:]
        y_ref[k1] = jnp.concatenate([br * kr - bi * ki, br * ki + bi * kr], axis=0).astype(BF16)
        return carry

    lax.fori_loop(0, FFT_N1, stage_b, 0, unroll=FFT_UNROLL)

    def stage_c(k1, carry):
        c = jnp.dot(ti_ref[k1], y_ref[k1], preferred_element_type=F32)
        a_ref[pl.ds(k1, FFT_N2, stride=FFT_PITCH), :] = c[:FFT_N2]
        a_ref[pl.ds(FFT_N1 + k1, FFT_N2, stride=FFT_PITCH), :] = c[FFT_N2:]
        return carry

    lax.fori_loop(0, FFT_N1, stage_c, 0, unroll=FFT_UNROLL)

    sk = sk_ref[0]

    def stage_d(p, carry):
        n2 = 2 * p
        cb = jnp.concatenate([a_ref[_slab_rows(n2), :], a_ref[_slab_rows(n2 + 1), :]], axis=1).astype(BF16)
        y = jnp.dot(fd_ref[...], cb, preferred_element_type=F32)
        for s in range(2):
            for bi in range(2):
                yb = y[bi * FFT_NZ1:(bi + 1) * FFT_NZ1, s * LANES:(s + 1) * LANES]
                z = xm_ref[bi, n2 + s].astype(F32) * (yb + sk * u_ref[bi, n2 + s].astype(F32))
                if slab_order_out:
                    o_ref[bi, n2 + s] = z.astype(o_ref.dtype)
                else:
                    o_ref[bi, pl.ds(n2 + s, FFT_NZ1, stride=FFT_N2), :] = z
        return carry

    lax.fori_loop(0, FFT_N2 // 2, stage_d, 0, unroll=FFT_UNROLL // 2)


def _hyena_conv(u, u_col0, xm, xm_col0, kf, order, skip, consts, slab_order_out):
    fa, _, t, ti, fd = consts
    ncb = HYENA_WIDTH // LANES
    slab_block = (BATCH, FFT_N2, FFT_NZ1, LANES)
    if slab_order_out:
        out_spec = pl.BlockSpec(slab_block, lambda c: (0, 0, 0, c))
        out_shape = jax.ShapeDtypeStruct((BATCH, FFT_N2, FFT_NZ1, HYENA_WIDTH), BF16)
    else:
        out_spec = pl.BlockSpec((BATCH, SEQ, LANES), lambda c: (0, 0, c))
        out_shape = jax.ShapeDtypeStruct((BATCH, SEQ, HYENA_WIDTH), F32)
    return pl.pallas_call(
        functools.partial(_hyena_conv_kernel, slab_order_out=slab_order_out),
        grid=(ncb,),
        in_specs=[pl.BlockSpec(slab_block, lambda c: (0, 0, 0, u_col0 + c)),
                  pl.BlockSpec(slab_block, lambda c: (0, 0, 0, xm_col0 + c)),
                  pl.BlockSpec((1, FFT_N1, 2 * FFT_N2, LANES), lambda c: (order, 0, 0, c)),
                  pl.BlockSpec((1, 1, LANES), lambda c: (order, 0, c)),
                  _const_spec(fa.shape), _const_spec(t.shape), _const_spec(ti.shape), _const_spec(fd.shape)],
        out_specs=out_spec,
        out_shape=out_shape,
        scratch_shapes=[pltpu.VMEM((FFT_N2 * FFT_PITCH, LANES), F32),
                        pltpu.VMEM((FFT_N1, 2 * FFT_N2, LANES), BF16)],
        compiler_params=_params("parallel"),
        name="hyena_conv",
    )(u, xm, kf, skip.reshape(HYENA_ORDER, 1, HYENA_WIDTH), fa, t, ti, fd)


MA_TM = 256
MA_VMEM_LIMIT = 60 * 1024 * 1024


def _mix_attn_kernel(x_ref, yc_ref, yh_ref, wa_ref, wb_ref, g_ref, wq_ref, k_ref, v_ref, wo_ref, *refs, n_cast):
    cast_in, o_ref, cast_out = refs[:n_cast], refs[n_cast], refs[n_cast + 1:2 * n_cast + 1]
    x1_ref, q_ref, ctx_ref = refs[2 * n_cast + 1:]
    _cast_blocks(cast_in, cast_out)
    mix = jnp.dot(yc_ref[0], wa_ref[...], preferred_element_type=F32)
    mix = mix + jnp.dot(yh_ref[0].astype(BF16), wb_ref[...], preferred_element_type=F32)
    x1 = x_ref[0] + mix
    x1_ref[...] = x1
    q_ref[...] = jnp.dot(_rms(x1, g_ref[...]).astype(BF16), wq_ref[...],
                         preferred_element_type=F32).astype(BF16)
    hd = XATTN_HEAD_DIM
    for h in range(XATTN_HEADS):
        hs = slice(h * hd, (h + 1) * hd)
        s = lax.dot_general(q_ref[:, hs], k_ref[0, :, hs], (((1,), (1,)), ((), ())),
                            preferred_element_type=F32) * (hd ** -0.5)
        s = s - jnp.max(s, axis=-1, keepdims=True)
        e = jnp.exp(s)
        pr = e / jnp.sum(e, axis=-1, keepdims=True)
        ctx_ref[:, hs] = jnp.dot(pr.astype(BF16), v_ref[0, :, hs], preferred_element_type=F32).astype(BF16)
    o_ref[0] = x1_ref[...] + jnp.dot(ctx_ref[...], wo_ref[...], preferred_element_type=F32)


def _mix_attn(x, yc, yh, w_out, g, wq, k, v, wo, cast_weights):
    b, l, d = x.shape
    tm = MA_TM
    kc = yc.shape[-1]
    grid = (b, l // tm)
    tok = lambda width: pl.BlockSpec((1, tm, width), lambda bi, i: (bi, i, 0))
    mem = pl.BlockSpec((1, N_MEM, d), lambda bi, i: (bi, 0, 0))
    cast_specs, cast_shapes = _cast_specs(cast_weights, grid)
    outs = pl.pallas_call(
        functools.partial(_mix_attn_kernel, n_cast=len(cast_weights)),
        grid=grid,
        in_specs=[tok(d), tok(kc), tok(kc),
                  _const_spec((kc, d), (0, 0)), _const_spec((kc, d), (1, 0)),
                  pl.BlockSpec((1, d), lambda bi, i: (0, 0)),
                  _const_spec((d, d)), mem, mem, _const_spec((d, d))] + cast_specs,
        out_specs=[tok(d)] + cast_specs,
        out_shape=[jax.ShapeDtypeStruct((b, l, d), F32)] + cast_shapes,
        scratch_shapes=[pltpu.VMEM((tm, d), F32), pltpu.VMEM((tm, d), BF16), pltpu.VMEM((tm, d), BF16)],
        compiler_params=_params("arbitrary", "arbitrary", vmem_limit=MA_VMEM_LIMIT),
        name="mix_attn",
    )(x, yc, yh, w_out, w_out, g.reshape(1, d), wq, k, v, wo, *cast_weights)
    return outs[0], outs[1:]


FFN_TM = 1024
FFN_FC = 512
FFN_VMEM_LIMIT = 62 * 1024 * 1024
FFN_HALO = 16


def _ffn_kernel(x_ref, xp_ref, xn_ref, g_ref, wg_ref, wv_ref, cw_ref, cb_ref, wd_ref, gf_ref,
                o_ref, h_ref, ge_ref):
    i = pl.program_id(1)
    j = pl.program_id(2)
    nt = pl.num_programs(1)
    nf = pl.num_programs(2)
    tm, hl = FFN_TM, FFN_HALO

    @pl.when(j == 0)
    def _():
        g = g_ref[...]
        h_ref[0:hl, :] = (_rms(xp_ref[0], g) * (i > 0).astype(F32)).astype(BF16)
        h_ref[hl:hl + tm, :] = _rms(x_ref[0], g).astype(BF16)
        h_ref[hl + tm:hl + tm + hl, :] = (_rms(xn_ref[0], g) * (i < nt - 1).astype(F32)).astype(BF16)
        o_ref[...] = jnp.zeros_like(o_ref)

    ge_ref[...] = jnp.dot(h_ref[...], wg_ref[...], preferred_element_type=F32)
    val = jnp.dot(h_ref[pl.ds(hl, tm), :], wv_ref[...], preferred_element_type=F32)
    gate = (cw_ref[0:1, :] * ge_ref[pl.ds(hl - 1, tm), :]
            + cw_ref[1:2, :] * ge_ref[pl.ds(hl, tm), :]
            + cw_ref[2:3, :] * ge_ref[pl.ds(hl + 1, tm), :]
            + cb_ref[...])
    act = (gate * _sigmoid(gate) * val).astype(BF16)
    o_ref[0] += jnp.dot(act, wd_ref[...], preferred_element_type=F32)

    @pl.when(j == nf - 1)
    def _():
        o_ref[0] = _rms(x_ref[0] + o_ref[0], gf_ref[...])


def _ffn(x, g_ffn, w_up, dw_w, dw_b, w_down, g_final):
    b, l, d = x.shape
    tm, fc, hl = FFN_TM, FFN_FC, FFN_HALO
    nf = D_FF // fc
    hb = tm // hl
    last_hb = l // hl - 1
    return pl.pallas_call(
        _ffn_kernel,
        grid=(b, l // tm, nf),
        in_specs=[pl.BlockSpec((1, tm, d), lambda bi, i, j: (bi, i, 0)),
                  pl.BlockSpec((1, hl, d), lambda bi, i, j: (bi, jnp.maximum(i * hb - 1, 0), 0)),
                  pl.BlockSpec((1, hl, d), lambda bi, i, j: (bi, jnp.minimum((i + 1) * hb, last_hb), 0)),
                  pl.BlockSpec((1, d), lambda bi, i, j: (0, 0)),
                  pl.BlockSpec((d, fc), lambda bi, i, j: (0, j)),
                  pl.BlockSpec((d, fc), lambda bi, i, j: (0, nf + j)),
                  pl.BlockSpec((3, fc), lambda bi, i, j: (0, j)),
                  pl.BlockSpec((1, fc), lambda bi, i, j: (0, j)),
                  pl.BlockSpec((fc, d), lambda bi, i, j: (j, 0)),
                  pl.BlockSpec((1, d), lambda bi, i, j: (0, 0))],
        out_specs=pl.BlockSpec((1, tm, d), lambda bi, i, j: (bi, i, 0)),
        out_shape=jax.ShapeDtypeStruct((b, l, d), F32),
        scratch_shapes=[pltpu.VMEM((tm + 2 * hl, d), BF16),
                        pltpu.VMEM((tm + 2 * hl, fc), F32)],
        compiler_params=_params("parallel", "parallel", "arbitrary", vmem_limit=FFN_VMEM_LIMIT),
        name="ffn",
    )(x, x, x, g_ffn.reshape(1, d), w_up, w_up, dw_w, dw_b.reshape(1, -1), w_down, g_final.reshape(1, d))


def _position_features(length):
    t = jnp.linspace(0.0, 1.0, length, dtype=F32)[:, None]
    w = 2.0 * math.pi * jnp.arange(length, dtype=F32)[:, None] / length
    bands = (HYENA_EMB - 1) // 2
    f = jnp.linspace(1e-4, bands - 1, bands, dtype=F32)[None, :]
    return jnp.concatenate([t, jnp.cos(f * w), -jnp.sin(f * w)], axis=-1)


def _decay_rates():
    max_decay = math.log(HYENA_TARGET) / HYENA_FAST_DECAY
    min_decay = math.log(HYENA_TARGET) / HYENA_SLOW_DECAY
    return jnp.abs(jnp.linspace(min_decay, max_decay, HYENA_WIDTH, dtype=F32)).reshape(1, HYENA_WIDTH)


def kernel(x, mem, g_mix, w_in, conf_dw_w, conf_dw_b, conf_ln_g, conf_ln_b, hyena_short_w, hyena_short_b, hyena_w1, hyena_b1, hyena_w_inner, hyena_b_inner, hyena_w_filt, hyena_sin_freq, hyena_skip, w_out, g_xattn, g_mem, w_q, w_k, w_v, w_o, g_ffn, w_ffn_up, ffn_dw_w, ffn_dw_b, w_ffn_down, g_final):
    b, l, d = x.shape
    consts = _dft_constants()

    hmlp = _filter_mlp(_position_features(l), hyena_w1[0], hyena_b1[0], hyena_w_inner[0],
                       hyena_b_inner[0], hyena_sin_freq[0])
    kf, (w_in_b, w_out_b, w_q_b, w_o_b) = _filter_spectrum(
        hmlp, hyena_w_filt[0], _decay_rates(), consts[1], consts[2], [w_in[0], w_out[0], w_q[0], w_o[0]])

    u_conf, hn = _conf_in(x.reshape(b * l, d), g_mix[0], w_in_b)
    y_conf = _conformer(u_conf.reshape(b, l, CONF_WIDTH), conf_dw_w[0], conf_dw_b[0], conf_ln_g[0],
                        conf_ln_b[0])

    uc = _hyena_in(hn.reshape(b, l, d), w_in_b, hyena_short_w[0], hyena_short_b[0])
    ncb = HYENA_WIDTH // LANES
    z1 = _hyena_conv(uc, 0, uc, ncb, kf, 0, hyena_skip[0], consts, slab_order_out=True)
    z2 = _hyena_conv(z1, 0, uc, 2 * ncb, kf, 1, hyena_skip[0], consts, slab_order_out=False)

    memf = mem.reshape(b * N_MEM, d)
    k = _norm_matmul(memf, g_mem[0], w_k[0], b * N_MEM, 1024, BF16).reshape(b, N_MEM, d)
    v = _norm_matmul(memf, g_mem[0], w_v[0], b * N_MEM, 1024, BF16).reshape(b, N_MEM, d)
    x2, (w_up_b, w_down_b) = _mix_attn(x, y_conf, z2, w_out_b, g_xattn[0], w_q_b, k, v, w_o_b,
                                       [w_ffn_up[0], w_ffn_down[0]])

    return _ffn(x2, g_ffn[0], w_up_b, ffn_dw_w[0], ffn_dw_b[0], w_down_b, g_final)
```

```python
import functools
import math

import numpy as np
import jax
import jax.numpy as jnp
from jax import lax
from jax.experimental import pallas as pl
from jax.experimental.pallas import tpu as pltpu

F32 = jnp.float32
BF16 = jnp.bfloat16

D_MODEL = 2048
BATCH = 2
SEQ = 4096
CONF_WIDTH = 1024
HYENA_WIDTH = 1024
CONF_KERNEL = 31
SHORT_KERNEL = 3
HYENA_ORDER = 2
HYENA_EMB = 33
HYENA_FILTER_ORDER = 64
HYENA_INNER_MLPS = 2
HYENA_FAST_DECAY = 0.3
HYENA_SLOW_DECAY = 1.5
HYENA_TARGET = 1e-2
N_MEM = 256
XATTN_HEADS = 4
XATTN_HEAD_DIM = D_MODEL // XATTN_HEADS
D_FF = 5632
EPS = 1e-6
IN_COLS = 2 * CONF_WIDTH + (HYENA_ORDER + 1) * HYENA_WIDTH

LANES = 128
VMEM_LIMIT = 56 * 1024 * 1024

N_FFT = 2 * SEQ
FFT_N1 = 128
FFT_N2 = 64
FFT_NZ1 = SEQ // FFT_N2
FFT_PITCH = 2 * FFT_N1 + 8
FFT_UNROLL = 16
FILTER_RC = 512
FILTER_PITCH = FFT_N2 + 8


def _params(*sem, vmem_limit=VMEM_LIMIT):
    return pltpu.CompilerParams(dimension_semantics=sem, vmem_limit_bytes=vmem_limit)


def _const_spec(shape, index=None):
    index = (0,) * len(shape) if index is None else index
    return pl.BlockSpec(shape, lambda *_: index, pipeline_mode=pl.Buffered(1))


def _sigmoid(x):
    return 0.5 * (jnp.tanh(0.5 * x) + 1.0)


def _cast_specs(weights, grid_shape):
    nsteps = math.prod(grid_shape)

    def step(*g):
        s = g[0]
        for gi, n in zip(g[1:], grid_shape[1:]):
            s = s * n + gi
        return (s, 0)

    specs, shapes = [], []
    for w in weights:
        rows, cols = w.shape
        assert rows % (16 * nsteps) == 0, (w.shape, nsteps)
        specs.append(pl.BlockSpec((rows // nsteps, cols), step))
        shapes.append(jax.ShapeDtypeStruct(w.shape, BF16))
    return specs, shapes


def _cast_blocks(in_refs, out_refs, col_maps=None):
    for n, (i_ref, o_ref) in enumerate(zip(in_refs, out_refs)):
        moves = col_maps[n] if col_maps else None
        if moves is None:
            o_ref[...] = i_ref[...].astype(BF16)
        else:
            for dst, src, width in moves:
                o_ref[:, dst:dst + width] = i_ref[:, src:src + width].astype(BF16)


def _w_in_column_moves(tn):
    moves = []
    for j in range(CONF_WIDTH // tn):
        moves.append((2 * j * tn, j * tn, tn))
        moves.append(((2 * j + 1) * tn, CONF_WIDTH + j * tn, tn))
    moves.append((2 * CONF_WIDTH, 2 * CONF_WIDTH, IN_COLS - 2 * CONF_WIDTH))
    return moves


def _rms(x, g):
    ms = jnp.mean(x * x, axis=-1, keepdims=True)
    return x * lax.rsqrt(ms + EPS) * g


@functools.lru_cache(maxsize=None)
def _dft_constants():
    k1 = np.arange(FFT_N1)
    f1 = np.exp(-2j * np.pi * np.outer(k1, k1) / FFT_N1)
    fh = f1[:, :FFT_NZ1]
    fa = np.block([[fh.real, -fh.imag], [fh.imag, fh.real]])
    perm = np.concatenate([np.arange(FFT_NZ1), FFT_N1 - 1 - np.arange(FFT_N1 - FFT_NZ1)])
    fp = f1[:, perm]
    fak = np.concatenate([fp.real, fp.imag], axis=0)
    n2 = np.arange(FFT_N2)
    kk = k1[:, None, None] + FFT_N1 * n2[None, :, None]
    ang = (n2[None, None, :] * kk) % N_FFT
    tc = np.exp(-2j * np.pi * ang / N_FFT)
    t = np.concatenate([np.concatenate([tc.real, -tc.imag], 2),
                        np.concatenate([tc.imag, tc.real], 2)], 1)
    ti = np.transpose(t, (0, 2, 1))
    g = np.conj(f1[:, :FFT_NZ1]).T / N_FFT
    fd = np.block([[g.real, -g.imag], [g.imag, g.real]])
    return tuple(jnp.asarray(a, dtype=BF16) for a in (fa, fak, t, ti, fd))


@functools.lru_cache(maxsize=None)
def _decay_rates():
    max_decay = math.log(HYENA_TARGET) / HYENA_FAST_DECAY
    min_decay = math.log(HYENA_TARGET) / HYENA_SLOW_DECAY
    return jnp.asarray(np.abs(np.linspace(min_decay, max_decay, HYENA_WIDTH, dtype=np.float32))[None, :])


def _norm_matmul_kernel(x_ref, g_ref, w_ref, o_ref, h_ref):
    @pl.when(pl.program_id(1) == 0)
    def _():
        h_ref[...] = _rms(x_ref[...], g_ref[...]).astype(BF16)

    o_ref[...] = jnp.dot(h_ref[...], w_ref[...].astype(BF16), preferred_element_type=F32).astype(o_ref.dtype)


def _norm_matmul(x, g, w, tm, tn, out_dtype):
    m, d = x.shape
    n = w.shape[1]
    return pl.pallas_call(
        _norm_matmul_kernel,
        grid=(m // tm, n // tn),
        in_specs=[pl.BlockSpec((tm, d), lambda i, j: (i, 0)),
                  pl.BlockSpec((1, d), lambda i, j: (0, 0)),
                  pl.BlockSpec((d, tn), lambda i, j: (0, j))],
        out_specs=pl.BlockSpec((tm, tn), lambda i, j: (i, j)),
        out_shape=jax.ShapeDtypeStruct((m, n), out_dtype),
        scratch_shapes=[pltpu.VMEM((tm, d), BF16)],
        compiler_params=_params("parallel", "arbitrary"),
        name="norm_matmul",
    )(x, g.reshape(1, d), w)


CIN_TM = 1024
CIN_TN = 512


def _conf_in_kernel(x_ref, g_ref, w_ref, o_ref, h_ref):
    @pl.when(pl.program_id(1) == 0)
    def _():
        h_ref[...] = _rms(x_ref[...], g_ref[...]).astype(BF16)

    ag = jnp.dot(h_ref[...], w_ref[...], preferred_element_type=F32)
    o_ref[...] = (ag[:, :CIN_TN] * _sigmoid(ag[:, CIN_TN:])).astype(o_ref.dtype)


def _conf_in(x, g, w_in):
    m, d = x.shape
    tm, tn = CIN_TM, CIN_TN
    ncol = CONF_WIDTH // tn
    return pl.pallas_call(
        _conf_in_kernel,
        grid=(m // tm, ncol),
        in_specs=[pl.BlockSpec((tm, d), lambda i, j: (i, 0)),
                  pl.BlockSpec((1, d), lambda i, j: (0, 0)),
                  pl.BlockSpec((d, 2 * tn), lambda i, j: (0, j))],
        out_specs=[pl.BlockSpec((tm, tn), lambda i, j: (i, j)),
                   pl.BlockSpec((tm, d), lambda i, j: (i, 0))],
        out_shape=[jax.ShapeDtypeStruct((m, CONF_WIDTH), BF16), jax.ShapeDtypeStruct((m, d), BF16)],
        compiler_params=_params("parallel", "arbitrary"),
        name="conf_in",
    )(x, g.reshape(1, d), w_in)


CONF_T = 512
CONF_HALO = 16
CONF_RC = 64


def _conformer_kernel(u_in, up_in, un_in, w_ref, b_ref, lg_ref, lb_ref, o_ref, u_ref, y_ref):
    i = pl.program_id(1)
    nt = pl.num_programs(1)
    t = CONF_T
    h = CONF_HALO
    prev_ok = (i > 0).astype(F32)
    next_ok = (i < nt - 1).astype(F32)
    nc = CONF_WIDTH // LANES
    for c in range(nc):
        cs = slice(c * LANES, (c + 1) * LANES)
        u_ref[c, 0:h, :] = up_in[0, :, cs].astype(F32) * prev_ok
        u_ref[c, h:h + t, :] = u_in[0, :, cs].astype(F32)
        u_ref[c, h + t:h + t + h, :] = un_in[0, :, cs].astype(F32) * next_ok

    pad = CONF_KERNEL // 2

    def conv_chunk(c, carry):
        w = w_ref[c]
        bias = b_ref[c]
        for r in range(t // CONF_RC):
            r0 = r * CONF_RC
            acc = jnp.zeros((CONF_RC, LANES), F32) + bias
            for j in range(CONF_KERNEL):
                acc = acc + w[j:j + 1, :] * u_ref[c, r0 + h - pad + j:r0 + h - pad + j + CONF_RC, :]
            y_ref[c, r0:r0 + CONF_RC, :] = acc
        return carry

    lax.fori_loop(0, nc, conv_chunk, 0)

    def ln_chunk(r, carry):
        r0 = pl.multiple_of(r * CONF_RC, CONF_RC)
        ys = [y_ref[c, pl.ds(r0, CONF_RC), :] for c in range(nc)]
        tot = ys[0]
        for y in ys[1:]:
            tot = tot + y
        mu = jnp.sum(tot, axis=-1, keepdims=True) * (1.0 / CONF_WIDTH)
        yc = [y - mu for y in ys]
        sq = yc[0] * yc[0]
        for y in yc[1:]:
            sq = sq + y * y
        inv = lax.rsqrt(jnp.sum(sq, axis=-1, keepdims=True) * (1.0 / CONF_WIDTH) + EPS)
        for c in range(nc):
            cs = slice(c * LANES, (c + 1) * LANES)
            yn = yc[c] * inv * lg_ref[:, cs] + lb_ref[:, cs]
            o_ref[0, pl.ds(r0, CONF_RC), cs] = (yn * _sigmoid(yn)).astype(o_ref.dtype)
        return carry

    lax.fori_loop(0, t // CONF_RC, ln_chunk, 0, unroll=2)


def _conformer(u, dw_w, dw_b, ln_g, ln_b):
    b, l, c = u.shape
    t, h = CONF_T, CONF_HALO
    nt = l // t
    hb = t // h
    last_hb = l // h - 1
    vec = pl.BlockSpec((1, c), lambda bi, i: (0, 0))
    nc = c // LANES
    chunked = lambda rows: pl.BlockSpec((nc, rows, LANES), lambda bi, i: (0, 0, 0))
    w3 = dw_w.reshape(CONF_KERNEL, nc, LANES).transpose(1, 0, 2)
    return pl.pallas_call(
        _conformer_kernel,
        grid=(b, nt),
        in_specs=[pl.BlockSpec((1, t, c), lambda bi, i: (bi, i, 0)),
                  pl.BlockSpec((1, h, c), lambda bi, i: (bi, jnp.maximum(i * hb - 1, 0), 0)),
                  pl.BlockSpec((1, h, c), lambda bi, i: (bi, jnp.minimum((i + 1) * hb, last_hb), 0)),
                  chunked(CONF_KERNEL), chunked(1), vec, vec],
        out_specs=pl.BlockSpec((1, t, c), lambda bi, i: (bi, i, 0)),
        out_shape=jax.ShapeDtypeStruct((b, l, c), BF16),
        scratch_shapes=[pltpu.VMEM((nc, t + 2 * h, LANES), F32), pltpu.VMEM((nc, t, LANES), F32)],
        compiler_params=_params("parallel", "parallel"),
        name="conformer",
    )(u, u, u, w3, dw_b.reshape(nc, 1, LANES), ln_g.reshape(1, c), ln_b.reshape(1, c))


HIN_TM = 1024
HIN_TN = 1024
HIN_HALO = 16
HIN_N1 = HIN_TM // FFT_N2
HIN_PITCH = FFT_N2 + 8


def _hyena_in_kernel(hc_ref, hp_ref, hn_ref, w_ref, cw_ref, cb_ref, o_ref, h_ref, s_ref):
    i = pl.program_id(1)
    j = pl.program_id(2)
    nt = pl.num_programs(1)
    tm, hl = HIN_TM, HIN_HALO

    @pl.when(j == 0)
    def _():
        h_ref[0:hl, :] = jnp.where(i > 0, hp_ref[0], jnp.zeros_like(hp_ref[0]))
        h_ref[hl:hl + tm, :] = hc_ref[0]
        h_ref[hl + tm:hl + tm + hl, :] = jnp.where(i < nt - 1, hn_ref[0], jnp.zeros_like(hn_ref[0]))

    e = jnp.dot(h_ref[...], w_ref[...], preferred_element_type=F32)
    nrows = tm + 2 * hl
    conv = (cw_ref[0:1, :] * pltpu.roll(e, 1, axis=0) + cw_ref[1:2, :] * e
            + cw_ref[2:3, :] * pltpu.roll(e, nrows - 1, axis=0) + cb_ref[...])
    nlc = HIN_TN // LANES
    for r in range(HIN_N1):
        r0 = hl + r * FFT_N2
        for lc in range(nlc):
            s_ref[lc, r * HIN_PITCH:r * HIN_PITCH + FFT_N2, :] = conv[r0:r0 + FFT_N2, lc * LANES:(lc + 1) * LANES]

    def gather(n2, carry):
        for lc in range(nlc):
            rows = s_ref[lc, pl.ds(n2, HIN_N1, stride=HIN_PITCH), :]
            o_ref[0, n2, :, lc * LANES:(lc + 1) * LANES] = rows.astype(o_ref.dtype)
        return carry

    lax.fori_loop(0, FFT_N2, gather, 0, unroll=8)


def _hyena_in(hn, w_in, cw, cb):
    b, l, d = hn.shape
    tm, tn, hl = HIN_TM, HIN_TN, HIN_HALO
    ncol = (HYENA_ORDER + 1) * HYENA_WIDTH // tn
    col0 = 2 * CONF_WIDTH // tn
    hb = tm // hl
    last_hb = l // hl - 1
    return pl.pallas_call(
        _hyena_in_kernel,
        grid=(b, l // tm, ncol),
        in_specs=[pl.BlockSpec((1, tm, d), lambda bi, i, j: (bi, i, 0)),
                  pl.BlockSpec((1, hl, d), lambda bi, i, j: (bi, jnp.maximum(i * hb - 1, 0), 0)),
                  pl.BlockSpec((1, hl, d), lambda bi, i, j: (bi, jnp.minimum((i + 1) * hb, last_hb), 0)),
                  pl.BlockSpec((d, tn), lambda bi, i, j: (0, col0 + j)),
                  pl.BlockSpec((SHORT_KERNEL, tn), lambda bi, i, j: (0, j)),
                  pl.BlockSpec((1, tn), lambda bi, i, j: (0, j))],
        out_specs=pl.BlockSpec((1, FFT_N2, HIN_N1, tn), lambda bi, i, j: (bi, 0, i, j)),
        out_shape=jax.ShapeDtypeStruct((b, FFT_N2, FFT_NZ1, ncol * tn), BF16),
        scratch_shapes=[pltpu.VMEM((tm + 2 * hl, d), BF16),
                        pltpu.VMEM((tn // LANES, HIN_N1 * HIN_PITCH, LANES), F32)],
        compiler_params=_params("parallel", "parallel", "arbitrary"),
        name="hyena_in",
    )(hn, hn, hn, w_in, cw, cb.reshape(1, -1))


MLP_HW = LANES // 2
MLP_W1_ROWS = 40
MLP_WI_ROW0 = MLP_W1_ROWS
MLP_VEC_ROW0 = MLP_WI_ROW0 + HYENA_INNER_MLPS * HYENA_FILTER_ORDER
MLP_ROWS = MLP_VEC_ROW0 + 8


def _filter_mlp_kernel(z_ref, p_ref, o_ref):
    hi = lax.Precision.HIGHEST

    def block_diag(rows0, nrows):
        w = p_ref[rows0:rows0 + nrows, :]
        lo = lax.broadcasted_iota(jnp.int32, w.shape, 1) < MLP_HW
        zeros = jnp.zeros((MLP_HW - nrows, LANES), F32)
        parts = [jnp.where(lo, w, 0.0), zeros, jnp.where(lo, 0.0, w), zeros]
        return jnp.concatenate([p for p in parts if p.shape[0]], axis=0)

    vec = lambda r: p_ref[MLP_VEC_ROW0 + r:MLP_VEC_ROW0 + r + 1, :]
    fr = vec(1 + HYENA_INNER_MLPS)
    w1 = block_diag(0, MLP_W1_ROWS)
    h = jnp.sin(fr * (jnp.dot(z_ref[...], w1, precision=hi, preferred_element_type=F32) + vec(0)))
    for i in range(HYENA_INNER_MLPS):
        wi = block_diag(MLP_WI_ROW0 + i * HYENA_FILTER_ORDER, HYENA_FILTER_ORDER)
        h = jnp.sin(fr * (jnp.dot(h, wi, precision=hi, preferred_element_type=F32) + vec(1 + i)))
    o_ref[...] = h


@functools.lru_cache(maxsize=None)
def _position_features_packed():
    f32 = np.float32
    t = np.linspace(0.0, 1.0, SEQ, dtype=f32)[:, None]
    w = (f32(2.0 * math.pi) * np.arange(SEQ, dtype=f32)[:, None] / f32(SEQ)).astype(f32)
    bands = (HYENA_EMB - 1) // 2
    f = np.linspace(1e-4, bands - 1, bands, dtype=f32)[None, :]
    z = np.concatenate([t, np.cos(f * w), -np.sin(f * w)], axis=-1).astype(f32)
    zp = np.zeros((SEQ // 2, LANES), f32)
    zp[:, :HYENA_EMB] = z[:SEQ // 2]
    zp[:, MLP_HW:MLP_HW + HYENA_EMB] = z[SEQ // 2:]
    return jnp.asarray(zp)


def _filter_mlp(w1, b1, w_inner, b_inner, freq):
    both = lambda a: jnp.concatenate([a, a], axis=-1)
    params = jnp.concatenate(
        [both(w1), jnp.zeros((MLP_W1_ROWS - HYENA_EMB, LANES), F32)]
        + [both(w_inner[i]) for i in range(HYENA_INNER_MLPS)]
        + [both(b1)[None], both(b_inner), both(freq)[None],
           jnp.zeros((MLP_ROWS - MLP_VEC_ROW0 - 2 - HYENA_INNER_MLPS, LANES), F32)], axis=0)
    return pl.pallas_call(
        _filter_mlp_kernel,
        out_shape=jax.ShapeDtypeStruct((SEQ // 2, LANES), F32),
        compiler_params=pltpu.CompilerParams(vmem_limit_bytes=VMEM_LIMIT),
        name="filter_mlp",
    )(_position_features_packed(), params)


def _stage_b_rhs(a_ref, k1):
    ar = a_ref[pl.ds(k1, FFT_N2, stride=FFT_PITCH), :]
    ai = a_ref[pl.ds(FFT_N1 + k1, FFT_N2, stride=FFT_PITCH), :]
    return jnp.concatenate([ar, ai], axis=0).astype(BF16)


def _slab_rows(n2):
    return pl.ds(pl.multiple_of(n2 * FFT_PITCH, 8), 2 * FFT_N1)


def _filter_spectrum_kernel(h_ref, wf_ref, wb_ref, dl_ref, fak_ref, t_ref, *refs, n_cast, col_maps):
    cast_in, o_ref, cast_out = refs[:n_cast], refs[n_cast], refs[n_cast + 1:2 * n_cast + 1]
    kt_ref, a_ref = refs[2 * n_cast + 1:]
    _cast_blocks(cast_in, cast_out, col_maps)
    rc = FILTER_RC
    half_chunks = SEQ // 2 // rc
    w = jnp.concatenate([wf_ref[...], wb_ref[...]], axis=1)
    zeros = jnp.zeros((MLP_HW, 2 * LANES), F32)
    ws = [jnp.concatenate([w, zeros], axis=0).astype(BF16), jnp.concatenate([zeros, w], axis=0).astype(BF16)]
    dl = jnp.concatenate([dl_ref[...], dl_ref[...]], axis=1)

    def gen(r, ss, p):
        r0 = pl.multiple_of(r * rc, rc)
        rows = lax.broadcasted_iota(jnp.int32, (rc, 2 * LANES), 0) + r0
        lanes = lax.broadcasted_iota(jnp.int32, (rc, 2 * LANES), 1)
        decay = jnp.exp(-(rows.astype(F32) * (1.0 / (SEQ - 1))) * dl)
        h = h_ref[pl.ds(pl.multiple_of(r0 - p * (SEQ // 2), rc), rc), :]
        k = jnp.dot(h.astype(BF16), ws[p], preferred_element_type=F32)
        k = k * decay
        k = jnp.where((rows >= SEQ - 1) & (lanes >= LANES), 0.0, k)
        for n in range(rc // FFT_N2):
            dst = pl.ds(pl.multiple_of((r * (rc // FFT_N2) + n) * FILTER_PITCH, 8), FFT_N2)
            kt_ref[0, dst, :] = k[n * FFT_N2:(n + 1) * FFT_N2, :LANES]
            kt_ref[1, dst, :] = k[n * FFT_N2:(n + 1) * FFT_N2, LANES:]
        return ss + jnp.sum(k * k, axis=0, keepdims=True)

    ss = jnp.zeros((1, 2 * LANES), F32)
    for p in range(2):
        ss = lax.fori_loop(p * half_chunks, (p + 1) * half_chunks, functools.partial(gen, p=p), ss)
    scale = lax.rsqrt(ss[:, :LANES] + ss[:, LANES:] + EPS)

    def filter_slab(n2):
        fwd = kt_ref[0, pl.ds(n2, FFT_NZ1, stride=FILTER_PITCH), :]
        bwd = kt_ref[1, pl.ds(FFT_N2 - 1 - n2, FFT_NZ1, stride=FILTER_PITCH), :]
        return (jnp.concatenate([fwd, bwd], axis=0) * scale).astype(BF16)

    def stage_a(p, carry):
        n2 = 2 * p
        rhs = jnp.concatenate([filter_slab(n2), filter_slab(n2 + 1)], axis=1)
        a = jnp.dot(fak_ref[...], rhs, preferred_element_type=F32)
        a_ref[_slab_rows(n2), :] = a[:, :LANES]
        a_ref[_slab_rows(n2 + 1), :] = a[:, LANES:]
        return carry

    lax.fori_loop(0, FFT_N2 // 2, stage_a, 0, unroll=FFT_UNROLL // 2)

    def stage_b(k1, carry):
        b = jnp.dot(t_ref[k1], _stage_b_rhs(a_ref, k1), preferred_element_type=F32)
        o_ref[0, k1] = b.astype(o_ref.dtype)
        return carry

    lax.fori_loop(0, FFT_N1, stage_b, 0, unroll=FFT_UNROLL)


def _filter_spectrum(hmlp, w_filt, fak, t, cast_weights, col_maps):
    assert HYENA_FILTER_ORDER == MLP_HW
    ncb = HYENA_WIDTH // LANES
    grid = (HYENA_ORDER, ncb)
    cast_specs, cast_shapes = _cast_specs(cast_weights, grid)
    w_spec = lambda direction: pl.BlockSpec((HYENA_FILTER_ORDER, LANES),
                                            lambda o, c: (0, (o * 2 + direction) * ncb + c))
    outs = pl.pallas_call(
        functools.partial(_filter_spectrum_kernel, n_cast=len(cast_weights), col_maps=col_maps),
        grid=grid,
        in_specs=[_const_spec((SEQ // 2, LANES)), w_spec(0), w_spec(1),
                  pl.BlockSpec((1, LANES), lambda o, c: (0, c)),
                  _const_spec((2 * FFT_N1, FFT_N1)),
                  _const_spec((FFT_N1, 2 * FFT_N2, 2 * FFT_N2))] + cast_specs,
        out_specs=[pl.BlockSpec((1, FFT_N1, 2 * FFT_N2, LANES), lambda o, c: (o, 0, 0, c))] + cast_specs,
        out_shape=[jax.ShapeDtypeStruct((HYENA_ORDER, FFT_N1, 2 * FFT_N2, HYENA_WIDTH), F32)] + cast_shapes,
        scratch_shapes=[pltpu.VMEM((2, FFT_NZ1 * FILTER_PITCH, LANES), F32),
                        pltpu.VMEM((FFT_N2 * FFT_PITCH, LANES), F32)],
        compiler_params=_params("arbitrary", "arbitrary"),
        name="filter_spectrum",
    )(hmlp, w_filt, w_filt, _decay_rates(), fak, t, *cast_weights)
    return outs[0], outs[1:]


def _hyena_conv_kernel(u_ref, xm_ref, kf_ref, sk_ref, fa_ref, t_ref, ti_ref, fd_ref, o_ref, a_ref, y_ref, *,
                       slab_order_out):
    def stage_a(p, carry):
        n2 = 2 * p
        rhs = jnp.concatenate(
            [jnp.concatenate([u_ref[0, n], u_ref[1, n]], axis=0) for n in (n2, n2 + 1)], axis=1).astype(BF16)
        a = jnp.dot(fa_ref[...], rhs, preferred_element_type=F32)
        a_ref[_slab_rows(n2), :] = a[:, :LANES]
        a_ref[_slab_rows(n2 + 1), :] = a[:, LANES:]
        return carry

    lax.fori_loop(0, FFT_N2 // 2, stage_a, 0, unroll=FFT_UNROLL // 2)

    def stage_b(k1, carry):
        b = jnp.dot(t_ref[k1], _stage_b_rhs(a_ref, k1), preferred_element_type=F32)
        kf = kf_ref[0, k1]
        br, bi = b[:FFT_N2], b[FFT_N2:]
        kr, ki = kf[:FFT_N2], kf[FFT_N2:]
        y_ref[k1] = jnp.concatenate([br * kr - bi * ki, br * ki + bi * kr], axis=0).astype(BF16)
        return carry

    lax.fori_loop(0, FFT_N1, stage_b, 0, unroll=FFT_UNROLL)

    def stage_c(k1, carry):
        c = jnp.dot(ti_ref[k1], y_ref[k1], preferred_element_type=F32)
        a_ref[pl.ds(k1, FFT_N2, stride=FFT_PITCH), :] = c[:FFT_N2]
        a_ref[pl.ds(FFT_N1 + k1, FFT_N2, stride=FFT_PITCH), :] = c[FFT_N2:]
        return carry

    lax.fori_loop(0, FFT_N1, stage_c, 0, unroll=FFT_UNROLL)

    sk = sk_ref[0]

    def stage_d(p, carry):
        n2 = 2 * p
        cb = jnp.concatenate([a_ref[_slab_rows(n2), :], a_ref[_slab_rows(n2 + 1), :]], axis=1).astype(BF16)
        y = jnp.dot(fd_ref[...], cb, preferred_element_type=F32)
        for s in range(2):
            for bi in range(2):
                yb = y[bi * FFT_NZ1:(bi + 1) * FFT_NZ1, s * LANES:(s + 1) * LANES]
                z = xm_ref[bi, n2 + s].astype(F32) * (yb + sk * u_ref[bi, n2 + s].astype(F32))
                if slab_order_out:
                    o_ref[bi, n2 + s] = z.astype(o_ref.dtype)
                else:
                    o_ref[bi, pl.ds(n2 + s, FFT_NZ1, stride=FFT_N2), :] = z
        return carry

    lax.fori_loop(0, FFT_N2 // 2, stage_d, 0, unroll=FFT_UNROLL // 2)


def _hyena_conv(u, u_col0, xm, xm_col0, kf, order, skip, consts, slab_order_out):
    fa, _, t, ti, fd = consts
    ncb = HYENA_WIDTH // LANES
    slab_block = (BATCH, FFT_N2, FFT_NZ1, LANES)
    if slab_order_out:
        out_spec = pl.BlockSpec(slab_block, lambda c: (0, 0, 0, c))
        out_shape = jax.ShapeDtypeStruct((BATCH, FFT_N2, FFT_NZ1, HYENA_WIDTH), BF16)
    else:
        out_spec = pl.BlockSpec((BATCH, SEQ, LANES), lambda c: (0, 0, c))
        out_shape = jax.ShapeDtypeStruct((BATCH, SEQ, HYENA_WIDTH), F32)
    return pl.pallas_call(
        functools.partial(_hyena_conv_kernel, slab_order_out=slab_order_out),
        grid=(ncb,),
        in_specs=[pl.BlockSpec(slab_block, lambda c: (0, 0, 0, u_col0 + c)),
                  pl.BlockSpec(slab_block, lambda c: (0, 0, 0, xm_col0 + c)),
                  pl.BlockSpec((1, FFT_N1, 2 * FFT_N2, LANES), lambda c: (order, 0, 0, c)),
                  pl.BlockSpec((1, 1, LANES), lambda c: (order, 0, c)),
                  _const_spec(fa.shape), _const_spec(t.shape), _const_spec(ti.shape), _const_spec(fd.shape)],
        out_specs=out_spec,
        out_shape=out_shape,
        scratch_shapes=[pltpu.VMEM((FFT_N2 * FFT_PITCH, LANES), F32),
                        pltpu.VMEM((FFT_N1, 2 * FFT_N2, LANES), BF16)],
        compiler_params=_params("parallel"),
        name="hyena_conv",
    )(u, xm, kf, skip.reshape(HYENA_ORDER, 1, HYENA_WIDTH), fa, t, ti, fd)


MA_TM = 256
MA_VMEM_LIMIT = 60 * 1024 * 1024


def _mix_attn_kernel(x_ref, yc_ref, yh_ref, wa_ref, wb_ref, g_ref, wq_ref, k_ref, v_ref, wo_ref, *refs, n_cast):
    cast_in, o_ref, cast_out = refs[:n_cast], refs[n_cast], refs[n_cast + 1:2 * n_cast + 1]
    x1_ref, q_ref, ctx_ref = refs[2 * n_cast + 1:]
    _cast_blocks(cast_in, cast_out)
    mix = jnp.dot(yc_ref[0], wa_ref[...], preferred_element_type=F32)
    mix = mix + jnp.dot(yh_ref[0].astype(BF16), wb_ref[...], preferred_element_type=F32)
    x1 = x_ref[0] + mix
    x1_ref[...] = x1
    q_ref[...] = jnp.dot(_rms(x1, g_ref[...]).astype(BF16), wq_ref[...],
                         preferred_element_type=F32).astype(BF16)
    hd = XATTN_HEAD_DIM
    for h in range(XATTN_HEADS):
        hs = slice(h * hd, (h + 1) * hd)
        s = lax.dot_general(q_ref[:, hs], k_ref[0, :, hs], (((1,), (1,)), ((), ())),
                            preferred_element_type=F32) * (hd ** -0.5)
        s = s - jnp.max(s, axis=-1, keepdims=True)
        e = jnp.exp(s)
        pr = e / jnp.sum(e, axis=-1, keepdims=True)
        ctx_ref[:, hs] = jnp.dot(pr.astype(BF16), v_ref[0, :, hs], preferred_element_type=F32).astype(BF16)
    o_ref[0] = x1_ref[...] + jnp.dot(ctx_ref[...], wo_ref[...], preferred_element_type=F32)


def _mix_attn(x, yc, yh, w_out, g, wq, k, v, wo, cast_weights):
    b, l, d = x.shape
    tm = MA_TM
    kc = yc.shape[-1]
    grid = (b, l // tm)
    tok = lambda width: pl.BlockSpec((1, tm, width), lambda bi, i: (bi, i, 0))
    mem = pl.BlockSpec((1, N_MEM, d), lambda bi, i: (bi, 0, 0))
    cast_specs, cast_shapes = _cast_specs(cast_weights, grid)
    outs = pl.pallas_call(
        functools.partial(_mix_attn_kernel, n_cast=len(cast_weights)),
        grid=grid,
        in_specs=[tok(d), tok(kc), tok(kc),
                  _const_spec((kc, d), (0, 0)), _const_spec((kc, d), (1, 0)),
                  pl.BlockSpec((1, d), lambda bi, i: (0, 0)),
                  _const_spec((d, d)), mem, mem, _const_spec((d, d))] + cast_specs,
        out_specs=[tok(d)] + cast_specs,
        out_shape=[jax.ShapeDtypeStruct((b, l, d), F32)] + cast_shapes,
        scratch_shapes=[pltpu.VMEM((tm, d), F32), pltpu.VMEM((tm, d), BF16), pltpu.VMEM((tm, d), BF16)],
        compiler_params=_params("arbitrary", "arbitrary", vmem_limit=MA_VMEM_LIMIT),
        name="mix_attn",
    )(x, yc, yh, w_out, w_out, g.reshape(1, d), wq, k, v, wo, *cast_weights)
    return outs[0], outs[1:]


FFN_TM = 1024
FFN_FC = 512
FFN_VMEM_LIMIT = 62 * 1024 * 1024
FFN_HALO = 16


def _ffn_kernel(x_ref, xp_ref, xn_ref, g_ref, wg_ref, wv_ref, cw_ref, cb_ref, wd_ref, gf_ref,
                o_ref, h_ref, ge_ref):
    i = pl.program_id(1)
    j = pl.program_id(2)
    nt = pl.num_programs(1)
    nf = pl.num_programs(2)
    tm, hl = FFN_TM, FFN_HALO

    @pl.when(j == 0)
    def _():
        g = g_ref[...]
        h_ref[0:hl, :] = (_rms(xp_ref[0], g) * (i > 0).astype(F32)).astype(BF16)
        h_ref[hl:hl + tm, :] = _rms(x_ref[0], g).astype(BF16)
        h_ref[hl + tm:hl + tm + hl, :] = (_rms(xn_ref[0], g) * (i < nt - 1).astype(F32)).astype(BF16)
        o_ref[...] = jnp.zeros_like(o_ref)

    ge_ref[...] = jnp.dot(h_ref[...], wg_ref[...], preferred_element_type=F32)
    val = jnp.dot(h_ref[pl.ds(hl, tm), :], wv_ref[...], preferred_element_type=F32)
    gate = (cw_ref[0:1, :] * ge_ref[pl.ds(hl - 1, tm), :]
            + cw_ref[1:2, :] * ge_ref[pl.ds(hl, tm), :]
            + cw_ref[2:3, :] * ge_ref[pl.ds(hl + 1, tm), :]
            + cb_ref[...])
    act = (gate * _sigmoid(gate) * val).astype(BF16)
    o_ref[0] += jnp.dot(act, wd_ref[...], preferred_element_type=F32)

    @pl.when(j == nf - 1)
    def _():
        o_ref[0] = _rms(x_ref[0] + o_ref[0], gf_ref[...])


def _ffn(x, g_ffn, w_up, dw_w, dw_b, w_down, g_final):
    b, l, d = x.shape
    tm, fc, hl = FFN_TM, FFN_FC, FFN_HALO
    nf = D_FF // fc
    hb = tm // hl
    last_hb = l // hl - 1
    return pl.pallas_call(
        _ffn_kernel,
        grid=(b, l // tm, nf),
        in_specs=[pl.BlockSpec((1, tm, d), lambda bi, i, j: (bi, i, 0)),
                  pl.BlockSpec((1, hl, d), lambda bi, i, j: (bi, jnp.maximum(i * hb - 1, 0), 0)),
                  pl.BlockSpec((1, hl, d), lambda bi, i, j: (bi, jnp.minimum((i + 1) * hb, last_hb), 0)),
                  pl.BlockSpec((1, d), lambda bi, i, j: (0, 0)),
                  pl.BlockSpec((d, fc), lambda bi, i, j: (0, j)),
                  pl.BlockSpec((d, fc), lambda bi, i, j: (0, nf + j)),
                  pl.BlockSpec((3, fc), lambda bi, i, j: (0, j)),
                  pl.BlockSpec((1, fc), lambda bi, i, j: (0, j)),
                  pl.BlockSpec((fc, d), lambda bi, i, j: (j, 0)),
                  pl.BlockSpec((1, d), lambda bi, i, j: (0, 0))],
        out_specs=pl.BlockSpec((1, tm, d), lambda bi, i, j: (bi, i, 0)),
        out_shape=jax.ShapeDtypeStruct((b, l, d), F32),
        scratch_shapes=[pltpu.VMEM((tm + 2 * hl, d), BF16),
                        pltpu.VMEM((tm + 2 * hl, fc), F32)],
        compiler_params=_params("parallel", "parallel", "arbitrary", vmem_limit=FFN_VMEM_LIMIT),
        name="ffn",
    )(x, x, x, g_ffn.reshape(1, d), w_up, w_up, dw_w, dw_b.reshape(1, -1), w_down, g_final.reshape(1, d))


def kernel(x, mem, g_mix, w_in, conf_dw_w, conf_dw_b, conf_ln_g, conf_ln_b, hyena_short_w, hyena_short_b, hyena_w1, hyena_b1, hyena_w_inner, hyena_b_inner, hyena_w_filt, hyena_sin_freq, hyena_skip, w_out, g_xattn, g_mem, w_q, w_k, w_v, w_o, g_ffn, w_ffn_up, ffn_dw_w, ffn_dw_b, w_ffn_down, g_final):
    b, l, d = x.shape
    consts = _dft_constants()

    hmlp = _filter_mlp(hyena_w1[0], hyena_b1[0], hyena_w_inner[0], hyena_b_inner[0], hyena_sin_freq[0])
    kf, (w_in_b, w_out_b, w_q_b, w_o_b) = _filter_spectrum(
        hmlp, hyena_w_filt[0], consts[1], consts[2], [w_in[0], w_out[0], w_q[0], w_o[0]],
        [_w_in_column_moves(CIN_TN), None, None, None])

    u_conf, hn = _conf_in(x.reshape(b * l, d), g_mix[0], w_in_b)
    y_conf = _conformer(u_conf.reshape(b, l, CONF_WIDTH), conf_dw_w[0], conf_dw_b[0], conf_ln_g[0],
                        conf_ln_b[0])

    uc = _hyena_in(hn.reshape(b, l, d), w_in_b, hyena_short_w[0], hyena_short_b[0])
    ncb = HYENA_WIDTH // LANES
    z1 = _hyena_conv(uc, 0, uc, ncb, kf, 0, hyena_skip[0], consts, slab_order_out=True)
    z2 = _hyena_conv(z1, 0, uc, 2 * ncb, kf, 1, hyena_skip[0], consts, slab_order_out=False)

    memf = mem.reshape(b * N_MEM, d)
    k = _norm_matmul(memf, g_mem[0], w_k[0], b * N_MEM, 1024, BF16).reshape(b, N_MEM, d)
    v = _norm_matmul(memf, g_mem[0], w_v[0], b * N_MEM, 1024, BF16).reshape(b, N_MEM, d)
    x2, (w_up_b, w_down_b) = _mix_attn(x, y_conf, z2, w_out_b, g_xattn[0], w_q_b, k, v, w_o_b,
                                       [w_ffn_up[0], w_ffn_down[0]])

    return _ffn(x2, g_ffn[0], w_up_b, ffn_dw_w[0], ffn_dw_b[0], w_down_b, g_final)
```

```python
import functools
import math

import numpy as np
import jax
import jax.numpy as jnp
from jax import lax
from jax.experimental import pallas as pl
from jax.experimental.pallas import tpu as pltpu

F32 = jnp.float32
BF16 = jnp.bfloat16

D_MODEL = 2048
BATCH = 2
SEQ = 4096
CONF_WIDTH = 1024
HYENA_WIDTH = 1024
CONF_KERNEL = 31
SHORT_KERNEL = 3
HYENA_ORDER = 2
HYENA_EMB = 33
HYENA_FILTER_ORDER = 64
HYENA_INNER_MLPS = 2
HYENA_FAST_DECAY = 0.3
HYENA_SLOW_DECAY = 1.5
HYENA_TARGET = 1e-2
N_MEM = 256
XATTN_HEADS = 4
XATTN_HEAD_DIM = D_MODEL // XATTN_HEADS
D_FF = 5632
EPS = 1e-6
IN_COLS = 2 * CONF_WIDTH + (HYENA_ORDER + 1) * HYENA_WIDTH

LANES = 128
VMEM_LIMIT = 56 * 1024 * 1024

N_FFT = 2 * SEQ
FFT_N1 = 128
FFT_N2 = 64
FFT_NZ1 = SEQ // FFT_N2
FFT_PITCH = 2 * FFT_N1 + 8
FFT_UNROLL = 16
FILTER_RC = 512
FILTER_PITCH = FFT_N2 + 8
CONV_VMEM_LIMIT = 60 * 1024 * 1024


def _params(*sem, vmem_limit=VMEM_LIMIT):
    return pltpu.CompilerParams(dimension_semantics=sem, vmem_limit_bytes=vmem_limit)


def _const_spec(shape, index=None):
    index = (0,) * len(shape) if index is None else index
    return pl.BlockSpec(shape, lambda *_: index, pipeline_mode=pl.Buffered(1))


def _sigmoid(x):
    return 0.5 * (jnp.tanh(0.5 * x) + 1.0)


def _cast_specs(weights, grid_shape):
    nsteps = math.prod(grid_shape)

    def step(*g):
        s = g[0]
        for gi, n in zip(g[1:], grid_shape[1:]):
            s = s * n + gi
        return (s, 0)

    specs, shapes = [], []
    for w in weights:
        rows, cols = w.shape
        assert rows % (16 * nsteps) == 0, (w.shape, nsteps)
        specs.append(pl.BlockSpec((rows // nsteps, cols), step))
        shapes.append(jax.ShapeDtypeStruct(w.shape, BF16))
    return specs, shapes


def _cast_blocks(in_refs, out_refs, col_maps=None):
    for n, (i_ref, o_ref) in enumerate(zip(in_refs, out_refs)):
        moves = col_maps[n] if col_maps else None
        if moves is None:
            o_ref[...] = i_ref[...].astype(BF16)
        else:
            for dst, src, width in moves:
                o_ref[:, dst:dst + width] = i_ref[:, src:src + width].astype(BF16)


def _w_in_column_moves(tn):
    moves = []
    for j in range(CONF_WIDTH // tn):
        moves.append((2 * j * tn, j * tn, tn))
        moves.append(((2 * j + 1) * tn, CONF_WIDTH + j * tn, tn))
    moves.append((2 * CONF_WIDTH, 2 * CONF_WIDTH, IN_COLS - 2 * CONF_WIDTH))
    return moves


def _rms(x, g):
    ms = jnp.mean(x * x, axis=-1, keepdims=True)
    return x * lax.rsqrt(ms + EPS) * g


@functools.lru_cache(maxsize=None)
def _dft_constants():
    k1 = np.arange(FFT_N1)
    f1 = np.exp(-2j * np.pi * np.outer(k1, k1) / FFT_N1)
    fh = f1[:, :FFT_NZ1]
    fa = np.block([[fh.real, -fh.imag], [fh.imag, fh.real]])
    perm = np.concatenate([np.arange(FFT_NZ1), FFT_N1 - 1 - np.arange(FFT_N1 - FFT_NZ1)])
    fp = f1[:, perm]
    fak = np.concatenate([fp.real, fp.imag], axis=0)
    n2 = np.arange(FFT_N2)
    kk = k1[:, None, None] + FFT_N1 * n2[None, :, None]
    ang = (n2[None, None, :] * kk) % N_FFT
    tc = np.exp(-2j * np.pi * ang / N_FFT)
    t = np.concatenate([np.concatenate([tc.real, -tc.imag], 2),
                        np.concatenate([tc.imag, tc.real], 2)], 1)
    g = np.conj(f1[:, :FFT_NZ1]).T / N_FFT
    fd = np.block([[g.real, -g.imag], [g.imag, g.real]])
    ti = np.transpose(t, (0, 2, 1))
    return tuple(jnp.asarray(a, dtype=BF16) for a in (fa, fak, t, ti, fd))


@functools.lru_cache(maxsize=None)
def _decay_rates():
    max_decay = math.log(HYENA_TARGET) / HYENA_FAST_DECAY
    min_decay = math.log(HYENA_TARGET) / HYENA_SLOW_DECAY
    return jnp.asarray(np.abs(np.linspace(min_decay, max_decay, HYENA_WIDTH, dtype=np.float32))[None, :])


def _norm_matmul_kernel(x_ref, g_ref, w_ref, o_ref, h_ref):
    @pl.when(pl.program_id(1) == 0)
    def _():
        h_ref[...] = _rms(x_ref[...], g_ref[...]).astype(BF16)

    o_ref[...] = jnp.dot(h_ref[...], w_ref[...].astype(BF16), preferred_element_type=F32).astype(o_ref.dtype)


def _norm_matmul(x, g, w, tm, tn, out_dtype):
    m, d = x.shape
    n = w.shape[1]
    return pl.pallas_call(
        _norm_matmul_kernel,
        grid=(m // tm, n // tn),
        in_specs=[pl.BlockSpec((tm, d), lambda i, j: (i, 0)),
                  pl.BlockSpec((1, d), lambda i, j: (0, 0)),
                  pl.BlockSpec((d, tn), lambda i, j: (0, j))],
        out_specs=pl.BlockSpec((tm, tn), lambda i, j: (i, j)),
        out_shape=jax.ShapeDtypeStruct((m, n), out_dtype),
        scratch_shapes=[pltpu.VMEM((tm, d), BF16)],
        compiler_params=_params("parallel", "arbitrary"),
        name="norm_matmul",
    )(x, g.reshape(1, d), w)


CIN_TM = 1024
CIN_TN = 512


def _conf_in_kernel(x_ref, g_ref, w_ref, o_ref, h_ref):
    @pl.when(pl.program_id(1) == 0)
    def _():
        h_ref[...] = _rms(x_ref[...], g_ref[...]).astype(BF16)

    ag = jnp.dot(h_ref[...], w_ref[...], preferred_element_type=F32)
    o_ref[...] = (ag[:, :CIN_TN] * _sigmoid(ag[:, CIN_TN:])).astype(o_ref.dtype)


def _conf_in(x, g, w_in):
    m, d = x.shape
    tm, tn = CIN_TM, CIN_TN
    ncol = CONF_WIDTH // tn
    return pl.pallas_call(
        _conf_in_kernel,
        grid=(m // tm, ncol),
        in_specs=[pl.BlockSpec((tm, d), lambda i, j: (i, 0)),
                  pl.BlockSpec((1, d), lambda i, j: (0, 0)),
                  pl.BlockSpec((d, 2 * tn), lambda i, j: (0, j))],
        out_specs=[pl.BlockSpec((tm, tn), lambda i, j: (i, j)),
                   pl.BlockSpec((tm, d), lambda i, j: (i, 0))],
        out_shape=[jax.ShapeDtypeStruct((m, CONF_WIDTH), BF16), jax.ShapeDtypeStruct((m, d), BF16)],
        compiler_params=_params("parallel", "arbitrary"),
        name="conf_in",
    )(x, g.reshape(1, d), w_in)


CONF_T = 512
CONF_HALO = 16
CONF_RC = 64


def _conformer_kernel(u_in, up_in, un_in, w_ref, b_ref, lg_ref, lb_ref, o_ref, u_ref, y_ref):
    i = pl.program_id(1)
    nt = pl.num_programs(1)
    t = CONF_T
    h = CONF_HALO
    prev_ok = (i > 0).astype(F32)
    next_ok = (i < nt - 1).astype(F32)
    nc = CONF_WIDTH // LANES
    for c in range(nc):
        cs = slice(c * LANES, (c + 1) * LANES)
        u_ref[c, 0:h, :] = up_in[0, :, cs].astype(F32) * prev_ok
        u_ref[c, h:h + t, :] = u_in[0, :, cs].astype(F32)
        u_ref[c, h + t:h + t + h, :] = un_in[0, :, cs].astype(F32) * next_ok

    pad = CONF_KERNEL // 2

    def conv_chunk(c, carry):
        w = w_ref[c]
        bias = b_ref[c]
        for r in range(t // CONF_RC):
            r0 = r * CONF_RC
            acc = jnp.zeros((CONF_RC, LANES), F32) + bias
            for j in range(CONF_KERNEL):
                acc = acc + w[j:j + 1, :] * u_ref[c, r0 + h - pad + j:r0 + h - pad + j + CONF_RC, :]
            y_ref[c, r0:r0 + CONF_RC, :] = acc
        return carry

    lax.fori_loop(0, nc, conv_chunk, 0)

    def ln_chunk(r, carry):
        r0 = pl.multiple_of(r * CONF_RC, CONF_RC)
        ys = [y_ref[c, pl.ds(r0, CONF_RC), :] for c in range(nc)]
        tot = ys[0]
        for y in ys[1:]:
            tot = tot + y
        mu = jnp.sum(tot, axis=-1, keepdims=True) * (1.0 / CONF_WIDTH)
        yc = [y - mu for y in ys]
        sq = yc[0] * yc[0]
        for y in yc[1:]:
            sq = sq + y * y
        inv = lax.rsqrt(jnp.sum(sq, axis=-1, keepdims=True) * (1.0 / CONF_WIDTH) + EPS)
        for c in range(nc):
            cs = slice(c * LANES, (c + 1) * LANES)
            yn = yc[c] * inv * lg_ref[:, cs] + lb_ref[:, cs]
            o_ref[0, pl.ds(r0, CONF_RC), cs] = (yn * _sigmoid(yn)).astype(o_ref.dtype)
        return carry

    lax.fori_loop(0, t // CONF_RC, ln_chunk, 0, unroll=2)


def _conformer(u, dw_w, dw_b, ln_g, ln_b):
    b, l, c = u.shape
    t, h = CONF_T, CONF_HALO
    nt = l // t
    hb = t // h
    last_hb = l // h - 1
    vec = pl.BlockSpec((1, c), lambda bi, i: (0, 0))
    nc = c // LANES
    chunked = lambda rows: pl.BlockSpec((nc, rows, LANES), lambda bi, i: (0, 0, 0))
    w3 = dw_w.reshape(CONF_KERNEL, nc, LANES).transpose(1, 0, 2)
    return pl.pallas_call(
        _conformer_kernel,
        grid=(b, nt),
        in_specs=[pl.BlockSpec((1, t, c), lambda bi, i: (bi, i, 0)),
                  pl.BlockSpec((1, h, c), lambda bi, i: (bi, jnp.maximum(i * hb - 1, 0), 0)),
                  pl.BlockSpec((1, h, c), lambda bi, i: (bi, jnp.minimum((i + 1) * hb, last_hb), 0)),
                  chunked(CONF_KERNEL), chunked(1), vec, vec],
        out_specs=pl.BlockSpec((1, t, c), lambda bi, i: (bi, i, 0)),
        out_shape=jax.ShapeDtypeStruct((b, l, c), BF16),
        scratch_shapes=[pltpu.VMEM((nc, t + 2 * h, LANES), F32), pltpu.VMEM((nc, t, LANES), F32)],
        compiler_params=_params("parallel", "parallel"),
        name="conformer",
    )(u, u, u, w3, dw_b.reshape(nc, 1, LANES), ln_g.reshape(1, c), ln_b.reshape(1, c))


HIN_TM = 1024
HIN_TN = 1024
HIN_HALO = 16
HIN_N1 = HIN_TM // FFT_N2
HIN_PITCH = FFT_N2 + 8


def _hyena_in_kernel(hc_ref, hp_ref, hn_ref, w_ref, cw_ref, cb_ref, o_ref, h_ref, s_ref):
    i = pl.program_id(1)
    j = pl.program_id(2)
    nt = pl.num_programs(1)
    tm, hl = HIN_TM, HIN_HALO

    @pl.when(j == 0)
    def _():
        h_ref[0:hl, :] = jnp.where(i > 0, hp_ref[0], jnp.zeros_like(hp_ref[0]))
        h_ref[hl:hl + tm, :] = hc_ref[0]
        h_ref[hl + tm:hl + tm + hl, :] = jnp.where(i < nt - 1, hn_ref[0], jnp.zeros_like(hn_ref[0]))

    e = jnp.dot(h_ref[...], w_ref[...], preferred_element_type=F32)
    nrows = tm + 2 * hl
    conv = (cw_ref[0:1, :] * pltpu.roll(e, 1, axis=0) + cw_ref[1:2, :] * e
            + cw_ref[2:3, :] * pltpu.roll(e, nrows - 1, axis=0) + cb_ref[...])
    nlc = HIN_TN // LANES
    for r in range(HIN_N1):
        r0 = hl + r * FFT_N2
        for lc in range(nlc):
            s_ref[lc, r * HIN_PITCH:r * HIN_PITCH + FFT_N2, :] = conv[r0:r0 + FFT_N2, lc * LANES:(lc + 1) * LANES]

    def gather(n2, carry):
        for lc in range(nlc):
            rows = s_ref[lc, pl.ds(n2, HIN_N1, stride=HIN_PITCH), :]
            o_ref[0, n2, :, lc * LANES:(lc + 1) * LANES] = rows.astype(o_ref.dtype)
        return carry

    lax.fori_loop(0, FFT_N2, gather, 0, unroll=8)


def _hyena_in(hn, w_in, cw, cb):
    b, l, d = hn.shape
    tm, tn, hl = HIN_TM, HIN_TN, HIN_HALO
    ncol = (HYENA_ORDER + 1) * HYENA_WIDTH // tn
    col0 = 2 * CONF_WIDTH // tn
    hb = tm // hl
    last_hb = l // hl - 1
    return pl.pallas_call(
        _hyena_in_kernel,
        grid=(b, l // tm, ncol),
        in_specs=[pl.BlockSpec((1, tm, d), lambda bi, i, j: (bi, i, 0)),
                  pl.BlockSpec((1, hl, d), lambda bi, i, j: (bi, jnp.maximum(i * hb - 1, 0), 0)),
                  pl.BlockSpec((1, hl, d), lambda bi, i, j: (bi, jnp.minimum((i + 1) * hb, last_hb), 0)),
                  pl.BlockSpec((d, tn), lambda bi, i, j: (0, col0 + j)),
                  pl.BlockSpec((SHORT_KERNEL, tn), lambda bi, i, j: (0, j)),
                  pl.BlockSpec((1, tn), lambda bi, i, j: (0, j))],
        out_specs=pl.BlockSpec((1, FFT_N2, HIN_N1, tn), lambda bi, i, j: (bi, 0, i, j)),
        out_shape=jax.ShapeDtypeStruct((b, FFT_N2, FFT_NZ1, ncol * tn), BF16),
        scratch_shapes=[pltpu.VMEM((tm + 2 * hl, d), BF16),
                        pltpu.VMEM((tn // LANES, HIN_N1 * HIN_PITCH, LANES), F32)],
        compiler_params=_params("parallel", "parallel", "arbitrary"),
        name="hyena_in",
    )(hn, hn, hn, w_in, cw, cb.reshape(1, -1))


MLP_HW = LANES // 2
MLP_W1_ROWS = 40
MLP_WI_ROW0 = MLP_W1_ROWS
MLP_VEC_ROW0 = MLP_WI_ROW0 + HYENA_INNER_MLPS * HYENA_FILTER_ORDER
MLP_ROWS = MLP_VEC_ROW0 + 8


def _filter_mlp_kernel(z_ref, p_ref, o_ref):
    hi = lax.Precision.HIGHEST

    def block_diag(rows0, nrows):
        w = p_ref[rows0:rows0 + nrows, :]
        lo = lax.broadcasted_iota(jnp.int32, w.shape, 1) < MLP_HW
        zeros = jnp.zeros((MLP_HW - nrows, LANES), F32)
        parts = [jnp.where(lo, w, 0.0), zeros, jnp.where(lo, 0.0, w), zeros]
        return jnp.concatenate([p for p in parts if p.shape[0]], axis=0)

    vec = lambda r: p_ref[MLP_VEC_ROW0 + r:MLP_VEC_ROW0 + r + 1, :]
    fr = vec(1 + HYENA_INNER_MLPS)
    w1 = block_diag(0, MLP_W1_ROWS)
    h = jnp.sin(fr * (jnp.dot(z_ref[...], w1, precision=hi, preferred_element_type=F32) + vec(0)))
    for i in range(HYENA_INNER_MLPS):
        wi = block_diag(MLP_WI_ROW0 + i * HYENA_FILTER_ORDER, HYENA_FILTER_ORDER)
        h = jnp.sin(fr * (jnp.dot(h, wi, precision=hi, preferred_element_type=F32) + vec(1 + i)))
    o_ref[...] = h


@functools.lru_cache(maxsize=None)
def _position_features_packed():
    f32 = np.float32
    t = np.linspace(0.0, 1.0, SEQ, dtype=f32)[:, None]
    w = (f32(2.0 * math.pi) * np.arange(SEQ, dtype=f32)[:, None] / f32(SEQ)).astype(f32)
    bands = (HYENA_EMB - 1) // 2
    f = np.linspace(1e-4, bands - 1, bands, dtype=f32)[None, :]
    z = np.concatenate([t, np.cos(f * w), -np.sin(f * w)], axis=-1).astype(f32)
    zp = np.zeros((SEQ // 2, LANES), f32)
    zp[:, :HYENA_EMB] = z[:SEQ // 2]
    zp[:, MLP_HW:MLP_HW + HYENA_EMB] = z[SEQ // 2:]
    return jnp.asarray(zp)


def _filter_mlp(w1, b1, w_inner, b_inner, freq):
    both = lambda a: jnp.concatenate([a, a], axis=-1)
    params = jnp.concatenate(
        [both(w1), jnp.zeros((MLP_W1_ROWS - HYENA_EMB, LANES), F32)]
        + [both(w_inner[i]) for i in range(HYENA_INNER_MLPS)]
        + [both(b1)[None], both(b_inner), both(freq)[None],
           jnp.zeros((MLP_ROWS - MLP_VEC_ROW0 - 2 - HYENA_INNER_MLPS, LANES), F32)], axis=0)
    return pl.pallas_call(
        _filter_mlp_kernel,
        out_shape=jax.ShapeDtypeStruct((SEQ // 2, LANES), F32),
        compiler_params=pltpu.CompilerParams(vmem_limit_bytes=VMEM_LIMIT),
        name="filter_mlp",
    )(_position_features_packed(), params)


def _stage_b_rhs(a_ref, k1):
    ar = a_ref[pl.ds(k1, FFT_N2, stride=FFT_PITCH), :]
    ai = a_ref[pl.ds(FFT_N1 + k1, FFT_N2, stride=FFT_PITCH), :]
    return jnp.concatenate([ar, ai], axis=0).astype(BF16)


def _slab_rows(n2):
    return pl.ds(pl.multiple_of(n2 * FFT_PITCH, 8), 2 * FFT_N1)


def _filter_spectrum_kernel(h_ref, wf_ref, wb_ref, dl_ref, fak_ref, t_ref, *refs, n_cast, col_maps):
    cast_in, o_ref, cast_out = refs[:n_cast], refs[n_cast], refs[n_cast + 1:2 * n_cast + 1]
    kt_ref, a_ref = refs[2 * n_cast + 1:]
    _cast_blocks(cast_in, cast_out, col_maps)
    rc = FILTER_RC
    half_chunks = SEQ // 2 // rc
    w = jnp.concatenate([wf_ref[...], wb_ref[...]], axis=1)
    zeros = jnp.zeros((MLP_HW, 2 * LANES), F32)
    ws = [jnp.concatenate([w, zeros], axis=0).astype(BF16), jnp.concatenate([zeros, w], axis=0).astype(BF16)]
    dl = jnp.concatenate([dl_ref[...], dl_ref[...]], axis=1)

    def gen(r, ss, p):
        r0 = pl.multiple_of(r * rc, rc)
        rows = lax.broadcasted_iota(jnp.int32, (rc, 2 * LANES), 0) + r0
        lanes = lax.broadcasted_iota(jnp.int32, (rc, 2 * LANES), 1)
        decay = jnp.exp(-(rows.astype(F32) * (1.0 / (SEQ - 1))) * dl)
        h = h_ref[pl.ds(pl.multiple_of(r0 - p * (SEQ // 2), rc), rc), :]
        k = jnp.dot(h.astype(BF16), ws[p], preferred_element_type=F32)
        k = k * decay
        k = jnp.where((rows >= SEQ - 1) & (lanes >= LANES), 0.0, k)
        for n in range(rc // FFT_N2):
            dst = pl.ds(pl.multiple_of((r * (rc // FFT_N2) + n) * FILTER_PITCH, 8), FFT_N2)
            kt_ref[0, dst, :] = k[n * FFT_N2:(n + 1) * FFT_N2, :LANES]
            kt_ref[1, dst, :] = k[n * FFT_N2:(n + 1) * FFT_N2, LANES:]
        return ss + jnp.sum(k * k, axis=0, keepdims=True)

    ss = jnp.zeros((1, 2 * LANES), F32)
    for p in range(2):
        ss = lax.fori_loop(p * half_chunks, (p + 1) * half_chunks, functools.partial(gen, p=p), ss)
    scale = lax.rsqrt(ss[:, :LANES] + ss[:, LANES:] + EPS)

    def filter_slab(n2):
        fwd = kt_ref[0, pl.ds(n2, FFT_NZ1, stride=FILTER_PITCH), :]
        bwd = kt_ref[1, pl.ds(FFT_N2 - 1 - n2, FFT_NZ1, stride=FILTER_PITCH), :]
        return (jnp.concatenate([fwd, bwd], axis=0) * scale).astype(BF16)

    def stage_a(p, carry):
        n2 = 2 * p
        rhs = jnp.concatenate([filter_slab(n2), filter_slab(n2 + 1)], axis=1)
        a = jnp.dot(fak_ref[...], rhs, preferred_element_type=F32)
        a_ref[_slab_rows(n2), :] = a[:, :LANES]
        a_ref[_slab_rows(n2 + 1), :] = a[:, LANES:]
        return carry

    lax.fori_loop(0, FFT_N2 // 2, stage_a, 0, unroll=FFT_UNROLL // 2)

    def stage_b(k1, carry):
        b = jnp.dot(t_ref[k1], _stage_b_rhs(a_ref, k1), preferred_element_type=F32)
        o_ref[0, k1] = b.astype(o_ref.dtype)
        return carry

    lax.fori_loop(0, FFT_N1, stage_b, 0, unroll=FFT_UNROLL)


def _filter_spectrum(hmlp, w_filt, fak, t, cast_weights, col_maps):
    assert HYENA_FILTER_ORDER == MLP_HW
    ncb = HYENA_WIDTH // LANES
    grid = (HYENA_ORDER, ncb)
    cast_specs, cast_shapes = _cast_specs(cast_weights, grid)
    w_spec = lambda direction: pl.BlockSpec((HYENA_FILTER_ORDER, LANES),
                                            lambda o, c: (0, (o * 2 + direction) * ncb + c))
    outs = pl.pallas_call(
        functools.partial(_filter_spectrum_kernel, n_cast=len(cast_weights), col_maps=col_maps),
        grid=grid,
        in_specs=[_const_spec((SEQ // 2, LANES)), w_spec(0), w_spec(1),
                  pl.BlockSpec((1, LANES), lambda o, c: (0, c)),
                  _const_spec((2 * FFT_N1, FFT_N1)),
                  _const_spec((FFT_N1, 2 * FFT_N2, 2 * FFT_N2))] + cast_specs,
        out_specs=[pl.BlockSpec((1, FFT_N1, 2 * FFT_N2, LANES), lambda o, c: (o, 0, 0, c))] + cast_specs,
        out_shape=[jax.ShapeDtypeStruct((HYENA_ORDER, FFT_N1, 2 * FFT_N2, HYENA_WIDTH), F32)] + cast_shapes,
        scratch_shapes=[pltpu.VMEM((2, FFT_NZ1 * FILTER_PITCH, LANES), F32),
                        pltpu.VMEM((FFT_N2 * FFT_PITCH, LANES), F32)],
        compiler_params=_params("arbitrary", "arbitrary"),
        name="filter_spectrum",
    )(hmlp, w_filt, w_filt, _decay_rates(), fak, t, *cast_weights)
    return outs[0], outs[1:]


def _hyena_conv_kernel(v_ref, xm_ref, kf_ref, sk_ref, fa_ref, t_ref, ti_ref, fd_ref, o_ref, a_ref, y_ref, z1_ref):
    def conv(u_ref, write):
        def stage_a(p, carry):
            n2 = 2 * p
            rhs = jnp.concatenate(
                [jnp.concatenate([u_ref[0, n], u_ref[1, n]], axis=0) for n in (n2, n2 + 1)], axis=1).astype(BF16)
            a = jnp.dot(fa_ref[...], rhs, preferred_element_type=F32)
            a_ref[_slab_rows(n2), :] = a[:, :LANES]
            a_ref[_slab_rows(n2 + 1), :] = a[:, LANES:]
            return carry

        lax.fori_loop(0, FFT_N2 // 2, stage_a, 0, unroll=FFT_UNROLL // 2)

        def stage_b(k1, carry):
            b = jnp.dot(t_ref[k1], _stage_b_rhs(a_ref, k1), preferred_element_type=F32)
            kf = kf_ref[0, k1]
            br, bi = b[:FFT_N2], b[FFT_N2:]
            kr, ki = kf[:FFT_N2], kf[FFT_N2:]
            y_ref[k1] = jnp.concatenate([br * kr - bi * ki, br * ki + bi * kr], axis=0).astype(BF16)
            return carry

        lax.fori_loop(0, FFT_N1, stage_b, 0, unroll=FFT_UNROLL)

        def stage_c(k1, carry):
            c = jnp.dot(ti_ref[k1], y_ref[k1], preferred_element_type=F32)
            a_ref[pl.ds(k1, FFT_N2, stride=FFT_PITCH), :] = c[:FFT_N2]
            a_ref[pl.ds(FFT_N1 + k1, FFT_N2, stride=FFT_PITCH), :] = c[FFT_N2:]
            return carry

        lax.fori_loop(0, FFT_N1, stage_c, 0, unroll=FFT_UNROLL)

        sk = sk_ref[0]

        def stage_d(p, carry):
            n2 = 2 * p
            cb = jnp.concatenate([a_ref[_slab_rows(n2), :], a_ref[_slab_rows(n2 + 1), :]], axis=1).astype(BF16)
            y = jnp.dot(fd_ref[...], cb, preferred_element_type=F32)
            for s in range(2):
                for bi in range(2):
                    yb = y[bi * FFT_NZ1:(bi + 1) * FFT_NZ1, s * LANES:(s + 1) * LANES]
                    write(bi, n2 + s, xm_ref[bi, n2 + s].astype(F32) * (yb + sk * u_ref[bi, n2 + s].astype(F32)))
            return carry

        lax.fori_loop(0, FFT_N2 // 2, stage_d, 0, unroll=FFT_UNROLL // 2)

    def write_z1(bi, n2, z):
        z1_ref[bi, n2] = z.astype(z1_ref.dtype)

    def write_out(bi, n2, z):
        o_ref[bi, pl.ds(n2, FFT_NZ1, stride=FFT_N2), :] = z

    order = pl.program_id(1)

    @pl.when(order == 0)
    def _():
        conv(v_ref, write_z1)

    @pl.when(order == 1)
    def _():
        conv(z1_ref, write_out)


def _hyena_conv(uc, kf, skip, consts):
    fa, _, t, ti, fd = consts
    ncb = HYENA_WIDTH // LANES
    slab_block = (BATCH, FFT_N2, FFT_NZ1, LANES)
    return pl.pallas_call(
        _hyena_conv_kernel,
        grid=(ncb, HYENA_ORDER),
        in_specs=[pl.BlockSpec(slab_block, lambda c, o: (0, 0, 0, c)),
                  pl.BlockSpec(slab_block, lambda c, o: (0, 0, 0, (o + 1) * ncb + c)),
                  pl.BlockSpec((1, FFT_N1, 2 * FFT_N2, LANES), lambda c, o: (o, 0, 0, c)),
                  pl.BlockSpec((1, 1, LANES), lambda c, o: (o, 0, c)),
                  _const_spec(fa.shape), _const_spec(t.shape), _const_spec(ti.shape), _const_spec(fd.shape)],
        out_specs=pl.BlockSpec((BATCH, SEQ, LANES), lambda c, o: (0, 0, c)),
        out_shape=jax.ShapeDtypeStruct((BATCH, SEQ, HYENA_WIDTH), F32),
        scratch_shapes=[pltpu.VMEM((FFT_N2 * FFT_PITCH, LANES), F32),
                        pltpu.VMEM((FFT_N1, 2 * FFT_N2, LANES), BF16),
                        pltpu.VMEM(slab_block, BF16)],
        compiler_params=_params("arbitrary", "arbitrary", vmem_limit=CONV_VMEM_LIMIT),
        name="hyena_conv",
    )(uc, uc, kf, skip.reshape(HYENA_ORDER, 1, HYENA_WIDTH), fa, t, ti, fd)


MA_TM = 256
MA_VMEM_LIMIT = 60 * 1024 * 1024


def _mix_attn_kernel(x_ref, yc_ref, yh_ref, wa_ref, wb_ref, g_ref, wq_ref, k_ref, v_ref, wo_ref, *refs, n_cast):
    cast_in, o_ref, cast_out = refs[:n_cast], refs[n_cast], refs[n_cast + 1:2 * n_cast + 1]
    x1_ref, q_ref, ctx_ref = refs[2 * n_cast + 1:]
    _cast_blocks(cast_in, cast_out)
    mix = jnp.dot(yc_ref[0], wa_ref[...], preferred_element_type=F32)
    mix = mix + jnp.dot(yh_ref[0].astype(BF16), wb_ref[...], preferred_element_type=F32)
    x1 = x_ref[0] + mix
    x1_ref[...] = x1
    q_ref[...] = jnp.dot(_rms(x1, g_ref[...]).astype(BF16), wq_ref[...],
                         preferred_element_type=F32).astype(BF16)
    hd = XATTN_HEAD_DIM
    for h in range(XATTN_HEADS):
        hs = slice(h * hd, (h + 1) * hd)
        s = lax.dot_general(q_ref[:, hs], k_ref[0, :, hs], (((1,), (1,)), ((), ())),
                            preferred_element_type=F32) * (hd ** -0.5)
        s = s - jnp.max(s, axis=-1, keepdims=True)
        e = jnp.exp(s)
        pr = e / jnp.sum(e, axis=-1, keepdims=True)
        ctx_ref[:, hs] = jnp.dot(pr.astype(BF16), v_ref[0, :, hs], preferred_element_type=F32).astype(BF16)
    o_ref[0] = x1_ref[...] + jnp.dot(ctx_ref[...], wo_ref[...], preferred_element_type=F32)


def _mix_attn(x, yc, yh, w_out, g, wq, k, v, wo, cast_weights):
    b, l, d = x.shape
    tm = MA_TM
    kc = yc.shape[-1]
    grid = (b, l // tm)
    tok = lambda width: pl.BlockSpec((1, tm, width), lambda bi, i: (bi, i, 0))
    mem = pl.BlockSpec((1, N_MEM, d), lambda bi, i: (bi, 0, 0))
    cast_specs, cast_shapes = _cast_specs(cast_weights, grid)
    outs = pl.pallas_call(
        functools.partial(_mix_attn_kernel, n_cast=len(cast_weights)),
        grid=grid,
        in_specs=[tok(d), tok(kc), tok(kc),
                  _const_spec((kc, d), (0, 0)), _const_spec((kc, d), (1, 0)),
                  pl.BlockSpec((1, d), lambda bi, i: (0, 0)),
                  _const_spec((d, d)), mem, mem, _const_spec((d, d))] + cast_specs,
        out_specs=[tok(d)] + cast_specs,
        out_shape=[jax.ShapeDtypeStruct((b, l, d), F32)] + cast_shapes,
        scratch_shapes=[pltpu.VMEM((tm, d), F32), pltpu.VMEM((tm, d), BF16), pltpu.VMEM((tm, d), BF16)],
        compiler_params=_params("arbitrary", "arbitrary", vmem_limit=MA_VMEM_LIMIT),
        name="mix_attn",
    )(x, yc, yh, w_out, w_out, g.reshape(1, d), wq, k, v, wo, *cast_weights)
    return outs[0], outs[1:]


FFN_TM = 1024
FFN_FC = 512
FFN_VMEM_LIMIT = 62 * 1024 * 1024
FFN_HALO = 16


def _ffn_kernel(x_ref, xp_ref, xn_ref, g_ref, wg_ref, wv_ref, cw_ref, cb_ref, wd_ref, gf_ref,
                o_ref, h_ref, ge_ref):
    i = pl.program_id(1)
    j = pl.program_id(2)
    nt = pl.num_programs(1)
    nf = pl.num_programs(2)
    tm, hl = FFN_TM, FFN_HALO

    @pl.when(j == 0)
    def _():
        g = g_ref[...]
        h_ref[0:hl, :] = (_rms(xp_ref[0], g) * (i > 0).astype(F32)).astype(BF16)
        h_ref[hl:hl + tm, :] = _rms(x_ref[0], g).astype(BF16)
        h_ref[hl + tm:hl + tm + hl, :] = (_rms(xn_ref[0], g) * (i < nt - 1).astype(F32)).astype(BF16)
        o_ref[...] = jnp.zeros_like(o_ref)

    ge_ref[...] = jnp.dot(h_ref[...], wg_ref[...], preferred_element_type=F32)
    val = jnp.dot(h_ref[pl.ds(hl, tm), :], wv_ref[...], preferred_element_type=F32)
    gate = (cw_ref[0:1, :] * ge_ref[pl.ds(hl - 1, tm), :]
            + cw_ref[1:2, :] * ge_ref[pl.ds(hl, tm), :]
            + cw_ref[2:3, :] * ge_ref[pl.ds(hl + 1, tm), :]
            + cb_ref[...])
    act = (gate * _sigmoid(gate) * val).astype(BF16)
    o_ref[0] += jnp.dot(act, wd_ref[...], preferred_element_type=F32)

    @pl.when(j == nf - 1)
    def _():
        o_ref[0] = _rms(x_ref[0] + o_ref[0], gf_ref[...])


def _ffn(x, g_ffn, w_up, dw_w, dw_b, w_down, g_final):
    b, l, d = x.shape
    tm, fc, hl = FFN_TM, FFN_FC, FFN_HALO
    nf = D_FF // fc
    hb = tm // hl
    last_hb = l // hl - 1
    return pl.pallas_call(
        _ffn_kernel,
        grid=(b, l // tm, nf),
        in_specs=[pl.BlockSpec((1, tm, d), lambda bi, i, j: (bi, i, 0)),
                  pl.BlockSpec((1, hl, d), lambda bi, i, j: (bi, jnp.maximum(i * hb - 1, 0), 0)),
                  pl.BlockSpec((1, hl, d), lambda bi, i, j: (bi, jnp.minimum((i + 1) * hb, last_hb), 0)),
                  pl.BlockSpec((1, d), lambda bi, i, j: (0, 0)),
                  pl.BlockSpec((d, fc), lambda bi, i, j: (0, j)),
                  pl.BlockSpec((d, fc), lambda bi, i, j: (0, nf + j)),
                  pl.BlockSpec((3, fc), lambda bi, i, j: (0, j)),
                  pl.BlockSpec((1, fc), lambda bi, i, j: (0, j)),
                  pl.BlockSpec((fc, d), lambda bi, i, j: (j, 0)),
                  pl.BlockSpec((1, d), lambda bi, i, j: (0, 0))],
        out_specs=pl.BlockSpec((1, tm, d), lambda bi, i, j: (bi, i, 0)),
        out_shape=jax.ShapeDtypeStruct((b, l, d), F32),
        scratch_shapes=[pltpu.VMEM((tm + 2 * hl, d), BF16),
                        pltpu.VMEM((tm + 2 * hl, fc), F32)],
        compiler_params=_params("parallel", "parallel", "arbitrary", vmem_limit=FFN_VMEM_LIMIT),
        name="ffn",
    )(x, x, x, g_ffn.reshape(1, d), w_up, w_up, dw_w, dw_b.reshape(1, -1), w_down, g_final.reshape(1, d))


def kernel(x, mem, g_mix, w_in, conf_dw_w, conf_dw_b, conf_ln_g, conf_ln_b, hyena_short_w, hyena_short_b, hyena_w1, hyena_b1, hyena_w_inner, hyena_b_inner, hyena_w_filt, hyena_sin_freq, hyena_skip, w_out, g_xattn, g_mem, w_q, w_k, w_v, w_o, g_ffn, w_ffn_up, ffn_dw_w, ffn_dw_b, w_ffn_down, g_final):
    b, l, d = x.shape
    consts = _dft_constants()

    hmlp = _filter_mlp(hyena_w1[0], hyena_b1[0], hyena_w_inner[0], hyena_b_inner[0], hyena_sin_freq[0])
    kf, (w_in_b, w_out_b, w_q_b, w_o_b) = _filter_spectrum(
        hmlp, hyena_w_filt[0], consts[1], consts[2], [w_in[0], w_out[0], w_q[0], w_o[0]],
        [_w_in_column_moves(CIN_TN), None, None, None])

    u_conf, hn = _conf_in(x.reshape(b * l, d), g_mix[0], w_in_b)
    y_conf = _conformer(u_conf.reshape(b, l, CONF_WIDTH), conf_dw_w[0], conf_dw_b[0], conf_ln_g[0],
                        conf_ln_b[0])

    uc = _hyena_in(hn.reshape(b, l, d), w_in_b, hyena_short_w[0], hyena_short_b[0])
    z2 = _hyena_conv(uc, kf, hyena_skip[0], consts)

    memf = mem.reshape(b * N_MEM, d)
    k = _norm_matmul(memf, g_mem[0], w_k[0], b * N_MEM, 1024, BF16).reshape(b, N_MEM, d)
    v = _norm_matmul(memf, g_mem[0], w_v[0], b * N_MEM, 1024, BF16).reshape(b, N_MEM, d)
    x2, (w_up_b, w_down_b) = _mix_attn(x, y_conf, z2, w_out_b, g_xattn[0], w_q_b, k, v, w_o_b,
                                       [w_ffn_up[0], w_ffn_down[0]])

    return _ffn(x2, g_ffn[0], w_up_b, ffn_dw_w[0], ffn_dw_b[0], w_down_b, g_final)
```

```python
import functools
import math

import numpy as np
import jax
import jax.numpy as jnp
from jax import lax
from jax.experimental import pallas as pl
from jax.experimental.pallas import tpu as pltpu

F32 = jnp.float32
BF16 = jnp.bfloat16

D_MODEL = 2048
BATCH = 2
SEQ = 4096
CONF_WIDTH = 1024
HYENA_WIDTH = 1024
CONF_KERNEL = 31
SHORT_KERNEL = 3
HYENA_ORDER = 2
HYENA_EMB = 33
HYENA_FILTER_ORDER = 64
HYENA_INNER_MLPS = 2
HYENA_FAST_DECAY = 0.3
HYENA_SLOW_DECAY = 1.5
HYENA_TARGET = 1e-2
N_MEM = 256
XATTN_HEADS = 4
XATTN_HEAD_DIM = D_MODEL // XATTN_HEADS
D_FF = 5632
EPS = 1e-6
IN_COLS = 2 * CONF_WIDTH + (HYENA_ORDER + 1) * HYENA_WIDTH

LANES = 128
VMEM_LIMIT = 56 * 1024 * 1024

N_FFT = 2 * SEQ
FFT_N1 = 128
FFT_N2 = 64
FFT_NZ1 = SEQ // FFT_N2
FFT_PITCH = 2 * FFT_N1 + 8
FFT_UNROLL = 64
FILTER_UNROLL = 128
FILTER_RC = 512
FILTER_PITCH = FFT_N2 + 8
CONV_VMEM_LIMIT = 60 * 1024 * 1024


def _params(*sem, vmem_limit=VMEM_LIMIT):
    return pltpu.CompilerParams(dimension_semantics=sem, vmem_limit_bytes=vmem_limit)


def _const_spec(shape, index=None):
    index = (0,) * len(shape) if index is None else index
    return pl.BlockSpec(shape, lambda *_: index, pipeline_mode=pl.Buffered(1))


def _sigmoid(x):
    return 0.5 * (jnp.tanh(0.5 * x) + 1.0)


def _cast_specs(windows, grid_shape):
    nsteps = math.prod(grid_shape)

    def step(g):
        s = g[0]
        for gi, n in zip(g[1:], grid_shape[1:]):
            s = s * n + gi
        return s

    arrays, in_specs, out_specs, shapes = [], [], [], []
    for win in windows:
        w, width, cblk = win if isinstance(win, tuple) else (win, win.shape[1], 0)
        rows = w.shape[0]
        assert rows % (16 * nsteps) == 0 and (cblk + 1) * width <= w.shape[1], (w.shape, width, cblk, nsteps)
        arrays.append(w)
        in_specs.append(pl.BlockSpec((rows // nsteps, width), lambda *g, c=cblk: (step(g), c)))
        out_specs.append(pl.BlockSpec((rows // nsteps, width), lambda *g: (step(g), 0)))
        shapes.append(jax.ShapeDtypeStruct((rows, width), BF16))
    return arrays, in_specs, out_specs, shapes


def _cast_blocks(in_refs, out_refs):
    for i_ref, o_ref in zip(in_refs, out_refs):
        o_ref[...] = i_ref[...].astype(BF16)


def _rms(x, g):
    ms = jnp.mean(x * x, axis=-1, keepdims=True)
    return x * lax.rsqrt(ms + EPS) * g


@functools.lru_cache(maxsize=None)
def _dft_constants():
    k1 = np.arange(FFT_N1)
    f1 = np.exp(-2j * np.pi * np.outer(k1, k1) / FFT_N1)
    fh = f1[:, :FFT_NZ1]
    fa = np.block([[fh.real, -fh.imag], [fh.imag, fh.real]])
    perm = np.concatenate([np.arange(FFT_NZ1), FFT_N1 - 1 - np.arange(FFT_N1 - FFT_NZ1)])
    fp = f1[:, perm]
    fak = np.concatenate([fp.real, fp.imag], axis=0)
    n2 = np.arange(FFT_N2)
    kk = k1[:, None, None] + FFT_N1 * n2[None, :, None]
    ang = (n2[None, None, :] * kk) % N_FFT
    tc = np.exp(-2j * np.pi * ang / N_FFT)
    t = np.concatenate([np.concatenate([tc.real, -tc.imag], 2),
                        np.concatenate([tc.imag, tc.real], 2)], 1)
    g = np.conj(f1[:, :FFT_NZ1]).T / N_FFT
    fd = np.block([[g.real, -g.imag], [g.imag, g.real]])
    ti = np.transpose(t, (0, 2, 1))
    return tuple(jnp.asarray(a, dtype=BF16) for a in (fa, fak, t, ti, fd))


@functools.lru_cache(maxsize=None)
def _decay_rates():
    max_decay = math.log(HYENA_TARGET) / HYENA_FAST_DECAY
    min_decay = math.log(HYENA_TARGET) / HYENA_SLOW_DECAY
    return jnp.asarray(np.abs(np.linspace(min_decay, max_decay, HYENA_WIDTH, dtype=np.float32))[None, :])


def _norm_matmul_kernel(x_ref, g_ref, w_ref, o_ref, h_ref):
    @pl.when(pl.program_id(1) == 0)
    def _():
        h_ref[...] = _rms(x_ref[...], g_ref[...]).astype(BF16)

    o_ref[...] = jnp.dot(h_ref[...], w_ref[...].astype(BF16), preferred_element_type=F32).astype(o_ref.dtype)


def _norm_matmul(x, g, w, tm, tn, out_dtype):
    m, d = x.shape
    n = w.shape[1]
    return pl.pallas_call(
        _norm_matmul_kernel,
        grid=(m // tm, n // tn),
        in_specs=[pl.BlockSpec((tm, d), lambda i, j: (i, 0)),
                  pl.BlockSpec((1, d), lambda i, j: (0, 0)),
                  pl.BlockSpec((d, tn), lambda i, j: (0, j))],
        out_specs=pl.BlockSpec((tm, tn), lambda i, j: (i, j)),
        out_shape=jax.ShapeDtypeStruct((m, n), out_dtype),
        scratch_shapes=[pltpu.VMEM((tm, d), BF16)],
        compiler_params=_params("parallel", "arbitrary"),
        name="norm_matmul",
    )(x, g.reshape(1, d), w)


CIN_TM = 1024
CIN_TN = 1024


def _conf_in_kernel(x_ref, g_ref, w_ref, o_ref, h_ref):
    @pl.when(pl.program_id(1) == 0)
    def _():
        h_ref[...] = _rms(x_ref[...], g_ref[...]).astype(BF16)

    ag = jnp.dot(h_ref[...], w_ref[...], preferred_element_type=F32)
    o_ref[...] = (ag[:, :CIN_TN] * _sigmoid(ag[:, CIN_TN:])).astype(o_ref.dtype)


def _conf_in(x, g, w_in):
    m, d = x.shape
    tm, tn = CIN_TM, CIN_TN
    ncol = CONF_WIDTH // tn
    return pl.pallas_call(
        _conf_in_kernel,
        grid=(m // tm, ncol),
        in_specs=[pl.BlockSpec((tm, d), lambda i, j: (i, 0)),
                  pl.BlockSpec((1, d), lambda i, j: (0, 0)),
                  pl.BlockSpec((d, 2 * tn), lambda i, j: (0, j))],
        out_specs=[pl.BlockSpec((tm, tn), lambda i, j: (i, j)),
                   pl.BlockSpec((tm, d), lambda i, j: (i, 0))],
        out_shape=[jax.ShapeDtypeStruct((m, CONF_WIDTH), BF16), jax.ShapeDtypeStruct((m, d), BF16)],
        compiler_params=_params("parallel", "arbitrary"),
        name="conf_in",
    )(x, g.reshape(1, d), w_in)


CONF_T = 512
CONF_HALO = 16
CONF_RC = 64


def _conformer_kernel(u_in, up_in, un_in, w_ref, b_ref, lg_ref, lb_ref, o_ref, u_ref, y_ref):
    i = pl.program_id(1)
    nt = pl.num_programs(1)
    t = CONF_T
    h = CONF_HALO
    prev_ok = (i > 0).astype(F32)
    next_ok = (i < nt - 1).astype(F32)
    nc = CONF_WIDTH // LANES
    for c in range(nc):
        cs = slice(c * LANES, (c + 1) * LANES)
        u_ref[c, 0:h, :] = up_in[0, :, cs].astype(F32) * prev_ok
        u_ref[c, h:h + t, :] = u_in[0, :, cs].astype(F32)
        u_ref[c, h + t:h + t + h, :] = un_in[0, :, cs].astype(F32) * next_ok

    pad = CONF_KERNEL // 2

    def conv_chunk(c, carry):
        w = w_ref[c]
        bias = b_ref[c]
        for r in range(t // CONF_RC):
            r0 = r * CONF_RC
            acc = jnp.zeros((CONF_RC, LANES), F32) + bias
            for j in range(CONF_KERNEL):
                acc = acc + w[j:j + 1, :] * u_ref[c, r0 + h - pad + j:r0 + h - pad + j + CONF_RC, :]
            y_ref[c, r0:r0 + CONF_RC, :] = acc
        return carry

    lax.fori_loop(0, nc, conv_chunk, 0)

    def ln_chunk(r, carry):
        r0 = pl.multiple_of(r * CONF_RC, CONF_RC)
        ys = [y_ref[c, pl.ds(r0, CONF_RC), :] for c in range(nc)]
        tot = ys[0]
        for y in ys[1:]:
            tot = tot + y
        mu = jnp.sum(tot, axis=-1, keepdims=True) * (1.0 / CONF_WIDTH)
        yc = [y - mu for y in ys]
        sq = yc[0] * yc[0]
        for y in yc[1:]:
            sq = sq + y * y
        inv = lax.rsqrt(jnp.sum(sq, axis=-1, keepdims=True) * (1.0 / CONF_WIDTH) + EPS)
        for c in range(nc):
            cs = slice(c * LANES, (c + 1) * LANES)
            yn = yc[c] * inv * lg_ref[:, cs] + lb_ref[:, cs]
            o_ref[0, pl.ds(r0, CONF_RC), cs] = (yn * _sigmoid(yn)).astype(o_ref.dtype)
        return carry

    lax.fori_loop(0, t // CONF_RC, ln_chunk, 0, unroll=8)


def _conformer(u, dw_w, dw_b, ln_g, ln_b):
    b, l, c = u.shape
    t, h = CONF_T, CONF_HALO
    nt = l // t
    hb = t // h
    last_hb = l // h - 1
    vec = pl.BlockSpec((1, c), lambda bi, i: (0, 0))
    nc = c // LANES
    chunked = lambda rows: pl.BlockSpec((nc, rows, LANES), lambda bi, i: (0, 0, 0))
    w3 = dw_w.reshape(CONF_KERNEL, nc, LANES).transpose(1, 0, 2)
    return pl.pallas_call(
        _conformer_kernel,
        grid=(b, nt),
        in_specs=[pl.BlockSpec((1, t, c), lambda bi, i: (bi, i, 0)),
                  pl.BlockSpec((1, h, c), lambda bi, i: (bi, jnp.maximum(i * hb - 1, 0), 0)),
                  pl.BlockSpec((1, h, c), lambda bi, i: (bi, jnp.minimum((i + 1) * hb, last_hb), 0)),
                  chunked(CONF_KERNEL), chunked(1), vec, vec],
        out_specs=pl.BlockSpec((1, t, c), lambda bi, i: (bi, i, 0)),
        out_shape=jax.ShapeDtypeStruct((b, l, c), BF16),
        scratch_shapes=[pltpu.VMEM((nc, t + 2 * h, LANES), F32), pltpu.VMEM((nc, t, LANES), F32)],
        compiler_params=_params("parallel", "parallel"),
        name="conformer",
    )(u, u, u, w3, dw_b.reshape(nc, 1, LANES), ln_g.reshape(1, c), ln_b.reshape(1, c))


HIN_TM = 1024
HIN_TN = 1024
HIN_HALO = 16
HIN_N1 = HIN_TM // FFT_N2
HIN_PITCH = FFT_N2 + 8


def _hyena_in_kernel(hc_ref, hp_ref, hn_ref, w_ref, cw_ref, cb_ref, o_ref, h_ref, s_ref):
    i = pl.program_id(1)
    j = pl.program_id(2)
    nt = pl.num_programs(1)
    tm, hl = HIN_TM, HIN_HALO

    @pl.when(j == 0)
    def _():
        h_ref[0:hl, :] = jnp.where(i > 0, hp_ref[0], jnp.zeros_like(hp_ref[0]))
        h_ref[hl:hl + tm, :] = hc_ref[0]
        h_ref[hl + tm:hl + tm + hl, :] = jnp.where(i < nt - 1, hn_ref[0], jnp.zeros_like(hn_ref[0]))

    e = jnp.dot(h_ref[...], w_ref[...], preferred_element_type=F32)
    nrows = tm + 2 * hl
    conv = (cw_ref[0:1, :] * pltpu.roll(e, 1, axis=0) + cw_ref[1:2, :] * e
            + cw_ref[2:3, :] * pltpu.roll(e, nrows - 1, axis=0) + cb_ref[...])
    nlc = HIN_TN // LANES
    for r in range(HIN_N1):
        r0 = hl + r * FFT_N2
        for lc in range(nlc):
            s_ref[lc, r * HIN_PITCH:r * HIN_PITCH + FFT_N2, :] = conv[r0:r0 + FFT_N2, lc * LANES:(lc + 1) * LANES]

    def gather(n2, carry):
        for lc in range(nlc):
            rows = s_ref[lc, pl.ds(n2, HIN_N1, stride=HIN_PITCH), :]
            o_ref[0, n2, :, lc * LANES:(lc + 1) * LANES] = rows.astype(o_ref.dtype)
        return carry

    lax.fori_loop(0, FFT_N2, gather, 0, unroll=8)


def _hyena_in(hn, w_in, cw, cb):
    b, l, d = hn.shape
    tm, tn, hl = HIN_TM, HIN_TN, HIN_HALO
    ncol = (HYENA_ORDER + 1) * HYENA_WIDTH // tn
    col0 = 2 * CONF_WIDTH // tn
    hb = tm // hl
    last_hb = l // hl - 1
    return pl.pallas_call(
        _hyena_in_kernel,
        grid=(b, l // tm, ncol),
        in_specs=[pl.BlockSpec((1, tm, d), lambda bi, i, j: (bi, i, 0)),
                  pl.BlockSpec((1, hl, d), lambda bi, i, j: (bi, jnp.maximum(i * hb - 1, 0), 0)),
                  pl.BlockSpec((1, hl, d), lambda bi, i, j: (bi, jnp.minimum((i + 1) * hb, last_hb), 0)),
                  pl.BlockSpec((d, tn), lambda bi, i, j: (0, col0 + j)),
                  pl.BlockSpec((SHORT_KERNEL, tn), lambda bi, i, j: (0, j)),
                  pl.BlockSpec((1, tn), lambda bi, i, j: (0, j))],
        out_specs=pl.BlockSpec((1, FFT_N2, HIN_N1, tn), lambda bi, i, j: (bi, 0, i, j)),
        out_shape=jax.ShapeDtypeStruct((b, FFT_N2, FFT_NZ1, ncol * tn), BF16),
        scratch_shapes=[pltpu.VMEM((tm + 2 * hl, d), BF16),
                        pltpu.VMEM((tn // LANES, HIN_N1 * HIN_PITCH, LANES), F32)],
        compiler_params=_params("parallel", "parallel", "arbitrary"),
        name="hyena_in",
    )(hn, hn, hn, w_in, cw, cb.reshape(1, -1))


MLP_HW = LANES // 2
MLP_W1_ROWS = 40
MLP_WI_ROW0 = MLP_W1_ROWS
MLP_VEC_ROW0 = MLP_WI_ROW0 + HYENA_INNER_MLPS * HYENA_FILTER_ORDER
MLP_ROWS = MLP_VEC_ROW0 + 8


def _filter_mlp_kernel(z_ref, p_ref, o_ref):
    hi = lax.Precision.HIGHEST

    def block_diag(rows0, nrows):
        w = p_ref[rows0:rows0 + nrows, :]
        lo = lax.broadcasted_iota(jnp.int32, w.shape, 1) < MLP_HW
        zeros = jnp.zeros((MLP_HW - nrows, LANES), F32)
        parts = [jnp.where(lo, w, 0.0), zeros, jnp.where(lo, 0.0, w), zeros]
        return jnp.concatenate([p for p in parts if p.shape[0]], axis=0)

    vec = lambda r: p_ref[MLP_VEC_ROW0 + r:MLP_VEC_ROW0 + r + 1, :]
    fr = vec(1 + HYENA_INNER_MLPS)
    w1 = block_diag(0, MLP_W1_ROWS)
    h = jnp.sin(fr * (jnp.dot(z_ref[...], w1, precision=hi, preferred_element_type=F32) + vec(0)))
    for i in range(HYENA_INNER_MLPS):
        wi = block_diag(MLP_WI_ROW0 + i * HYENA_FILTER_ORDER, HYENA_FILTER_ORDER)
        h = jnp.sin(fr * (jnp.dot(h, wi, precision=hi, preferred_element_type=F32) + vec(1 + i)))
    o_ref[...] = h


@functools.lru_cache(maxsize=None)
def _position_features_packed():
    f32 = np.float32
    t = np.linspace(0.0, 1.0, SEQ, dtype=f32)[:, None]
    w = (f32(2.0 * math.pi) * np.arange(SEQ, dtype=f32)[:, None] / f32(SEQ)).astype(f32)
    bands = (HYENA_EMB - 1) // 2
    f = np.linspace(1e-4, bands - 1, bands, dtype=f32)[None, :]
    z = np.concatenate([t, np.cos(f * w), -np.sin(f * w)], axis=-1).astype(f32)
    zp = np.zeros((SEQ // 2, LANES), f32)
    zp[:, :HYENA_EMB] = z[:SEQ // 2]
    zp[:, MLP_HW:MLP_HW + HYENA_EMB] = z[SEQ // 2:]
    return jnp.asarray(zp)


def _filter_mlp(w1, b1, w_inner, b_inner, freq):
    both = lambda a: jnp.concatenate([a, a], axis=-1)
    params = jnp.concatenate(
        [both(w1), jnp.zeros((MLP_W1_ROWS - HYENA_EMB, LANES), F32)]
        + [both(w_inner[i]) for i in range(HYENA_INNER_MLPS)]
        + [both(b1)[None], both(b_inner), both(freq)[None],
           jnp.zeros((MLP_ROWS - MLP_VEC_ROW0 - 2 - HYENA_INNER_MLPS, LANES), F32)], axis=0)
    return pl.pallas_call(
        _filter_mlp_kernel,
        out_shape=jax.ShapeDtypeStruct((SEQ // 2, LANES), F32),
        compiler_params=pltpu.CompilerParams(vmem_limit_bytes=VMEM_LIMIT),
        name="filter_mlp",
    )(_position_features_packed(), params)


def _stage_b_rhs(a_ref, k1):
    ar = a_ref[pl.ds(k1, FFT_N2, stride=FFT_PITCH), :]
    ai = a_ref[pl.ds(FFT_N1 + k1, FFT_N2, stride=FFT_PITCH), :]
    return jnp.concatenate([ar, ai], axis=0).astype(BF16)


def _slab_rows(n2):
    return pl.ds(pl.multiple_of(n2 * FFT_PITCH, 8), 2 * FFT_N1)


def _filter_spectrum_kernel(h_ref, wf_ref, wb_ref, dl_ref, fak_ref, t_ref, *refs, n_cast):
    cast_in, o_ref, cast_out = refs[:n_cast], refs[n_cast], refs[n_cast + 1:2 * n_cast + 1]
    kt_ref, a_ref = refs[2 * n_cast + 1:]
    _cast_blocks(cast_in, cast_out)
    rc = FILTER_RC
    half_chunks = SEQ // 2 // rc
    w = jnp.concatenate([wf_ref[...], wb_ref[...]], axis=1)
    zeros = jnp.zeros((MLP_HW, 2 * LANES), F32)
    ws = [jnp.concatenate([w, zeros], axis=0).astype(BF16), jnp.concatenate([zeros, w], axis=0).astype(BF16)]
    dl = jnp.concatenate([dl_ref[...], dl_ref[...]], axis=1)

    def gen(r, ss, p):
        r0 = pl.multiple_of(r * rc, rc)
        rows = lax.broadcasted_iota(jnp.int32, (rc, 2 * LANES), 0) + r0
        lanes = lax.broadcasted_iota(jnp.int32, (rc, 2 * LANES), 1)
        decay = jnp.exp(-(rows.astype(F32) * (1.0 / (SEQ - 1))) * dl)
        h = h_ref[pl.ds(pl.multiple_of(r0 - p * (SEQ // 2), rc), rc), :]
        k = jnp.dot(h.astype(BF16), ws[p], preferred_element_type=F32)
        k = k * decay
        k = jnp.where((rows >= SEQ - 1) & (lanes >= LANES), 0.0, k)
        for n in range(rc // FFT_N2):
            dst = pl.ds(pl.multiple_of((r * (rc // FFT_N2) + n) * FILTER_PITCH, 8), FFT_N2)
            kt_ref[0, dst, :] = k[n * FFT_N2:(n + 1) * FFT_N2, :LANES]
            kt_ref[1, dst, :] = k[n * FFT_N2:(n + 1) * FFT_N2, LANES:]
        return ss + jnp.sum(k * k, axis=0, keepdims=True)

    ss = jnp.zeros((1, 2 * LANES), F32)
    for p in range(2):
        ss = lax.fori_loop(p * half_chunks, (p + 1) * half_chunks, functools.partial(gen, p=p), ss, unroll=4)
    scale = lax.rsqrt(ss[:, :LANES] + ss[:, LANES:] + EPS)

    def filter_slab(n2):
        fwd = kt_ref[0, pl.ds(n2, FFT_NZ1, stride=FILTER_PITCH), :]
        bwd = kt_ref[1, pl.ds(FFT_N2 - 1 - n2, FFT_NZ1, stride=FILTER_PITCH), :]
        return (jnp.concatenate([fwd, bwd], axis=0) * scale).astype(BF16)

    def stage_a(p, carry):
        n2 = 2 * p
        rhs = jnp.concatenate([filter_slab(n2), filter_slab(n2 + 1)], axis=1)
        a = jnp.dot(fak_ref[...], rhs, preferred_element_type=F32)
        a_ref[_slab_rows(n2), :] = a[:, :LANES]
        a_ref[_slab_rows(n2 + 1), :] = a[:, LANES:]
        return carry

    lax.fori_loop(0, FFT_N2 // 2, stage_a, 0, unroll=min(FFT_UNROLL // 2, FFT_N2 // 2))

    def stage_b(k1, carry):
        b = jnp.dot(t_ref[k1], _stage_b_rhs(a_ref, k1), preferred_element_type=F32)
        o_ref[0, k1] = b.astype(o_ref.dtype)
        return carry

    lax.fori_loop(0, FFT_N1, stage_b, 0, unroll=FILTER_UNROLL)


def _filter_spectrum(hmlp, w_filt, fak, t, cast_windows):
    assert HYENA_FILTER_ORDER == MLP_HW
    ncb = HYENA_WIDTH // LANES
    grid = (HYENA_ORDER, ncb)
    cast_weights, cast_in, cast_out, cast_shapes = _cast_specs(cast_windows, grid)
    w_spec = lambda direction: pl.BlockSpec((HYENA_FILTER_ORDER, LANES),
                                            lambda o, c: (0, (o * 2 + direction) * ncb + c))
    outs = pl.pallas_call(
        functools.partial(_filter_spectrum_kernel, n_cast=len(cast_weights)),
        grid=grid,
        in_specs=[_const_spec((SEQ // 2, LANES)), w_spec(0), w_spec(1),
                  pl.BlockSpec((1, LANES), lambda o, c: (0, c)),
                  _const_spec((2 * FFT_N1, FFT_N1)),
                  _const_spec((FFT_N1, 2 * FFT_N2, 2 * FFT_N2))] + cast_in,
        out_specs=[pl.BlockSpec((1, FFT_N1, 2 * FFT_N2, LANES), lambda o, c: (o, 0, 0, c))] + cast_out,
        out_shape=[jax.ShapeDtypeStruct((HYENA_ORDER, FFT_N1, 2 * FFT_N2, HYENA_WIDTH), BF16)] + cast_shapes,
        scratch_shapes=[pltpu.VMEM((2, FFT_NZ1 * FILTER_PITCH, LANES), F32),
                        pltpu.VMEM((FFT_N2 * FFT_PITCH, LANES), F32)],
        compiler_params=_params("arbitrary", "arbitrary"),
        name="filter_spectrum",
    )(hmlp, w_filt, w_filt, _decay_rates(), fak, t, *cast_weights)
    return outs[0], outs[1:]


def _hyena_conv_kernel(v_ref, xm_ref, kf_ref, sk_ref, fa_ref, t_ref, ti_ref, fd_ref, o_ref, a_ref, y_ref, z1_ref):
    def conv(u_ref, write):
        def stage_a(p, carry):
            n2 = 2 * p
            rhs = jnp.concatenate(
                [jnp.concatenate([u_ref[0, n], u_ref[1, n]], axis=0) for n in (n2, n2 + 1)], axis=1).astype(BF16)
            a = jnp.dot(fa_ref[...], rhs, preferred_element_type=F32)
            a_ref[_slab_rows(n2), :] = a[:, :LANES]
            a_ref[_slab_rows(n2 + 1), :] = a[:, LANES:]
            return carry

        lax.fori_loop(0, FFT_N2 // 2, stage_a, 0, unroll=min(FFT_UNROLL // 2, FFT_N2 // 2))

        def stage_b(k1, carry):
            b = jnp.dot(t_ref[k1], _stage_b_rhs(a_ref, k1), preferred_element_type=F32)
            kf = kf_ref[0, k1].astype(F32)
            br, bi = b[:FFT_N2], b[FFT_N2:]
            kr, ki = kf[:FFT_N2], kf[FFT_N2:]
            y_ref[k1] = jnp.concatenate([br * kr - bi * ki, br * ki + bi * kr], axis=0).astype(BF16)
            return carry

        lax.fori_loop(0, FFT_N1, stage_b, 0, unroll=FFT_UNROLL)

        def stage_c(k1, carry):
            c = jnp.dot(ti_ref[k1], y_ref[k1], preferred_element_type=F32)
            a_ref[pl.ds(k1, FFT_N2, stride=FFT_PITCH), :] = c[:FFT_N2]
            a_ref[pl.ds(FFT_N1 + k1, FFT_N2, stride=FFT_PITCH), :] = c[FFT_N2:]
            return carry

        lax.fori_loop(0, FFT_N1, stage_c, 0, unroll=FFT_UNROLL)

        sk = sk_ref[0]

        def stage_d(p, carry):
            n2 = 2 * p
            cb = jnp.concatenate([a_ref[_slab_rows(n2), :], a_ref[_slab_rows(n2 + 1), :]], axis=1).astype(BF16)
            y = jnp.dot(fd_ref[...], cb, preferred_element_type=F32)
            for s in range(2):
                for bi in range(2):
                    yb = y[bi * FFT_NZ1:(bi + 1) * FFT_NZ1, s * LANES:(s + 1) * LANES]
                    write(bi, n2 + s, xm_ref[bi, n2 + s].astype(F32) * (yb + sk * u_ref[bi, n2 + s].astype(F32)))
            return carry

        lax.fori_loop(0, FFT_N2 // 2, stage_d, 0, unroll=min(FFT_UNROLL // 2, FFT_N2 // 2))

    def write_z1(bi, n2, z):
        z1_ref[bi, n2] = z.astype(z1_ref.dtype)

    def write_out(bi, n2, z):
        o_ref[bi, pl.ds(n2, FFT_NZ1, stride=FFT_N2), :] = z

    order = pl.program_id(1)

    @pl.when(order == 0)
    def _():
        conv(v_ref, write_z1)

    @pl.when(order == 1)
    def _():
        conv(z1_ref, write_out)


def _hyena_conv(uc, kf, skip, consts):
    fa, _, t, ti, fd = consts
    ncb = HYENA_WIDTH // LANES
    slab_block = (BATCH, FFT_N2, FFT_NZ1, LANES)
    return pl.pallas_call(
        _hyena_conv_kernel,
        grid=(ncb, HYENA_ORDER),
        in_specs=[pl.BlockSpec(slab_block, lambda c, o: (0, 0, 0, c)),
                  pl.BlockSpec(slab_block, lambda c, o: (0, 0, 0, (o + 1) * ncb + c)),
                  pl.BlockSpec((1, FFT_N1, 2 * FFT_N2, LANES), lambda c, o: (o, 0, 0, c)),
                  pl.BlockSpec((1, 1, LANES), lambda c, o: (o, 0, c)),
                  _const_spec(fa.shape), _const_spec(t.shape), _const_spec(ti.shape), _const_spec(fd.shape)],
        out_specs=pl.BlockSpec((BATCH, SEQ, LANES), lambda c, o: (0, 0, c)),
        out_shape=jax.ShapeDtypeStruct((BATCH, SEQ, HYENA_WIDTH), F32),
        scratch_shapes=[pltpu.VMEM((FFT_N2 * FFT_PITCH, LANES), F32),
                        pltpu.VMEM((FFT_N1, 2 * FFT_N2, LANES), BF16),
                        pltpu.VMEM(slab_block, BF16)],
        compiler_params=_params("arbitrary", "arbitrary", vmem_limit=CONV_VMEM_LIMIT),
        name="hyena_conv",
    )(uc, uc, kf, skip.reshape(HYENA_ORDER, 1, HYENA_WIDTH), fa, t, ti, fd)


MA_TM = 256
MA_VMEM_LIMIT = 60 * 1024 * 1024


def _mix_attn_kernel(x_ref, yc_ref, yh_ref, wa_ref, wb_ref, g_ref, wq_ref, k_ref, v_ref, wo_ref, *refs, n_cast):
    cast_in, o_ref, cast_out = refs[:n_cast], refs[n_cast], refs[n_cast + 1:2 * n_cast + 1]
    x1_ref, q_ref, ctx_ref = refs[2 * n_cast + 1:]
    _cast_blocks(cast_in, cast_out)
    mix = jnp.dot(yc_ref[0], wa_ref[...], preferred_element_type=F32)
    mix = mix + jnp.dot(yh_ref[0].astype(BF16), wb_ref[...], preferred_element_type=F32)
    x1 = x_ref[0] + mix
    x1_ref[...] = x1
    q_ref[...] = jnp.dot(_rms(x1, g_ref[...]).astype(BF16), wq_ref[...],
                         preferred_element_type=F32).astype(BF16)
    hd = XATTN_HEAD_DIM
    for h in range(XATTN_HEADS):
        hs = slice(h * hd, (h + 1) * hd)
        s = lax.dot_general(q_ref[:, hs], k_ref[0, :, hs], (((1,), (1,)), ((), ())),
                            preferred_element_type=F32) * (hd ** -0.5)
        s = s - jnp.max(s, axis=-1, keepdims=True)
        e = jnp.exp(s)
        pr = e / jnp.sum(e, axis=-1, keepdims=True)
        ctx_ref[:, hs] = jnp.dot(pr.astype(BF16), v_ref[0, :, hs], preferred_element_type=F32).astype(BF16)
    o_ref[0] = x1_ref[...] + jnp.dot(ctx_ref[...], wo_ref[...], preferred_element_type=F32)


def _mix_attn(x, yc, yh, w_out, g, wq, k, v, wo, cast_windows):
    b, l, d = x.shape
    tm = MA_TM
    kc = yc.shape[-1]
    grid = (b, l // tm)
    tok = lambda width: pl.BlockSpec((1, tm, width), lambda bi, i: (bi, i, 0))
    mem = pl.BlockSpec((1, N_MEM, d), lambda bi, i: (bi, 0, 0))
    cast_weights, cast_in, cast_out, cast_shapes = _cast_specs(cast_windows, grid)
    outs = pl.pallas_call(
        functools.partial(_mix_attn_kernel, n_cast=len(cast_weights)),
        grid=grid,
        in_specs=[tok(d), tok(kc), tok(kc),
                  _const_spec((kc, d), (0, 0)), _const_spec((kc, d), (1, 0)),
                  pl.BlockSpec((1, d), lambda bi, i: (0, 0)),
                  _const_spec((d, d)), mem, mem, _const_spec((d, d))] + cast_in,
        out_specs=[tok(d)] + cast_out,
        out_shape=[jax.ShapeDtypeStruct((b, l, d), F32)] + cast_shapes,
        scratch_shapes=[pltpu.VMEM((tm, d), F32), pltpu.VMEM((tm, d), BF16), pltpu.VMEM((tm, d), BF16)],
        compiler_params=_params("arbitrary", "arbitrary", vmem_limit=MA_VMEM_LIMIT),
        name="mix_attn",
    )(x, yc, yh, w_out, w_out, g.reshape(1, d), wq, k, v, wo, *cast_weights)
    return outs[0], outs[1:]


FFN_TM = 1024
FFN_FC = 512
FFN_VMEM_LIMIT = 62 * 1024 * 1024
FFN_HALO = 16


def _ffn_kernel(x_ref, xp_ref, xn_ref, g_ref, wg_ref, wv_ref, cw_ref, cb_ref, wd_ref, gf_ref,
                o_ref, h_ref, ge_ref):
    i = pl.program_id(1)
    j = pl.program_id(2)
    nt = pl.num_programs(1)
    nf = pl.num_programs(2)
    tm, hl = FFN_TM, FFN_HALO

    @pl.when(j == 0)
    def _():
        g = g_ref[...]
        h_ref[0:hl, :] = (_rms(xp_ref[0], g) * (i > 0).astype(F32)).astype(BF16)
        h_ref[hl:hl + tm, :] = _rms(x_ref[0], g).astype(BF16)
        h_ref[hl + tm:hl + tm + hl, :] = (_rms(xn_ref[0], g) * (i < nt - 1).astype(F32)).astype(BF16)
        o_ref[...] = jnp.zeros_like(o_ref)

    ge_ref[...] = jnp.dot(h_ref[...], wg_ref[...], preferred_element_type=F32)
    val = jnp.dot(h_ref[pl.ds(hl, tm), :], wv_ref[...], preferred_element_type=F32)
    gate = (cw_ref[0:1, :] * ge_ref[pl.ds(hl - 1, tm), :]
            + cw_ref[1:2, :] * ge_ref[pl.ds(hl, tm), :]
            + cw_ref[2:3, :] * ge_ref[pl.ds(hl + 1, tm), :]
            + cb_ref[...])
    act = (gate * _sigmoid(gate) * val).astype(BF16)
    o_ref[0] += jnp.dot(act, wd_ref[...], preferred_element_type=F32)

    @pl.when(j == nf - 1)
    def _():
        o_ref[0] = _rms(x_ref[0] + o_ref[0], gf_ref[...])


def _ffn(x, g_ffn, w_up, dw_w, dw_b, w_down, g_final):
    b, l, d = x.shape
    tm, fc, hl = FFN_TM, FFN_FC, FFN_HALO
    nf = D_FF // fc
    hb = tm // hl
    last_hb = l // hl - 1
    return pl.pallas_call(
        _ffn_kernel,
        grid=(b, l // tm, nf),
        in_specs=[pl.BlockSpec((1, tm, d), lambda bi, i, j: (bi, i, 0)),
                  pl.BlockSpec((1, hl, d), lambda bi, i, j: (bi, jnp.maximum(i * hb - 1, 0), 0)),
                  pl.BlockSpec((1, hl, d), lambda bi, i, j: (bi, jnp.minimum((i + 1) * hb, last_hb), 0)),
                  pl.BlockSpec((1, d), lambda bi, i, j: (0, 0)),
                  pl.BlockSpec((d, fc), lambda bi, i, j: (0, j)),
                  pl.BlockSpec((d, fc), lambda bi, i, j: (0, nf + j)),
                  pl.BlockSpec((3, fc), lambda bi, i, j: (0, j)),
                  pl.BlockSpec((1, fc), lambda bi, i, j: (0, j)),
                  pl.BlockSpec((fc, d), lambda bi, i, j: (j, 0)),
                  pl.BlockSpec((1, d), lambda bi, i, j: (0, 0))],
        out_specs=pl.BlockSpec((1, tm, d), lambda bi, i, j: (bi, i, 0)),
        out_shape=jax.ShapeDtypeStruct((b, l, d), F32),
        scratch_shapes=[pltpu.VMEM((tm + 2 * hl, d), BF16),
                        pltpu.VMEM((tm + 2 * hl, fc), F32)],
        compiler_params=_params("parallel", "parallel", "arbitrary", vmem_limit=FFN_VMEM_LIMIT),
        name="ffn",
    )(x, x, x, g_ffn.reshape(1, d), w_up, w_up, dw_w, dw_b.reshape(1, -1), w_down, g_final.reshape(1, d))


def kernel(x, mem, g_mix, w_in, conf_dw_w, conf_dw_b, conf_ln_g, conf_ln_b, hyena_short_w, hyena_short_b, hyena_w1, hyena_b1, hyena_w_inner, hyena_b_inner, hyena_w_filt, hyena_sin_freq, hyena_skip, w_out, g_xattn, g_mem, w_q, w_k, w_v, w_o, g_ffn, w_ffn_up, ffn_dw_w, ffn_dw_b, w_ffn_down, g_final):
    b, l, d = x.shape
    consts = _dft_constants()

    hmlp = _filter_mlp(hyena_w1[0], hyena_b1[0], hyena_w_inner[0], hyena_b_inner[0], hyena_sin_freq[0])
    kf, (w_in_b, w_out_b, w_q_b, w_o_b) = _filter_spectrum(
        hmlp, hyena_w_filt[0], consts[1], consts[2], [w_in[0], w_out[0], w_q[0], w_o[0]])

    u_conf, hn = _conf_in(x.reshape(b * l, d), g_mix[0], w_in_b)
    y_conf = _conformer(u_conf.reshape(b, l, CONF_WIDTH), conf_dw_w[0], conf_dw_b[0], conf_ln_g[0],
                        conf_ln_b[0])

    uc = _hyena_in(hn.reshape(b, l, d), w_in_b, hyena_short_w[0], hyena_short_b[0])
    z2 = _hyena_conv(uc, kf, hyena_skip[0], consts)

    memf = mem.reshape(b * N_MEM, d)
    k = _norm_matmul(memf, g_mem[0], w_k[0], b * N_MEM, 1024, BF16).reshape(b, N_MEM, d)
    v = _norm_matmul(memf, g_mem[0], w_v[0], b * N_MEM, 1024, BF16).reshape(b, N_MEM, d)
    x2, (w_up_b, w_down_b) = _mix_attn(x, y_conf, z2, w_out_b, g_xattn[0], w_q_b, k, v, w_o_b,
                                       [w_ffn_up[0], w_ffn_down[0]])

    return _ffn(x2, g_ffn[0], w_up_b, ffn_dw_w[0], ffn_dw_b[0], w_down_b, g_final)
```

```python
import functools
import math

import numpy as np
import jax
import jax.numpy as jnp
from jax import lax
from jax.experimental import pallas as pl
from jax.experimental.pallas import tpu as pltpu

F32 = jnp.float32
BF16 = jnp.bfloat16

D_MODEL = 2048
BATCH = 2
SEQ = 4096
CONF_WIDTH = 1024
HYENA_WIDTH = 1024
CONF_KERNEL = 31
SHORT_KERNEL = 3
HYENA_ORDER = 2
HYENA_EMB = 33
HYENA_FILTER_ORDER = 64
HYENA_INNER_MLPS = 2
HYENA_FAST_DECAY = 0.3
HYENA_SLOW_DECAY = 1.5
HYENA_TARGET = 1e-2
N_MEM = 256
XATTN_HEADS = 4
XATTN_HEAD_DIM = D_MODEL // XATTN_HEADS
D_FF = 5632
EPS = 1e-6
IN_COLS = 2 * CONF_WIDTH + (HYENA_ORDER + 1) * HYENA_WIDTH

LANES = 128
VMEM_LIMIT = 56 * 1024 * 1024

N_FFT = 2 * SEQ
FFT_N1 = 128
FFT_N2 = 64
FFT_NZ1 = SEQ // FFT_N2
FFT_PITCH = 2 * FFT_N1 + 8
FFT_UNROLL = 64
FILTER_UNROLL = 128
FILTER_RC = 512
FILTER_PITCH = FFT_N2 + 8
CONV_VMEM_LIMIT = 60 * 1024 * 1024


def _params(*sem, vmem_limit=VMEM_LIMIT):
    return pltpu.CompilerParams(dimension_semantics=sem, vmem_limit_bytes=vmem_limit)


def _const_spec(shape, index=None):
    index = (0,) * len(shape) if index is None else index
    return pl.BlockSpec(shape, lambda *_: index, pipeline_mode=pl.Buffered(1))


def _sigmoid(x):
    return 0.5 * (jnp.tanh(0.5 * x) + 1.0)


def _cast_specs(windows, grid_shape):
    nsteps = math.prod(grid_shape)

    def step(g):
        s = g[0]
        for gi, n in zip(g[1:], grid_shape[1:]):
            s = s * n + gi
        return s

    arrays, in_specs, out_specs, shapes = [], [], [], []
    for win in windows:
        w, width, cblk = win if isinstance(win, tuple) else (win, win.shape[1], 0)
        rows = w.shape[0]
        assert rows % (16 * nsteps) == 0 and (cblk + 1) * width <= w.shape[1], (w.shape, width, cblk, nsteps)
        arrays.append(w)
        in_specs.append(pl.BlockSpec((rows // nsteps, width), lambda *g, c=cblk: (step(g), c)))
        out_specs.append(pl.BlockSpec((rows // nsteps, width), lambda *g: (step(g), 0)))
        shapes.append(jax.ShapeDtypeStruct((rows, width), BF16))
    return arrays, in_specs, out_specs, shapes


def _cast_blocks(in_refs, out_refs):
    for i_ref, o_ref in zip(in_refs, out_refs):
        o_ref[...] = i_ref[...].astype(BF16)


def _rms(x, g):
    ms = jnp.mean(x * x, axis=-1, keepdims=True)
    return x * lax.rsqrt(ms + EPS) * g


@functools.lru_cache(maxsize=None)
def _dft_constants():
    k1 = np.arange(FFT_N1)
    f1 = np.exp(-2j * np.pi * np.outer(k1, k1) / FFT_N1)
    fh = f1[:, :FFT_NZ1]
    fa = np.block([[fh.real, -fh.imag], [fh.imag, fh.real]])
    perm = np.concatenate([np.arange(FFT_NZ1), FFT_N1 - 1 - np.arange(FFT_N1 - FFT_NZ1)])
    fp = f1[:, perm]
    fak = np.concatenate([fp.real, fp.imag], axis=0)
    n2 = np.arange(FFT_N2)
    kk = k1[:, None, None] + FFT_N1 * n2[None, :, None]
    ang = (n2[None, None, :] * kk) % N_FFT
    tc = np.exp(-2j * np.pi * ang / N_FFT)
    t = np.concatenate([np.concatenate([tc.real, -tc.imag], 2),
                        np.concatenate([tc.imag, tc.real], 2)], 1)
    g = np.conj(f1[:, :FFT_NZ1]).T / N_FFT
    fd = np.block([[g.real, -g.imag], [g.imag, g.real]])
    ti = np.transpose(t, (0, 2, 1))
    return tuple(jnp.asarray(a, dtype=BF16) for a in (fa, fak, t, ti, fd))


@functools.lru_cache(maxsize=None)
def _decay_rates():
    max_decay = math.log(HYENA_TARGET) / HYENA_FAST_DECAY
    min_decay = math.log(HYENA_TARGET) / HYENA_SLOW_DECAY
    return jnp.asarray(np.abs(np.linspace(min_decay, max_decay, HYENA_WIDTH, dtype=np.float32))[None, :])


def _norm_matmul_kernel(x_ref, g_ref, w_ref, o_ref, h_ref):
    @pl.when(pl.program_id(1) == 0)
    def _():
        h_ref[...] = _rms(x_ref[...], g_ref[...]).astype(BF16)

    o_ref[...] = jnp.dot(h_ref[...], w_ref[...].astype(BF16), preferred_element_type=F32).astype(o_ref.dtype)


def _norm_matmul(x, g, w, tm, tn, out_dtype):
    m, d = x.shape
    n = w.shape[1]
    return pl.pallas_call(
        _norm_matmul_kernel,
        grid=(m // tm, n // tn),
        in_specs=[pl.BlockSpec((tm, d), lambda i, j: (i, 0)),
                  pl.BlockSpec((1, d), lambda i, j: (0, 0)),
                  pl.BlockSpec((d, tn), lambda i, j: (0, j))],
        out_specs=pl.BlockSpec((tm, tn), lambda i, j: (i, j)),
        out_shape=jax.ShapeDtypeStruct((m, n), out_dtype),
        scratch_shapes=[pltpu.VMEM((tm, d), BF16)],
        compiler_params=_params("parallel", "arbitrary"),
        name="norm_matmul",
    )(x, g.reshape(1, d), w)


CIN_TM = 1024
CIN_TN = 1024


def _conf_in_kernel(x_ref, g_ref, w_ref, o_ref, h_ref):
    @pl.when(pl.program_id(1) == 0)
    def _():
        h_ref[...] = _rms(x_ref[...], g_ref[...]).astype(BF16)

    ag = jnp.dot(h_ref[...], w_ref[...], preferred_element_type=F32)
    o_ref[...] = (ag[:, :CIN_TN] * _sigmoid(ag[:, CIN_TN:])).astype(o_ref.dtype)


def _conf_in(x, g, w_in):
    m, d = x.shape
    tm, tn = CIN_TM, CIN_TN
    ncol = CONF_WIDTH // tn
    return pl.pallas_call(
        _conf_in_kernel,
        grid=(m // tm, ncol),
        in_specs=[pl.BlockSpec((tm, d), lambda i, j: (i, 0)),
                  pl.BlockSpec((1, d), lambda i, j: (0, 0)),
                  pl.BlockSpec((d, 2 * tn), lambda i, j: (0, j))],
        out_specs=[pl.BlockSpec((tm, tn), lambda i, j: (i, j)),
                   pl.BlockSpec((tm, d), lambda i, j: (i, 0))],
        out_shape=[jax.ShapeDtypeStruct((m, CONF_WIDTH), BF16), jax.ShapeDtypeStruct((m, d), BF16)],
        compiler_params=_params("parallel", "arbitrary"),
        name="conf_in",
    )(x, g.reshape(1, d), w_in)


CONF_T = 1024
CONF_HALO = 16
CONF_RC = 64


def _conformer_kernel(u_in, up_in, un_in, w_ref, b_ref, lg_ref, lb_ref, o_ref, u_ref, y_ref):
    i = pl.program_id(1)
    nt = pl.num_programs(1)
    t = CONF_T
    h = CONF_HALO
    prev_ok = (i > 0).astype(F32)
    next_ok = (i < nt - 1).astype(F32)
    nc = CONF_WIDTH // LANES
    for c in range(nc):
        cs = slice(c * LANES, (c + 1) * LANES)
        u_ref[c, 0:h, :] = up_in[0, :, cs].astype(F32) * prev_ok
        u_ref[c, h:h + t, :] = u_in[0, :, cs].astype(F32)
        u_ref[c, h + t:h + t + h, :] = un_in[0, :, cs].astype(F32) * next_ok

    pad = CONF_KERNEL // 2

    def conv_chunk(c, carry):
        w = w_ref[c]
        bias = b_ref[c]
        for r in range(t // CONF_RC):
            r0 = r * CONF_RC
            acc = jnp.zeros((CONF_RC, LANES), F32) + bias
            for j in range(CONF_KERNEL):
                acc = acc + w[j:j + 1, :] * u_ref[c, r0 + h - pad + j:r0 + h - pad + j + CONF_RC, :]
            y_ref[c, r0:r0 + CONF_RC, :] = acc
        return carry

    lax.fori_loop(0, nc, conv_chunk, 0)

    def ln_chunk(r, carry):
        r0 = pl.multiple_of(r * CONF_RC, CONF_RC)
        ys = [y_ref[c, pl.ds(r0, CONF_RC), :] for c in range(nc)]
        tot = ys[0]
        for y in ys[1:]:
            tot = tot + y
        mu = jnp.sum(tot, axis=-1, keepdims=True) * (1.0 / CONF_WIDTH)
        yc = [y - mu for y in ys]
        sq = yc[0] * yc[0]
        for y in yc[1:]:
            sq = sq + y * y
        inv = lax.rsqrt(jnp.sum(sq, axis=-1, keepdims=True) * (1.0 / CONF_WIDTH) + EPS)
        for c in range(nc):
            cs = slice(c * LANES, (c + 1) * LANES)
            yn = yc[c] * inv * lg_ref[:, cs] + lb_ref[:, cs]
            o_ref[0, pl.ds(r0, CONF_RC), cs] = (yn * _sigmoid(yn)).astype(o_ref.dtype)
        return carry

    lax.fori_loop(0, t // CONF_RC, ln_chunk, 0, unroll=8)


def _conformer(u, dw_w, dw_b, ln_g, ln_b):
    b, l, c = u.shape
    t, h = CONF_T, CONF_HALO
    nt = l // t
    hb = t // h
    last_hb = l // h - 1
    vec = pl.BlockSpec((1, c), lambda bi, i: (0, 0))
    nc = c // LANES
    chunked = lambda rows: pl.BlockSpec((nc, rows, LANES), lambda bi, i: (0, 0, 0))
    w3 = dw_w.reshape(CONF_KERNEL, nc, LANES).transpose(1, 0, 2)
    return pl.pallas_call(
        _conformer_kernel,
        grid=(b, nt),
        in_specs=[pl.BlockSpec((1, t, c), lambda bi, i: (bi, i, 0)),
                  pl.BlockSpec((1, h, c), lambda bi, i: (bi, jnp.maximum(i * hb - 1, 0), 0)),
                  pl.BlockSpec((1, h, c), lambda bi, i: (bi, jnp.minimum((i + 1) * hb, last_hb), 0)),
                  chunked(CONF_KERNEL), chunked(1), vec, vec],
        out_specs=pl.BlockSpec((1, t, c), lambda bi, i: (bi, i, 0)),
        out_shape=jax.ShapeDtypeStruct((b, l, c), BF16),
        scratch_shapes=[pltpu.VMEM((nc, t + 2 * h, LANES), F32), pltpu.VMEM((nc, t, LANES), F32)],
        compiler_params=_params("parallel", "parallel"),
        name="conformer",
    )(u, u, u, w3, dw_b.reshape(nc, 1, LANES), ln_g.reshape(1, c), ln_b.reshape(1, c))


HIN_TM = 1024
HIN_TN = 1024
HIN_HALO = 16
HIN_N1 = HIN_TM // FFT_N2
HIN_PITCH = FFT_N2 + 8


def _hyena_in_kernel(hc_ref, hp_ref, hn_ref, w_ref, cw_ref, cb_ref, o_ref, h_ref, s_ref):
    i = pl.program_id(1)
    j = pl.program_id(2)
    nt = pl.num_programs(1)
    tm, hl = HIN_TM, HIN_HALO

    @pl.when(j == 0)
    def _():
        h_ref[0:hl, :] = jnp.where(i > 0, hp_ref[0], jnp.zeros_like(hp_ref[0]))
        h_ref[hl:hl + tm, :] = hc_ref[0]
        h_ref[hl + tm:hl + tm + hl, :] = jnp.where(i < nt - 1, hn_ref[0], jnp.zeros_like(hn_ref[0]))

    e = jnp.dot(h_ref[...], w_ref[...], preferred_element_type=F32)
    nrows = tm + 2 * hl
    conv = (cw_ref[0:1, :] * pltpu.roll(e, 1, axis=0) + cw_ref[1:2, :] * e
            + cw_ref[2:3, :] * pltpu.roll(e, nrows - 1, axis=0) + cb_ref[...])
    nlc = HIN_TN // LANES
    for r in range(HIN_N1):
        r0 = hl + r * FFT_N2
        for lc in range(nlc):
            s_ref[lc, r * HIN_PITCH:r * HIN_PITCH + FFT_N2, :] = conv[r0:r0 + FFT_N2, lc * LANES:(lc + 1) * LANES]

    def gather(n2, carry):
        for lc in range(nlc):
            rows = s_ref[lc, pl.ds(n2, HIN_N1, stride=HIN_PITCH), :]
            o_ref[0, n2, :, lc * LANES:(lc + 1) * LANES] = rows.astype(o_ref.dtype)
        return carry

    lax.fori_loop(0, FFT_N2, gather, 0, unroll=8)


def _hyena_in(hn, w_in, cw, cb):
    b, l, d = hn.shape
    tm, tn, hl = HIN_TM, HIN_TN, HIN_HALO
    ncol = (HYENA_ORDER + 1) * HYENA_WIDTH // tn
    col0 = 2 * CONF_WIDTH // tn
    hb = tm // hl
    last_hb = l // hl - 1
    return pl.pallas_call(
        _hyena_in_kernel,
        grid=(b, l // tm, ncol),
        in_specs=[pl.BlockSpec((1, tm, d), lambda bi, i, j: (bi, i, 0)),
                  pl.BlockSpec((1, hl, d), lambda bi, i, j: (bi, jnp.maximum(i * hb - 1, 0), 0)),
                  pl.BlockSpec((1, hl, d), lambda bi, i, j: (bi, jnp.minimum((i + 1) * hb, last_hb), 0)),
                  pl.BlockSpec((d, tn), lambda bi, i, j: (0, col0 + j)),
                  pl.BlockSpec((SHORT_KERNEL, tn), lambda bi, i, j: (0, j)),
                  pl.BlockSpec((1, tn), lambda bi, i, j: (0, j))],
        out_specs=pl.BlockSpec((1, FFT_N2, HIN_N1, tn), lambda bi, i, j: (bi, 0, i, j)),
        out_shape=jax.ShapeDtypeStruct((b, FFT_N2, FFT_NZ1, ncol * tn), BF16),
        scratch_shapes=[pltpu.VMEM((tm + 2 * hl, d), BF16),
                        pltpu.VMEM((tn // LANES, HIN_N1 * HIN_PITCH, LANES), F32)],
        compiler_params=_params("parallel", "parallel", "arbitrary"),
        name="hyena_in",
    )(hn, hn, hn, w_in, cw, cb.reshape(1, -1))


MLP_HW = LANES // 2
MLP_W1_ROWS = 40
MLP_WI_ROW0 = MLP_W1_ROWS
MLP_VEC_ROW0 = MLP_WI_ROW0 + HYENA_INNER_MLPS * HYENA_FILTER_ORDER
MLP_ROWS = MLP_VEC_ROW0 + 8


def _filter_mlp_kernel(z_ref, p_ref, o_ref):
    hi = lax.Precision.HIGHEST

    def block_diag(rows0, nrows):
        w = p_ref[rows0:rows0 + nrows, :]
        lo = lax.broadcasted_iota(jnp.int32, w.shape, 1) < MLP_HW
        zeros = jnp.zeros((MLP_HW - nrows, LANES), F32)
        parts = [jnp.where(lo, w, 0.0), zeros, jnp.where(lo, 0.0, w), zeros]
        return jnp.concatenate([p for p in parts if p.shape[0]], axis=0)

    vec = lambda r: p_ref[MLP_VEC_ROW0 + r:MLP_VEC_ROW0 + r + 1, :]
    fr = vec(1 + HYENA_INNER_MLPS)
    w1 = block_diag(0, MLP_W1_ROWS)
    h = jnp.sin(fr * (jnp.dot(z_ref[...], w1, precision=hi, preferred_element_type=F32) + vec(0)))
    for i in range(HYENA_INNER_MLPS):
        wi = block_diag(MLP_WI_ROW0 + i * HYENA_FILTER_ORDER, HYENA_FILTER_ORDER)
        h = jnp.sin(fr * (jnp.dot(h, wi, precision=hi, preferred_element_type=F32) + vec(1 + i)))
    o_ref[...] = h


@functools.lru_cache(maxsize=None)
def _position_features_packed():
    f32 = np.float32
    t = np.linspace(0.0, 1.0, SEQ, dtype=f32)[:, None]
    w = (f32(2.0 * math.pi) * np.arange(SEQ, dtype=f32)[:, None] / f32(SEQ)).astype(f32)
    bands = (HYENA_EMB - 1) // 2
    f = np.linspace(1e-4, bands - 1, bands, dtype=f32)[None, :]
    z = np.concatenate([t, np.cos(f * w), -np.sin(f * w)], axis=-1).astype(f32)
    zp = np.zeros((SEQ // 2, LANES), f32)
    zp[:, :HYENA_EMB] = z[:SEQ // 2]
    zp[:, MLP_HW:MLP_HW + HYENA_EMB] = z[SEQ // 2:]
    return jnp.asarray(zp)


def _filter_mlp(w1, b1, w_inner, b_inner, freq):
    both = lambda a: jnp.concatenate([a, a], axis=-1)
    params = jnp.concatenate(
        [both(w1), jnp.zeros((MLP_W1_ROWS - HYENA_EMB, LANES), F32)]
        + [both(w_inner[i]) for i in range(HYENA_INNER_MLPS)]
        + [both(b1)[None], both(b_inner), both(freq)[None],
           jnp.zeros((MLP_ROWS - MLP_VEC_ROW0 - 2 - HYENA_INNER_MLPS, LANES), F32)], axis=0)
    return pl.pallas_call(
        _filter_mlp_kernel,
        out_shape=jax.ShapeDtypeStruct((SEQ // 2, LANES), F32),
        compiler_params=pltpu.CompilerParams(vmem_limit_bytes=VMEM_LIMIT),
        name="filter_mlp",
    )(_position_features_packed(), params)


def _stage_b_rhs(a_ref, k1):
    ar = a_ref[pl.ds(k1, FFT_N2, stride=FFT_PITCH), :]
    ai = a_ref[pl.ds(FFT_N1 + k1, FFT_N2, stride=FFT_PITCH), :]
    return jnp.concatenate([ar, ai], axis=0).astype(BF16)


def _slab_rows(n2):
    return pl.ds(pl.multiple_of(n2 * FFT_PITCH, 8), 2 * FFT_N1)


def _filter_spectrum_kernel(h_ref, wf_ref, wb_ref, dl_ref, fak_ref, t_ref, *refs, n_cast):
    cast_in, o_ref, cast_out = refs[:n_cast], refs[n_cast], refs[n_cast + 1:2 * n_cast + 1]
    kt_ref, a_ref = refs[2 * n_cast + 1:]
    _cast_blocks(cast_in, cast_out)
    rc = FILTER_RC
    half_chunks = SEQ // 2 // rc
    w = jnp.concatenate([wf_ref[...], wb_ref[...]], axis=1)
    zeros = jnp.zeros((MLP_HW, 2 * LANES), F32)
    ws = [jnp.concatenate([w, zeros], axis=0).astype(BF16), jnp.concatenate([zeros, w], axis=0).astype(BF16)]
    dl = jnp.concatenate([dl_ref[...], dl_ref[...]], axis=1)

    def gen(r, ss, p):
        r0 = pl.multiple_of(r * rc, rc)
        rows = lax.broadcasted_iota(jnp.int32, (rc, 2 * LANES), 0) + r0
        lanes = lax.broadcasted_iota(jnp.int32, (rc, 2 * LANES), 1)
        decay = jnp.exp(-(rows.astype(F32) * (1.0 / (SEQ - 1))) * dl)
        h = h_ref[pl.ds(pl.multiple_of(r0 - p * (SEQ // 2), rc), rc), :]
        k = jnp.dot(h.astype(BF16), ws[p], preferred_element_type=F32)
        k = k * decay
        k = jnp.where((rows >= SEQ - 1) & (lanes >= LANES), 0.0, k)
        for n in range(rc // FFT_N2):
            dst = pl.ds(pl.multiple_of((r * (rc // FFT_N2) + n) * FILTER_PITCH, 8), FFT_N2)
            kt_ref[0, dst, :] = k[n * FFT_N2:(n + 1) * FFT_N2, :LANES]
            kt_ref[1, dst, :] = k[n * FFT_N2:(n + 1) * FFT_N2, LANES:]
        return ss + jnp.sum(k * k, axis=0, keepdims=True)

    ss = jnp.zeros((1, 2 * LANES), F32)
    for p in range(2):
        ss = lax.fori_loop(p * half_chunks, (p + 1) * half_chunks, functools.partial(gen, p=p), ss, unroll=4)
    scale = lax.rsqrt(ss[:, :LANES] + ss[:, LANES:] + EPS)

    def filter_slab(n2):
        fwd = kt_ref[0, pl.ds(n2, FFT_NZ1, stride=FILTER_PITCH), :]
        bwd = kt_ref[1, pl.ds(FFT_N2 - 1 - n2, FFT_NZ1, stride=FILTER_PITCH), :]
        return (jnp.concatenate([fwd, bwd], axis=0) * scale).astype(BF16)

    def stage_a(p, carry):
        n2 = 2 * p
        rhs = jnp.concatenate([filter_slab(n2), filter_slab(n2 + 1)], axis=1)
        a = jnp.dot(fak_ref[...], rhs, preferred_element_type=F32)
        a_ref[_slab_rows(n2), :] = a[:, :LANES]
        a_ref[_slab_rows(n2 + 1), :] = a[:, LANES:]
        return carry

    lax.fori_loop(0, FFT_N2 // 2, stage_a, 0, unroll=min(FFT_UNROLL // 2, FFT_N2 // 2))

    def stage_b(k1, carry):
        b = jnp.dot(t_ref[k1], _stage_b_rhs(a_ref, k1), preferred_element_type=F32)
        o_ref[0, k1] = b.astype(o_ref.dtype)
        return carry

    lax.fori_loop(0, FFT_N1, stage_b, 0, unroll=FILTER_UNROLL)


def _filter_spectrum(hmlp, w_filt, fak, t, cast_windows):
    assert HYENA_FILTER_ORDER == MLP_HW
    ncb = HYENA_WIDTH // LANES
    grid = (HYENA_ORDER, ncb)
    cast_weights, cast_in, cast_out, cast_shapes = _cast_specs(cast_windows, grid)
    w_spec = lambda direction: pl.BlockSpec((HYENA_FILTER_ORDER, LANES),
                                            lambda o, c: (0, (o * 2 + direction) * ncb + c))
    outs = pl.pallas_call(
        functools.partial(_filter_spectrum_kernel, n_cast=len(cast_weights)),
        grid=grid,
        in_specs=[_const_spec((SEQ // 2, LANES)), w_spec(0), w_spec(1),
                  pl.BlockSpec((1, LANES), lambda o, c: (0, c)),
                  _const_spec((2 * FFT_N1, FFT_N1)),
                  _const_spec((FFT_N1, 2 * FFT_N2, 2 * FFT_N2))] + cast_in,
        out_specs=[pl.BlockSpec((1, FFT_N1, 2 * FFT_N2, LANES), lambda o, c: (o, 0, 0, c))] + cast_out,
        out_shape=[jax.ShapeDtypeStruct((HYENA_ORDER, FFT_N1, 2 * FFT_N2, HYENA_WIDTH), BF16)] + cast_shapes,
        scratch_shapes=[pltpu.VMEM((2, FFT_NZ1 * FILTER_PITCH, LANES), F32),
                        pltpu.VMEM((FFT_N2 * FFT_PITCH, LANES), F32)],
        compiler_params=_params("arbitrary", "arbitrary"),
        name="filter_spectrum",
    )(hmlp, w_filt, w_filt, _decay_rates(), fak, t, *cast_weights)
    return outs[0], outs[1:]


def _hyena_conv_kernel(v_ref, xm_ref, kf_ref, sk_ref, fa_ref, t_ref, ti_ref, fd_ref, o_ref, a_ref, y_ref, z1_ref):
    def conv(u_ref, write):
        def stage_a(p, carry):
            n2 = 2 * p
            rhs = jnp.concatenate(
                [jnp.concatenate([u_ref[0, n], u_ref[1, n]], axis=0) for n in (n2, n2 + 1)], axis=1).astype(BF16)
            a = jnp.dot(fa_ref[...], rhs, preferred_element_type=F32)
            a_ref[_slab_rows(n2), :] = a[:, :LANES]
            a_ref[_slab_rows(n2 + 1), :] = a[:, LANES:]
            return carry

        lax.fori_loop(0, FFT_N2 // 2, stage_a, 0, unroll=min(FFT_UNROLL // 2, FFT_N2 // 2))

        def stage_b(k1, carry):
            b = jnp.dot(t_ref[k1], _stage_b_rhs(a_ref, k1), preferred_element_type=F32)
            kf = kf_ref[0, k1].astype(F32)
            br, bi = b[:FFT_N2], b[FFT_N2:]
            kr, ki = kf[:FFT_N2], kf[FFT_N2:]
            y_ref[k1] = jnp.concatenate([br * kr - bi * ki, br * ki + bi * kr], axis=0).astype(BF16)
            return carry

        lax.fori_loop(0, FFT_N1, stage_b, 0, unroll=FFT_UNROLL)

        def stage_c(k1, carry):
            c = jnp.dot(ti_ref[k1], y_ref[k1], preferred_element_type=F32)
            a_ref[pl.ds(k1, FFT_N2, stride=FFT_PITCH), :] = c[:FFT_N2]
            a_ref[pl.ds(FFT_N1 + k1, FFT_N2, stride=FFT_PITCH), :] = c[FFT_N2:]
            return carry

        lax.fori_loop(0, FFT_N1, stage_c, 0, unroll=FFT_UNROLL)

        sk = sk_ref[0]

        def stage_d(p, carry):
            n2 = 2 * p
            cb = jnp.concatenate([a_ref[_slab_rows(n2), :], a_ref[_slab_rows(n2 + 1), :]], axis=1).astype(BF16)
            y = jnp.dot(fd_ref[...], cb, preferred_element_type=F32)
            for s in range(2):
                for bi in range(2):
                    yb = y[bi * FFT_NZ1:(bi + 1) * FFT_NZ1, s * LANES:(s + 1) * LANES]
                    write(bi, n2 + s, xm_ref[bi, n2 + s].astype(F32) * (yb + sk * u_ref[bi, n2 + s].astype(F32)))
            return carry

        lax.fori_loop(0, FFT_N2 // 2, stage_d, 0, unroll=min(FFT_UNROLL // 2, FFT_N2 // 2))

    def write_z1(bi, n2, z):
        z1_ref[bi, n2] = z.astype(z1_ref.dtype)

    def write_out(bi, n2, z):
        o_ref[bi, pl.ds(n2, FFT_NZ1, stride=FFT_N2), :] = z

    order = pl.program_id(1)

    @pl.when(order == 0)
    def _():
        conv(v_ref, write_z1)

    @pl.when(order == 1)
    def _():
        conv(z1_ref, write_out)


def _hyena_conv(uc, kf, skip, consts):
    fa, _, t, ti, fd = consts
    ncb = HYENA_WIDTH // LANES
    slab_block = (BATCH, FFT_N2, FFT_NZ1, LANES)
    return pl.pallas_call(
        _hyena_conv_kernel,
        grid=(ncb, HYENA_ORDER),
        in_specs=[pl.BlockSpec(slab_block, lambda c, o: (0, 0, 0, c)),
                  pl.BlockSpec(slab_block, lambda c, o: (0, 0, 0, (o + 1) * ncb + c)),
                  pl.BlockSpec((1, FFT_N1, 2 * FFT_N2, LANES), lambda c, o: (o, 0, 0, c)),
                  pl.BlockSpec((1, 1, LANES), lambda c, o: (o, 0, c)),
                  _const_spec(fa.shape), _const_spec(t.shape), _const_spec(ti.shape), _const_spec(fd.shape)],
        out_specs=pl.BlockSpec((BATCH, SEQ, LANES), lambda c, o: (0, 0, c)),
        out_shape=jax.ShapeDtypeStruct((BATCH, SEQ, HYENA_WIDTH), F32),
        scratch_shapes=[pltpu.VMEM((FFT_N2 * FFT_PITCH, LANES), F32),
                        pltpu.VMEM((FFT_N1, 2 * FFT_N2, LANES), BF16),
                        pltpu.VMEM(slab_block, BF16)],
        compiler_params=_params("arbitrary", "arbitrary", vmem_limit=CONV_VMEM_LIMIT),
        name="hyena_conv",
    )(uc, uc, kf, skip.reshape(HYENA_ORDER, 1, HYENA_WIDTH), fa, t, ti, fd)


MA_TM = 256
MA_VMEM_LIMIT = 60 * 1024 * 1024


def _mix_attn_kernel(x_ref, yc_ref, yh_ref, wa_ref, wb_ref, g_ref, wq_ref, k_ref, v_ref, wo_ref, *refs, n_cast):
    cast_in, o_ref, cast_out = refs[:n_cast], refs[n_cast], refs[n_cast + 1:2 * n_cast + 1]
    x1_ref, q_ref, ctx_ref = refs[2 * n_cast + 1:]
    _cast_blocks(cast_in, cast_out)
    mix = jnp.dot(yc_ref[0], wa_ref[...], preferred_element_type=F32)
    mix = mix + jnp.dot(yh_ref[0].astype(BF16), wb_ref[...], preferred_element_type=F32)
    x1 = x_ref[0] + mix
    x1_ref[...] = x1
    q_ref[...] = jnp.dot(_rms(x1, g_ref[...]).astype(BF16), wq_ref[...],
                         preferred_element_type=F32).astype(BF16)
    hd = XATTN_HEAD_DIM
    for h in range(XATTN_HEADS):
        hs = slice(h * hd, (h + 1) * hd)
        s = lax.dot_general(q_ref[:, hs], k_ref[0, :, hs], (((1,), (1,)), ((), ())),
                            preferred_element_type=F32) * (hd ** -0.5)
        s = s - jnp.max(s, axis=-1, keepdims=True)
        e = jnp.exp(s)
        pr = e / jnp.sum(e, axis=-1, keepdims=True)
        ctx_ref[:, hs] = jnp.dot(pr.astype(BF16), v_ref[0, :, hs], preferred_element_type=F32).astype(BF16)
    o_ref[0] = x1_ref[...] + jnp.dot(ctx_ref[...], wo_ref[...], preferred_element_type=F32)


def _mix_attn(x, yc, yh, w_out, g, wq, k, v, wo, cast_windows):
    b, l, d = x.shape
    tm = MA_TM
    kc = yc.shape[-1]
    grid = (b, l // tm)
    tok = lambda width: pl.BlockSpec((1, tm, width), lambda bi, i: (bi, i, 0))
    mem = pl.BlockSpec((1, N_MEM, d), lambda bi, i: (bi, 0, 0))
    cast_weights, cast_in, cast_out, cast_shapes = _cast_specs(cast_windows, grid)
    outs = pl.pallas_call(
        functools.partial(_mix_attn_kernel, n_cast=len(cast_weights)),
        grid=grid,
        in_specs=[tok(d), tok(kc), tok(kc),
                  _const_spec((kc, d), (0, 0)), _const_spec((kc, d), (1, 0)),
                  pl.BlockSpec((1, d), lambda bi, i: (0, 0)),
                  _const_spec((d, d)), mem, mem, _const_spec((d, d))] + cast_in,
        out_specs=[tok(d)] + cast_out,
        out_shape=[jax.ShapeDtypeStruct((b, l, d), F32)] + cast_shapes,
        scratch_shapes=[pltpu.VMEM((tm, d), F32), pltpu.VMEM((tm, d), BF16), pltpu.VMEM((tm, d), BF16)],
        compiler_params=_params("arbitrary", "arbitrary", vmem_limit=MA_VMEM_LIMIT),
        name="mix_attn",
    )(x, yc, yh, w_out, w_out, g.reshape(1, d), wq, k, v, wo, *cast_weights)
    return outs[0], outs[1:]


FFN_TM = 1024
FFN_FC = 512
FFN_VMEM_LIMIT = 62 * 1024 * 1024
FFN_HALO = 16


def _ffn_kernel(x_ref, xp_ref, xn_ref, g_ref, wg_ref, wv_ref, cw_ref, cb_ref, wd_ref, gf_ref,
                o_ref, h_ref):
    i = pl.program_id(1)
    j = pl.program_id(2)
    nt = pl.num_programs(1)
    nf = pl.num_programs(2)
    tm, hl = FFN_TM, FFN_HALO

    @pl.when(j == 0)
    def _():
        g = g_ref[...]
        h_ref[0:hl, :] = (_rms(xp_ref[0], g) * (i > 0).astype(F32)).astype(BF16)
        h_ref[hl:hl + tm, :] = _rms(x_ref[0], g).astype(BF16)
        h_ref[hl + tm:hl + tm + hl, :] = (_rms(xn_ref[0], g) * (i < nt - 1).astype(F32)).astype(BF16)
        o_ref[...] = jnp.zeros_like(o_ref)

    ge = jnp.dot(h_ref[...], wg_ref[...], preferred_element_type=F32)
    val = jnp.dot(h_ref[pl.ds(hl, tm), :], wv_ref[...], preferred_element_type=F32)
    nrows = tm + 2 * hl
    gate = (cw_ref[0:1, :] * pltpu.roll(ge, 1, axis=0) + cw_ref[1:2, :] * ge
            + cw_ref[2:3, :] * pltpu.roll(ge, nrows - 1, axis=0) + cb_ref[...])[hl:hl + tm]
    act = (gate * _sigmoid(gate) * val).astype(BF16)
    o_ref[0] += jnp.dot(act, wd_ref[...], preferred_element_type=F32)

    @pl.when(j == nf - 1)
    def _():
        o_ref[0] = _rms(x_ref[0] + o_ref[0], gf_ref[...])


def _ffn(x, g_ffn, w_up, dw_w, dw_b, w_down, g_final):
    b, l, d = x.shape
    tm, fc, hl = FFN_TM, FFN_FC, FFN_HALO
    nf = D_FF // fc
    hb = tm // hl
    last_hb = l // hl - 1
    return pl.pallas_call(
        _ffn_kernel,
        grid=(b, l // tm, nf),
        in_specs=[pl.BlockSpec((1, tm, d), lambda bi, i, j: (bi, i, 0)),
                  pl.BlockSpec((1, hl, d), lambda bi, i, j: (bi, jnp.maximum(i * hb - 1, 0), 0)),
                  pl.BlockSpec((1, hl, d), lambda bi, i, j: (bi, jnp.minimum((i + 1) * hb, last_hb), 0)),
                  pl.BlockSpec((1, d), lambda bi, i, j: (0, 0)),
                  pl.BlockSpec((d, fc), lambda bi, i, j: (0, j)),
                  pl.BlockSpec((d, fc), lambda bi, i, j: (0, nf + j)),
                  pl.BlockSpec((3, fc), lambda bi, i, j: (0, j)),
                  pl.BlockSpec((1, fc), lambda bi, i, j: (0, j)),
                  pl.BlockSpec((fc, d), lambda bi, i, j: (j, 0)),
                  pl.BlockSpec((1, d), lambda bi, i, j: (0, 0))],
        out_specs=pl.BlockSpec((1, tm, d), lambda bi, i, j: (bi, i, 0)),
        out_shape=jax.ShapeDtypeStruct((b, l, d), F32),
        scratch_shapes=[pltpu.VMEM((tm + 2 * hl, d), BF16)],
        compiler_params=_params("parallel", "parallel", "arbitrary", vmem_limit=FFN_VMEM_LIMIT),
        name="ffn",
    )(x, x, x, g_ffn.reshape(1, d), w_up, w_up, dw_w, dw_b.reshape(1, -1), w_down, g_final.reshape(1, d))


def kernel(x, mem, g_mix, w_in, conf_dw_w, conf_dw_b, conf_ln_g, conf_ln_b, hyena_short_w, hyena_short_b, hyena_w1, hyena_b1, hyena_w_inner, hyena_b_inner, hyena_w_filt, hyena_sin_freq, hyena_skip, w_out, g_xattn, g_mem, w_q, w_k, w_v, w_o, g_ffn, w_ffn_up, ffn_dw_w, ffn_dw_b, w_ffn_down, g_final):
    b, l, d = x.shape
    consts = _dft_constants()

    hmlp = _filter_mlp(hyena_w1[0], hyena_b1[0], hyena_w_inner[0], hyena_b_inner[0], hyena_sin_freq[0])
    kf, (w_in_b, w_out_b, w_q_b, w_o_b) = _filter_spectrum(
        hmlp, hyena_w_filt[0], consts[1], consts[2], [w_in[0], w_out[0], w_q[0], w_o[0]])

    u_conf, hn = _conf_in(x.reshape(b * l, d), g_mix[0], w_in_b)
    y_conf = _conformer(u_conf.reshape(b, l, CONF_WIDTH), conf_dw_w[0], conf_dw_b[0], conf_ln_g[0],
                        conf_ln_b[0])

    uc = _hyena_in(hn.reshape(b, l, d), w_in_b, hyena_short_w[0], hyena_short_b[0])
    z2 = _hyena_conv(uc, kf, hyena_skip[0], consts)

    memf = mem.reshape(b * N_MEM, d)
    k = _norm_matmul(memf, g_mem[0], w_k[0], b * N_MEM, 1024, BF16).reshape(b, N_MEM, d)
    v = _norm_matmul(memf, g_mem[0], w_v[0], b * N_MEM, 1024, BF16).reshape(b, N_MEM, d)
    x2, (w_up_b, w_down_b) = _mix_attn(x, y_conf, z2, w_out_b, g_xattn[0], w_q_b, k, v, w_o_b,
                                       [w_ffn_up[0], w_ffn_down[0]])

    return _ffn(x2, g_ffn[0], w_up_b, ffn_dw_w[0], ffn_dw_b[0], w_down_b, g_final)
```

```python
import functools
import math

import numpy as np
import jax
import jax.numpy as jnp
from jax import lax
from jax.experimental import pallas as pl
from jax.experimental.pallas import tpu as pltpu

F32 = jnp.float32
BF16 = jnp.bfloat16

D_MODEL = 2048
BATCH = 2
SEQ = 4096
CONF_WIDTH = 1024
HYENA_WIDTH = 1024
CONF_KERNEL = 31
SHORT_KERNEL = 3
HYENA_ORDER = 2
HYENA_EMB = 33
HYENA_FILTER_ORDER = 64
HYENA_INNER_MLPS = 2
HYENA_FAST_DECAY = 0.3
HYENA_SLOW_DECAY = 1.5
HYENA_TARGET = 1e-2
N_MEM = 256
XATTN_HEADS = 4
XATTN_HEAD_DIM = D_MODEL // XATTN_HEADS
D_FF = 5632
EPS = 1e-6
IN_COLS = 2 * CONF_WIDTH + (HYENA_ORDER + 1) * HYENA_WIDTH

LANES = 128
VMEM_LIMIT = 56 * 1024 * 1024

N_FFT = 2 * SEQ
FFT_N1 = 128
FFT_N2 = 64
FFT_NZ1 = SEQ // FFT_N2
FFT_PITCH = 2 * FFT_N1 + 8
FFT_UNROLL = 64
FILTER_UNROLL = 128
FILTER_RC = 512
FILTER_PITCH = FFT_N2 + 8


def _params(*sem, vmem_limit=VMEM_LIMIT):
    return pltpu.CompilerParams(dimension_semantics=sem, vmem_limit_bytes=vmem_limit)


def _const_spec(shape, index=None):
    index = (0,) * len(shape) if index is None else index
    return pl.BlockSpec(shape, lambda *_: index, pipeline_mode=pl.Buffered(1))


def _sigmoid(x):
    return 0.5 * (jnp.tanh(0.5 * x) + 1.0)


def _cast_specs(windows, grid_shape):
    nsteps = math.prod(grid_shape)

    def step(g):
        s = g[0]
        for gi, n in zip(g[1:], grid_shape[1:]):
            s = s * n + gi
        return s

    arrays, in_specs, out_specs, shapes = [], [], [], []
    for win in windows:
        w, width, cblk = win if isinstance(win, tuple) else (win, win.shape[1], 0)
        rows = w.shape[0]
        assert rows % (16 * nsteps) == 0 and (cblk + 1) * width <= w.shape[1], (w.shape, width, cblk, nsteps)
        arrays.append(w)
        in_specs.append(pl.BlockSpec((rows // nsteps, width), lambda *g, c=cblk: (step(g), c)))
        out_specs.append(pl.BlockSpec((rows // nsteps, width), lambda *g: (step(g), 0)))
        shapes.append(jax.ShapeDtypeStruct((rows, width), BF16))
    return arrays, in_specs, out_specs, shapes


def _cast_blocks(in_refs, out_refs):
    for i_ref, o_ref in zip(in_refs, out_refs):
        o_ref[...] = i_ref[...].astype(BF16)


def _rms(x, g):
    ms = jnp.mean(x * x, axis=-1, keepdims=True)
    return x * lax.rsqrt(ms + EPS) * g


@functools.lru_cache(maxsize=None)
def _dft_constants():
    k1 = np.arange(FFT_N1)
    f1 = np.exp(-2j * np.pi * np.outer(k1, k1) / FFT_N1)
    fh = f1[:, :FFT_NZ1]
    fa = np.block([[fh.real, -fh.imag], [fh.imag, fh.real]])
    perm = np.concatenate([np.arange(FFT_NZ1), FFT_N1 - 1 - np.arange(FFT_N1 - FFT_NZ1)])
    fp = f1[:, perm]
    fak = np.concatenate([fp.real, fp.imag], axis=0)
    n2 = np.arange(FFT_N2)
    kk = k1[:, None, None] + FFT_N1 * n2[None, :, None]
    ang = (n2[None, None, :] * kk) % N_FFT
    tc = np.exp(-2j * np.pi * ang / N_FFT)
    t = np.concatenate([np.concatenate([tc.real, -tc.imag], 2),
                        np.concatenate([tc.imag, tc.real], 2)], 1)
    g = np.conj(f1[:, :FFT_NZ1]).T / N_FFT
    fd = np.block([[g.real, -g.imag], [g.imag, g.real]])
    ti = np.transpose(t, (0, 2, 1))
    return tuple(jnp.asarray(a, dtype=BF16) for a in (fa, fak, t, ti, fd))


@functools.lru_cache(maxsize=None)
def _decay_rates():
    max_decay = math.log(HYENA_TARGET) / HYENA_FAST_DECAY
    min_decay = math.log(HYENA_TARGET) / HYENA_SLOW_DECAY
    return jnp.asarray(np.abs(np.linspace(min_decay, max_decay, HYENA_WIDTH, dtype=np.float32))[None, :])


def _norm_matmul_kernel(x_ref, g_ref, w_ref, o_ref, h_ref):
    @pl.when(pl.program_id(1) == 0)
    def _():
        h_ref[...] = _rms(x_ref[...], g_ref[...]).astype(BF16)

    o_ref[...] = jnp.dot(h_ref[...], w_ref[...].astype(BF16), preferred_element_type=F32).astype(o_ref.dtype)


def _norm_matmul(x, g, w, tm, tn, out_dtype):
    m, d = x.shape
    n = w.shape[1]
    return pl.pallas_call(
        _norm_matmul_kernel,
        grid=(m // tm, n // tn),
        in_specs=[pl.BlockSpec((tm, d), lambda i, j: (i, 0)),
                  pl.BlockSpec((1, d), lambda i, j: (0, 0)),
                  pl.BlockSpec((d, tn), lambda i, j: (0, j))],
        out_specs=pl.BlockSpec((tm, tn), lambda i, j: (i, j)),
        out_shape=jax.ShapeDtypeStruct((m, n), out_dtype),
        scratch_shapes=[pltpu.VMEM((tm, d), BF16)],
        compiler_params=_params("parallel", "arbitrary"),
        name="norm_matmul",
    )(x, g.reshape(1, d), w)


CIN_TM = 1024
CIN_TN = 1024


def _conf_in_kernel(x_ref, g_ref, w_ref, o_ref, h_ref):
    @pl.when(pl.program_id(1) == 0)
    def _():
        h_ref[...] = _rms(x_ref[...], g_ref[...]).astype(BF16)

    ag = jnp.dot(h_ref[...], w_ref[...], preferred_element_type=F32)
    o_ref[...] = (ag[:, :CIN_TN] * _sigmoid(ag[:, CIN_TN:])).astype(o_ref.dtype)


def _conf_in(x, g, w_in):
    m, d = x.shape
    tm, tn = CIN_TM, CIN_TN
    ncol = CONF_WIDTH // tn
    return pl.pallas_call(
        _conf_in_kernel,
        grid=(m // tm, ncol),
        in_specs=[pl.BlockSpec((tm, d), lambda i, j: (i, 0)),
                  pl.BlockSpec((1, d), lambda i, j: (0, 0)),
                  pl.BlockSpec((d, 2 * tn), lambda i, j: (0, j))],
        out_specs=[pl.BlockSpec((tm, tn), lambda i, j: (i, j)),
                   pl.BlockSpec((tm, d), lambda i, j: (i, 0))],
        out_shape=[jax.ShapeDtypeStruct((m, CONF_WIDTH), BF16), jax.ShapeDtypeStruct((m, d), BF16)],
        compiler_params=_params("parallel", "arbitrary"),
        name="conf_in",
    )(x, g.reshape(1, d), w_in)


CONF_T = 1024
CONF_HALO = 16
CONF_RC = 64


def _conformer_kernel(u_in, up_in, un_in, w_ref, b_ref, lg_ref, lb_ref, o_ref, u_ref, y_ref):
    i = pl.program_id(1)
    nt = pl.num_programs(1)
    t = CONF_T
    h = CONF_HALO
    prev_ok = (i > 0).astype(F32)
    next_ok = (i < nt - 1).astype(F32)
    nc = CONF_WIDTH // LANES
    for c in range(nc):
        cs = slice(c * LANES, (c + 1) * LANES)
        u_ref[c, 0:h, :] = up_in[0, :, cs].astype(F32) * prev_ok
        u_ref[c, h:h + t, :] = u_in[0, :, cs].astype(F32)
        u_ref[c, h + t:h + t + h, :] = un_in[0, :, cs].astype(F32) * next_ok

    pad = CONF_KERNEL // 2

    def conv_chunk(c, carry):
        w = w_ref[c]
        bias = b_ref[c]
        for r in range(t // CONF_RC):
            r0 = r * CONF_RC
            acc = jnp.zeros((CONF_RC, LANES), F32) + bias
            for j in range(CONF_KERNEL):
                acc = acc + w[j:j + 1, :] * u_ref[c, r0 + h - pad + j:r0 + h - pad + j + CONF_RC, :]
            y_ref[c, r0:r0 + CONF_RC, :] = acc
        return carry

    lax.fori_loop(0, nc, conv_chunk, 0)

    def ln_chunk(r, carry):
        r0 = pl.multiple_of(r * CONF_RC, CONF_RC)
        ys = [y_ref[c, pl.ds(r0, CONF_RC), :] for c in range(nc)]
        tot = ys[0]
        for y in ys[1:]:
            tot = tot + y
        mu = jnp.sum(tot, axis=-1, keepdims=True) * (1.0 / CONF_WIDTH)
        yc = [y - mu for y in ys]
        sq = yc[0] * yc[0]
        for y in yc[1:]:
            sq = sq + y * y
        inv = lax.rsqrt(jnp.sum(sq, axis=-1, keepdims=True) * (1.0 / CONF_WIDTH) + EPS)
        for c in range(nc):
            cs = slice(c * LANES, (c + 1) * LANES)
            yn = yc[c] * inv * lg_ref[:, cs] + lb_ref[:, cs]
            o_ref[0, pl.ds(r0, CONF_RC), cs] = (yn * _sigmoid(yn)).astype(o_ref.dtype)
        return carry

    lax.fori_loop(0, t // CONF_RC, ln_chunk, 0, unroll=8)


def _conformer(u, dw_w, dw_b, ln_g, ln_b):
    b, l, c = u.shape
    t, h = CONF_T, CONF_HALO
    nt = l // t
    hb = t // h
    last_hb = l // h - 1
    vec = pl.BlockSpec((1, c), lambda bi, i: (0, 0))
    nc = c // LANES
    chunked = lambda rows: pl.BlockSpec((nc, rows, LANES), lambda bi, i: (0, 0, 0))
    w3 = dw_w.reshape(CONF_KERNEL, nc, LANES).transpose(1, 0, 2)
    return pl.pallas_call(
        _conformer_kernel,
        grid=(b, nt),
        in_specs=[pl.BlockSpec((1, t, c), lambda bi, i: (bi, i, 0)),
                  pl.BlockSpec((1, h, c), lambda bi, i: (bi, jnp.maximum(i * hb - 1, 0), 0)),
                  pl.BlockSpec((1, h, c), lambda bi, i: (bi, jnp.minimum((i + 1) * hb, last_hb), 0)),
                  chunked(CONF_KERNEL), chunked(1), vec, vec],
        out_specs=pl.BlockSpec((1, t, c), lambda bi, i: (bi, i, 0)),
        out_shape=jax.ShapeDtypeStruct((b, l, c), BF16),
        scratch_shapes=[pltpu.VMEM((nc, t + 2 * h, LANES), F32), pltpu.VMEM((nc, t, LANES), F32)],
        compiler_params=_params("parallel", "parallel"),
        name="conformer",
    )(u, u, u, w3, dw_b.reshape(nc, 1, LANES), ln_g.reshape(1, c), ln_b.reshape(1, c))


HIN_TM = 1024
HIN_TN = 1024
HIN_HALO = 16
HIN_N1 = HIN_TM // FFT_N2
HIN_PITCH = FFT_N2 + 8


def _hyena_in_kernel(hc_ref, hp_ref, hn_ref, w_ref, cw_ref, cb_ref, o_ref, h_ref, s_ref):
    i = pl.program_id(1)
    j = pl.program_id(2)
    nt = pl.num_programs(1)
    tm, hl = HIN_TM, HIN_HALO

    @pl.when(j == 0)
    def _():
        h_ref[0:hl, :] = jnp.where(i > 0, hp_ref[0], jnp.zeros_like(hp_ref[0]))
        h_ref[hl:hl + tm, :] = hc_ref[0]
        h_ref[hl + tm:hl + tm + hl, :] = jnp.where(i < nt - 1, hn_ref[0], jnp.zeros_like(hn_ref[0]))

    e = jnp.dot(h_ref[...], w_ref[...], preferred_element_type=F32)
    nrows = tm + 2 * hl
    conv = (cw_ref[0:1, :] * pltpu.roll(e, 1, axis=0) + cw_ref[1:2, :] * e
            + cw_ref[2:3, :] * pltpu.roll(e, nrows - 1, axis=0) + cb_ref[...])
    nlc = HIN_TN // LANES
    for r in range(HIN_N1):
        r0 = hl + r * FFT_N2
        for lc in range(nlc):
            s_ref[lc, r * HIN_PITCH:r * HIN_PITCH + FFT_N2, :] = conv[r0:r0 + FFT_N2, lc * LANES:(lc + 1) * LANES]

    def gather(n2, carry):
        for lc in range(nlc):
            rows = s_ref[lc, pl.ds(n2, HIN_N1, stride=HIN_PITCH), :]
            o_ref[0, n2, :, lc * LANES:(lc + 1) * LANES] = rows.astype(o_ref.dtype)
        return carry

    lax.fori_loop(0, FFT_N2, gather, 0, unroll=8)


def _hyena_in(hn, w_in, cw, cb):
    b, l, d = hn.shape
    tm, tn, hl = HIN_TM, HIN_TN, HIN_HALO
    ncol = (HYENA_ORDER + 1) * HYENA_WIDTH // tn
    col0 = 2 * CONF_WIDTH // tn
    hb = tm // hl
    last_hb = l // hl - 1
    return pl.pallas_call(
        _hyena_in_kernel,
        grid=(b, l // tm, ncol),
        in_specs=[pl.BlockSpec((1, tm, d), lambda bi, i, j: (bi, i, 0)),
                  pl.BlockSpec((1, hl, d), lambda bi, i, j: (bi, jnp.maximum(i * hb - 1, 0), 0)),
                  pl.BlockSpec((1, hl, d), lambda bi, i, j: (bi, jnp.minimum((i + 1) * hb, last_hb), 0)),
                  pl.BlockSpec((d, tn), lambda bi, i, j: (0, col0 + j)),
                  pl.BlockSpec((SHORT_KERNEL, tn), lambda bi, i, j: (0, j)),
                  pl.BlockSpec((1, tn), lambda bi, i, j: (0, j))],
        out_specs=pl.BlockSpec((1, FFT_N2, HIN_N1, tn), lambda bi, i, j: (bi, 0, i, j)),
        out_shape=jax.ShapeDtypeStruct((b, FFT_N2, FFT_NZ1, ncol * tn), BF16),
        scratch_shapes=[pltpu.VMEM((tm + 2 * hl, d), BF16),
                        pltpu.VMEM((tn // LANES, HIN_N1 * HIN_PITCH, LANES), F32)],
        compiler_params=_params("parallel", "parallel", "arbitrary"),
        name="hyena_in",
    )(hn, hn, hn, w_in, cw, cb.reshape(1, -1))


MLP_HW = LANES // 2
MLP_W1_ROWS = 40
MLP_WI_ROW0 = MLP_W1_ROWS
MLP_VEC_ROW0 = MLP_WI_ROW0 + HYENA_INNER_MLPS * HYENA_FILTER_ORDER
MLP_ROWS = MLP_VEC_ROW0 + 8


def _filter_mlp_kernel(z_ref, p_ref, o_ref):
    hi = lax.Precision.HIGHEST

    def block_diag(rows0, nrows):
        w = p_ref[rows0:rows0 + nrows, :]
        lo = lax.broadcasted_iota(jnp.int32, w.shape, 1) < MLP_HW
        zeros = jnp.zeros((MLP_HW - nrows, LANES), F32)
        parts = [jnp.where(lo, w, 0.0), zeros, jnp.where(lo, 0.0, w), zeros]
        return jnp.concatenate([p for p in parts if p.shape[0]], axis=0)

    vec = lambda r: p_ref[MLP_VEC_ROW0 + r:MLP_VEC_ROW0 + r + 1, :]
    fr = vec(1 + HYENA_INNER_MLPS)
    w1 = block_diag(0, MLP_W1_ROWS)
    h = jnp.sin(fr * (jnp.dot(z_ref[...], w1, precision=hi, preferred_element_type=F32) + vec(0)))
    for i in range(HYENA_INNER_MLPS):
        wi = block_diag(MLP_WI_ROW0 + i * HYENA_FILTER_ORDER, HYENA_FILTER_ORDER)
        h = jnp.sin(fr * (jnp.dot(h, wi, precision=hi, preferred_element_type=F32) + vec(1 + i)))
    o_ref[...] = h


@functools.lru_cache(maxsize=None)
def _position_features_packed():
    f32 = np.float32
    t = np.linspace(0.0, 1.0, SEQ, dtype=f32)[:, None]
    w = (f32(2.0 * math.pi) * np.arange(SEQ, dtype=f32)[:, None] / f32(SEQ)).astype(f32)
    bands = (HYENA_EMB - 1) // 2
    f = np.linspace(1e-4, bands - 1, bands, dtype=f32)[None, :]
    z = np.concatenate([t, np.cos(f * w), -np.sin(f * w)], axis=-1).astype(f32)
    zp = np.zeros((SEQ // 2, LANES), f32)
    zp[:, :HYENA_EMB] = z[:SEQ // 2]
    zp[:, MLP_HW:MLP_HW + HYENA_EMB] = z[SEQ // 2:]
    return jnp.asarray(zp)


def _filter_mlp(w1, b1, w_inner, b_inner, freq):
    both = lambda a: jnp.concatenate([a, a], axis=-1)
    params = jnp.concatenate(
        [both(w1), jnp.zeros((MLP_W1_ROWS - HYENA_EMB, LANES), F32)]
        + [both(w_inner[i]) for i in range(HYENA_INNER_MLPS)]
        + [both(b1)[None], both(b_inner), both(freq)[None],
           jnp.zeros((MLP_ROWS - MLP_VEC_ROW0 - 2 - HYENA_INNER_MLPS, LANES), F32)], axis=0)
    return pl.pallas_call(
        _filter_mlp_kernel,
        out_shape=jax.ShapeDtypeStruct((SEQ // 2, LANES), F32),
        compiler_params=pltpu.CompilerParams(vmem_limit_bytes=VMEM_LIMIT),
        name="filter_mlp",
    )(_position_features_packed(), params)


def _stage_b_rhs(a_ref, k1):
    ar = a_ref[pl.ds(k1, FFT_N2, stride=FFT_PITCH), :]
    ai = a_ref[pl.ds(FFT_N1 + k1, FFT_N2, stride=FFT_PITCH), :]
    return jnp.concatenate([ar, ai], axis=0).astype(BF16)


def _slab_rows(n2):
    return pl.ds(pl.multiple_of(n2 * FFT_PITCH, 8), 2 * FFT_N1)


def _filter_spectrum_kernel(h_ref, wf_ref, wb_ref, dl_ref, fak_ref, t_ref, *refs, n_cast):
    cast_in, o_ref, cast_out = refs[:n_cast], refs[n_cast], refs[n_cast + 1:2 * n_cast + 1]
    kt_ref, a_ref = refs[2 * n_cast + 1:]
    _cast_blocks(cast_in, cast_out)
    rc = FILTER_RC
    half_chunks = SEQ // 2 // rc
    w = jnp.concatenate([wf_ref[...], wb_ref[...]], axis=1)
    zeros = jnp.zeros((MLP_HW, 2 * LANES), F32)
    ws = [jnp.concatenate([w, zeros], axis=0).astype(BF16), jnp.concatenate([zeros, w], axis=0).astype(BF16)]
    dl = jnp.concatenate([dl_ref[...], dl_ref[...]], axis=1)

    def gen(r, ss, p):
        r0 = pl.multiple_of(r * rc, rc)
        rows = lax.broadcasted_iota(jnp.int32, (rc, 2 * LANES), 0) + r0
        lanes = lax.broadcasted_iota(jnp.int32, (rc, 2 * LANES), 1)
        decay = jnp.exp(-(rows.astype(F32) * (1.0 / (SEQ - 1))) * dl)
        h = h_ref[pl.ds(pl.multiple_of(r0 - p * (SEQ // 2), rc), rc), :]
        k = jnp.dot(h.astype(BF16), ws[p], preferred_element_type=F32)
        k = k * decay
        k = jnp.where((rows >= SEQ - 1) & (lanes >= LANES), 0.0, k)
        for n in range(rc // FFT_N2):
            dst = pl.ds(pl.multiple_of((r * (rc // FFT_N2) + n) * FILTER_PITCH, 8), FFT_N2)
            kt_ref[0, dst, :] = k[n * FFT_N2:(n + 1) * FFT_N2, :LANES]
            kt_ref[1, dst, :] = k[n * FFT_N2:(n + 1) * FFT_N2, LANES:]
        return ss + jnp.sum(k * k, axis=0, keepdims=True)

    ss = jnp.zeros((1, 2 * LANES), F32)
    for p in range(2):
        ss = lax.fori_loop(p * half_chunks, (p + 1) * half_chunks, functools.partial(gen, p=p), ss, unroll=4)
    scale = lax.rsqrt(ss[:, :LANES] + ss[:, LANES:] + EPS)

    def filter_slab(n2):
        fwd = kt_ref[0, pl.ds(n2, FFT_NZ1, stride=FILTER_PITCH), :]
        bwd = kt_ref[1, pl.ds(FFT_N2 - 1 - n2, FFT_NZ1, stride=FILTER_PITCH), :]
        return (jnp.concatenate([fwd, bwd], axis=0) * scale).astype(BF16)

    def stage_a(p, carry):
        n2 = 2 * p
        rhs = jnp.concatenate([filter_slab(n2), filter_slab(n2 + 1)], axis=1)
        a = jnp.dot(fak_ref[...], rhs, preferred_element_type=F32)
        a_ref[_slab_rows(n2), :] = a[:, :LANES]
        a_ref[_slab_rows(n2 + 1), :] = a[:, LANES:]
        return carry

    lax.fori_loop(0, FFT_N2 // 2, stage_a, 0, unroll=min(FFT_UNROLL // 2, FFT_N2 // 2))

    def stage_b(k1, carry):
        b = jnp.dot(t_ref[k1], _stage_b_rhs(a_ref, k1), preferred_element_type=F32)
        o_ref[0, k1] = b.astype(o_ref.dtype)
        return carry

    lax.fori_loop(0, FFT_N1, stage_b, 0, unroll=FILTER_UNROLL)


def _filter_spectrum(hmlp, w_filt, fak, t, cast_windows):
    assert HYENA_FILTER_ORDER == MLP_HW
    ncb = HYENA_WIDTH // LANES
    grid = (HYENA_ORDER, ncb)
    cast_weights, cast_in, cast_out, cast_shapes = _cast_specs(cast_windows, grid)
    w_spec = lambda direction: pl.BlockSpec((HYENA_FILTER_ORDER, LANES),
                                            lambda o, c: (0, (o * 2 + direction) * ncb + c))
    outs = pl.pallas_call(
        functools.partial(_filter_spectrum_kernel, n_cast=len(cast_weights)),
        grid=grid,
        in_specs=[_const_spec((SEQ // 2, LANES)), w_spec(0), w_spec(1),
                  pl.BlockSpec((1, LANES), lambda o, c: (0, c)),
                  _const_spec((2 * FFT_N1, FFT_N1)),
                  _const_spec((FFT_N1, 2 * FFT_N2, 2 * FFT_N2))] + cast_in,
        out_specs=[pl.BlockSpec((1, FFT_N1, 2 * FFT_N2, LANES), lambda o, c: (o, 0, 0, c))] + cast_out,
        out_shape=[jax.ShapeDtypeStruct((HYENA_ORDER, FFT_N1, 2 * FFT_N2, HYENA_WIDTH), BF16)] + cast_shapes,
        scratch_shapes=[pltpu.VMEM((2, FFT_NZ1 * FILTER_PITCH, LANES), F32),
                        pltpu.VMEM((FFT_N2 * FFT_PITCH, LANES), F32)],
        compiler_params=_params("arbitrary", "arbitrary"),
        name="filter_spectrum",
    )(hmlp, w_filt, w_filt, _decay_rates(), fak, t, *cast_weights)
    return outs[0], outs[1:]


def _hyena_conv_kernel(v_ref, xm_ref, kf_ref, sk_ref, fa_ref, t_ref, ti_ref, fd_ref, o_ref, a_ref, y_ref, z1_ref):
    def conv(u_ref, write):
        def stage_a(p, carry):
            n2 = 2 * p
            rhs = jnp.concatenate(
                [jnp.concatenate([u_ref[0, n], u_ref[1, n]], axis=0) for n in (n2, n2 + 1)], axis=1).astype(BF16)
            a = jnp.dot(fa_ref[...], rhs, preferred_element_type=F32)
            a_ref[_slab_rows(n2), :] = a[:, :LANES]
            a_ref[_slab_rows(n2 + 1), :] = a[:, LANES:]
            return carry

        lax.fori_loop(0, FFT_N2 // 2, stage_a, 0, unroll=min(FFT_UNROLL // 2, FFT_N2 // 2))

        def stage_b(k1, carry):
            b = jnp.dot(t_ref[k1], _stage_b_rhs(a_ref, k1), preferred_element_type=F32)
            kf = kf_ref[0, k1].astype(F32)
            br, bi = b[:FFT_N2], b[FFT_N2:]
            kr, ki = kf[:FFT_N2], kf[FFT_N2:]
            y_ref[k1] = jnp.concatenate([br * kr - bi * ki, br * ki + bi * kr], axis=0).astype(BF16)
            return carry

        lax.fori_loop(0, FFT_N1, stage_b, 0, unroll=FFT_UNROLL)

        def stage_c(k1, carry):
            c = jnp.dot(ti_ref[k1], y_ref[k1], preferred_element_type=F32)
            a_ref[pl.ds(k1, FFT_N2, stride=FFT_PITCH), :] = c[:FFT_N2]
            a_ref[pl.ds(FFT_N1 + k1, FFT_N2, stride=FFT_PITCH), :] = c[FFT_N2:]
            return carry

        lax.fori_loop(0, FFT_N1, stage_c, 0, unroll=FFT_UNROLL)

        sk = sk_ref[0]

        def stage_d(p, carry):
            n2 = 2 * p
            cb = jnp.concatenate([a_ref[_slab_rows(n2), :], a_ref[_slab_rows(n2 + 1), :]], axis=1).astype(BF16)
            y = jnp.dot(fd_ref[...], cb, preferred_element_type=F32)
            for s in range(2):
                for bi in range(2):
                    yb = y[bi * FFT_NZ1:(bi + 1) * FFT_NZ1, s * LANES:(s + 1) * LANES]
                    write(bi, n2 + s, xm_ref[bi, n2 + s].astype(F32) * (yb + sk * u_ref[bi, n2 + s].astype(F32)))
            return carry

        lax.fori_loop(0, FFT_N2 // 2, stage_d, 0, unroll=min(FFT_UNROLL // 2, FFT_N2 // 2))

    def write_z1(bi, n2, z):
        z1_ref[bi, n2] = z.astype(z1_ref.dtype)

    def write_out(bi, n2, z):
        o_ref[bi, pl.ds(n2, FFT_NZ1, stride=FFT_N2), :] = z

    order = pl.program_id(1)

    @pl.when(order == 0)
    def _():
        conv(v_ref, write_z1)

    @pl.when(order == 1)
    def _():
        conv(z1_ref, write_out)


def _hyena_conv(uc, kf, skip, consts):
    fa, _, t, ti, fd = consts
    ncb = HYENA_WIDTH // LANES
    slab_block = (BATCH, FFT_N2, FFT_NZ1, LANES)
    return pl.pallas_call(
        _hyena_conv_kernel,
        grid=(ncb, HYENA_ORDER),
        in_specs=[pl.BlockSpec(slab_block, lambda c, o: (0, 0, 0, c)),
                  pl.BlockSpec(slab_block, lambda c, o: (0, 0, 0, (o + 1) * ncb + c)),
                  pl.BlockSpec((1, FFT_N1, 2 * FFT_N2, LANES), lambda c, o: (o, 0, 0, c)),
                  pl.BlockSpec((1, 1, LANES), lambda c, o: (o, 0, c)),
                  _const_spec(fa.shape), _const_spec(t.shape), _const_spec(ti.shape), _const_spec(fd.shape)],
        out_specs=pl.BlockSpec((BATCH, SEQ, LANES), lambda c, o: (0, 0, c)),
        out_shape=jax.ShapeDtypeStruct((BATCH, SEQ, HYENA_WIDTH), F32),
        scratch_shapes=[pltpu.VMEM((FFT_N2 * FFT_PITCH, LANES), F32),
                        pltpu.VMEM((FFT_N1, 2 * FFT_N2, LANES), BF16),
                        pltpu.VMEM(slab_block, BF16)],
        compiler_params=_params("arbitrary", "arbitrary"),
        name="hyena_conv",
    )(uc, uc, kf, skip.reshape(HYENA_ORDER, 1, HYENA_WIDTH), fa, t, ti, fd)


MA_TM = 256
MA_VMEM_LIMIT = 60 * 1024 * 1024


def _mix_attn_kernel(x_ref, yc_ref, yh_ref, wa_ref, wb_ref, g_ref, wq_ref, k_ref, v_ref, wo_ref, *refs, n_cast):
    cast_in, o_ref, cast_out = refs[:n_cast], refs[n_cast], refs[n_cast + 1:2 * n_cast + 1]
    x1_ref, q_ref, ctx_ref = refs[2 * n_cast + 1:]
    _cast_blocks(cast_in, cast_out)
    mix = jnp.dot(yc_ref[0], wa_ref[...], preferred_element_type=F32)
    mix = mix + jnp.dot(yh_ref[0].astype(BF16), wb_ref[...], preferred_element_type=F32)
    x1 = x_ref[0] + mix
    x1_ref[...] = x1
    q_ref[...] = jnp.dot(_rms(x1, g_ref[...]).astype(BF16), wq_ref[...],
                         preferred_element_type=F32).astype(BF16)
    hd = XATTN_HEAD_DIM
    for h in range(XATTN_HEADS):
        hs = slice(h * hd, (h + 1) * hd)
        s = lax.dot_general(q_ref[:, hs], k_ref[0, :, hs], (((1,), (1,)), ((), ())),
                            preferred_element_type=F32) * (hd ** -0.5)
        s = s - jnp.max(s, axis=-1, keepdims=True)
        e = jnp.exp(s)
        pr = e / jnp.sum(e, axis=-1, keepdims=True)
        ctx_ref[:, hs] = jnp.dot(pr.astype(BF16), v_ref[0, :, hs], preferred_element_type=F32).astype(BF16)
    o_ref[0] = x1_ref[...] + jnp.dot(ctx_ref[...], wo_ref[...], preferred_element_type=F32)


def _mix_attn(x, yc, yh, w_out, g, wq, k, v, wo, cast_windows):
    b, l, d = x.shape
    tm = MA_TM
    kc = yc.shape[-1]
    grid = (b, l // tm)
    tok = lambda width: pl.BlockSpec((1, tm, width), lambda bi, i: (bi, i, 0))
    mem = pl.BlockSpec((1, N_MEM, d), lambda bi, i: (bi, 0, 0))
    cast_weights, cast_in, cast_out, cast_shapes = _cast_specs(cast_windows, grid)
    outs = pl.pallas_call(
        functools.partial(_mix_attn_kernel, n_cast=len(cast_weights)),
        grid=grid,
        in_specs=[tok(d), tok(kc), tok(kc),
                  _const_spec((kc, d), (0, 0)), _const_spec((kc, d), (1, 0)),
                  pl.BlockSpec((1, d), lambda bi, i: (0, 0)),
                  _const_spec((d, d)), mem, mem, _const_spec((d, d))] + cast_in,
        out_specs=[tok(d)] + cast_out,
        out_shape=[jax.ShapeDtypeStruct((b, l, d), F32)] + cast_shapes,
        scratch_shapes=[pltpu.VMEM((tm, d), F32), pltpu.VMEM((tm, d), BF16), pltpu.VMEM((tm, d), BF16)],
        compiler_params=_params("arbitrary", "arbitrary", vmem_limit=MA_VMEM_LIMIT),
        name="mix_attn",
    )(x, yc, yh, w_out, w_out, g.reshape(1, d), wq, k, v, wo, *cast_weights)
    return outs[0], outs[1:]


FFN_TM = 1024
FFN_FC = 512
FFN_VMEM_LIMIT = 62 * 1024 * 1024
FFN_HALO = 16


def _ffn_kernel(x_ref, xp_ref, xn_ref, g_ref, wg_ref, wv_ref, cw_ref, cb_ref, wd_ref, gf_ref,
                o_ref, h_ref):
    i = pl.program_id(1)
    j = pl.program_id(2)
    nt = pl.num_programs(1)
    nf = pl.num_programs(2)
    tm, hl = FFN_TM, FFN_HALO

    @pl.when(j == 0)
    def _():
        g = g_ref[...]
        h_ref[0:hl, :] = (_rms(xp_ref[0], g) * (i > 0).astype(F32)).astype(BF16)
        h_ref[hl:hl + tm, :] = _rms(x_ref[0], g).astype(BF16)
        h_ref[hl + tm:hl + tm + hl, :] = (_rms(xn_ref[0], g) * (i < nt - 1).astype(F32)).astype(BF16)
        o_ref[...] = jnp.zeros_like(o_ref)

    ge = jnp.dot(h_ref[...], wg_ref[...], preferred_element_type=F32)
    val = jnp.dot(h_ref[pl.ds(hl, tm), :], wv_ref[...], preferred_element_type=F32)
    nrows = tm + 2 * hl
    gate = (cw_ref[0:1, :] * pltpu.roll(ge, 1, axis=0) + cw_ref[1:2, :] * ge
            + cw_ref[2:3, :] * pltpu.roll(ge, nrows - 1, axis=0) + cb_ref[...])[hl:hl + tm]
    act = (gate * _sigmoid(gate) * val).astype(BF16)
    o_ref[0] += jnp.dot(act, wd_ref[...], preferred_element_type=F32)

    @pl.when(j == nf - 1)
    def _():
        o_ref[0] = _rms(x_ref[0] + o_ref[0], gf_ref[...])


def _ffn(x, g_ffn, w_up, dw_w, dw_b, w_down, g_final):
    b, l, d = x.shape
    tm, fc, hl = FFN_TM, FFN_FC, FFN_HALO
    nf = D_FF // fc
    hb = tm // hl
    last_hb = l // hl - 1
    return pl.pallas_call(
        _ffn_kernel,
        grid=(b, l // tm, nf),
        in_specs=[pl.BlockSpec((1, tm, d), lambda bi, i, j: (bi, i, 0)),
                  pl.BlockSpec((1, hl, d), lambda bi, i, j: (bi, jnp.maximum(i * hb - 1, 0), 0)),
                  pl.BlockSpec((1, hl, d), lambda bi, i, j: (bi, jnp.minimum((i + 1) * hb, last_hb), 0)),
                  pl.BlockSpec((1, d), lambda bi, i, j: (0, 0)),
                  pl.BlockSpec((d, fc), lambda bi, i, j: (0, j)),
                  pl.BlockSpec((d, fc), lambda bi, i, j: (0, nf + j)),
                  pl.BlockSpec((3, fc), lambda bi, i, j: (0, j)),
                  pl.BlockSpec((1, fc), lambda bi, i, j: (0, j)),
                  pl.BlockSpec((fc, d), lambda bi, i, j: (j, 0)),
                  pl.BlockSpec((1, d), lambda bi, i, j: (0, 0))],
        out_specs=pl.BlockSpec((1, tm, d), lambda bi, i, j: (bi, i, 0)),
        out_shape=jax.ShapeDtypeStruct((b, l, d), F32),
        scratch_shapes=[pltpu.VMEM((tm + 2 * hl, d), BF16)],
        compiler_params=_params("parallel", "parallel", "arbitrary", vmem_limit=FFN_VMEM_LIMIT),
        name="ffn",
    )(x, x, x, g_ffn.reshape(1, d), w_up, w_up, dw_w, dw_b.reshape(1, -1), w_down, g_final.reshape(1, d))


def kernel(x, mem, g_mix, w_in, conf_dw_w, conf_dw_b, conf_ln_g, conf_ln_b, hyena_short_w, hyena_short_b, hyena_w1, hyena_b1, hyena_w_inner, hyena_b_inner, hyena_w_filt, hyena_sin_freq, hyena_skip, w_out, g_xattn, g_mem, w_q, w_k, w_v, w_o, g_ffn, w_ffn_up, ffn_dw_w, ffn_dw_b, w_ffn_down, g_final):
    b, l, d = x.shape
    consts = _dft_constants()

    hmlp = _filter_mlp(hyena_w1[0], hyena_b1[0], hyena_w_inner[0], hyena_b_inner[0], hyena_sin_freq[0])
    kf, (w_in_b, w_out_b, w_q_b, w_o_b) = _filter_spectrum(
        hmlp, hyena_w_filt[0], consts[1], consts[2], [w_in[0], w_out[0], w_q[0], w_o[0]])

    u_conf, hn = _conf_in(x.reshape(b * l, d), g_mix[0], w_in_b)
    y_conf = _conformer(u_conf.reshape(b, l, CONF_WIDTH), conf_dw_w[0], conf_dw_b[0], conf_ln_g[0],
                        conf_ln_b[0])

    uc = _hyena_in(hn.reshape(b, l, d), w_in_b, hyena_short_w[0], hyena_short_b[0])
    z2 = _hyena_conv(uc, kf, hyena_skip[0], consts)

    memf = mem.reshape(b * N_MEM, d)
    k = _norm_matmul(memf, g_mem[0], w_k[0], b * N_MEM, 1024, BF16).reshape(b, N_MEM, d)
    v = _norm_matmul(memf, g_mem[0], w_v[0], b * N_MEM, 1024, BF16).reshape(b, N_MEM, d)
    x2, (w_up_b, w_down_b) = _mix_attn(x, y_conf, z2, w_out_b, g_xattn[0], w_q_b, k, v, w_o_b,
                                       [w_ffn_up[0], w_ffn_down[0]])

    return _ffn(x2, g_ffn[0], w_up_b, ffn_dw_w[0], ffn_dw_b[0], w_down_b, g_final)
```
